```python
import math, functools
import jax, jax.numpy as jnp
from jax import lax
import numpy as np

D_MODEL = 1024
BATCH = 1
SEQ = 16384
DEPTH = 1
DEC_BATCH = 32
DEC_SEQ = 4
PAST_LEN = 16384
PAGE_SIZE = 128

N_HEADS = 8
HEAD_DIM = 64
V_DIM = 2 * HEAD_DIM
QK_WIDTH = N_HEADS * 2 * HEAD_DIM
ATTN_WIDTH = N_HEADS * V_DIM
ROPE_THETA = 10000.0
Q_BLOCK = 128
SSM_WIDTH = 512
SSM_GROUP = 16
N_GROUPS = SSM_WIDTH // SSM_GROUP
STATE_DIM = 64
DT_MIN = 1e-3
DT_MAX = 1e-1
N_KEYS = 128
N_EXPERTS = N_KEYS * N_KEYS
PEER_HEADS = 8
PEER_KEY_DIM = 128
PEER_TOPK = 16
PEER_BLOCK = 128
IN_COLS = 2 * QK_WIDTH + ATTN_WIDTH + SSM_WIDTH + 2 * D_MODEL
SPLITS = [QK_WIDTH, 2 * QK_WIDTH, 2 * QK_WIDTH + ATTN_WIDTH,
          2 * QK_WIDTH + ATTN_WIDTH + SSM_WIDTH,
          2 * QK_WIDTH + ATTN_WIDTH + SSM_WIDTH + D_MODEL]
N_MOD = 6
RMS_EPS = 1e-6

kernel_name = "hybrid_s5_diffattn_peer_step"

F32 = jnp.float32


def rms_norm(x, w):
    xf = x.astype(F32)
    y = xf * lax.rsqrt(jnp.mean(xf * xf, axis=-1, keepdims=True) + RMS_EPS)
    return (y * w.astype(F32)).astype(x.dtype)


def rotary(x, pos):
    half = HEAD_DIM // 2
    inv = ROPE_THETA ** (-jnp.arange(half, dtype=F32) / half)
    ang = pos.astype(F32)[:, None] * inv[None, :]
    cos = jnp.cos(ang)[None, :, None, None, :]
    sin = jnp.sin(ang)[None, :, None, None, :]
    x1, x2 = x[..., :half], x[..., half:]
    return jnp.concatenate([x1 * cos - x2 * sin, x2 * cos + x1 * sin], axis=-1)


def diff_attn_core(q, k, v, q_pos, k_pos, lam):
    s = jnp.einsum('bqhcd,bkhcd->bhcqk', q, k.astype(F32)) * (HEAD_DIM ** -0.5)
    mask = k_pos[None, :] <= q_pos[:, None]
    s = jnp.where(mask, s, -jnp.inf)
    p = jax.nn.softmax(s, axis=-1)
    a = p[:, :, 0] - lam * p[:, :, 1]
    return jnp.einsum('bhqk,bkhe->bqhe', a, v.astype(F32))


def prompt_attend(q, k, v, lam):
    B, T = q.shape[0], q.shape[1]
    nb = T // Q_BLOCK
    pos = jnp.arange(T, dtype=jnp.int32)
    qb = q.reshape(B, nb, Q_BLOCK, N_HEADS, 2, HEAD_DIM).swapaxes(0, 1)
    pb = pos.reshape(nb, Q_BLOCK)
    out = lax.map(lambda a: diff_attn_core(a[0], k, v, a[1], pos, lam), (qb, pb))
    return out.swapaxes(0, 1).reshape(B, T, N_HEADS, V_DIM)


def sample_attend(q, k, v, lam, q_pos, cache_k, cache_v, page_table, layer):
    Tn = q.shape[1]
    past = page_table.shape[1] * PAGE_SIZE
    k_pos = jnp.arange(past + Tn, dtype=jnp.int32)

    def one(args):
        qi, ki, vi, pages = args
        kp = cache_k[layer, pages].reshape(past, N_HEADS, 2, HEAD_DIM).astype(F32)
        vp = cache_v[layer, pages].reshape(past, N_HEADS, V_DIM).astype(F32)
        kk = jnp.concatenate([kp, ki], axis=0)[None]
        vv = jnp.concatenate([vp, vi.astype(F32)], axis=0)[None]
        return diff_attn_core(qi[None], kk, vv, q_pos, k_pos, lam)[0]

    return lax.map(one, (q, k, v, page_table))


def _cmul(ar, ai, br, bi):
    return ar * br - ai * bi, ar * bi + ai * br


def _ssm_combine(e1, e2):
    a1r, a1i, b1r, b1i = e1
    a2r, a2i, b2r, b2i = e2
    ar, ai = _cmul(a2r, a2i, a1r, a1i)
    br, bi = _cmul(a2r, a2i, b1r, b1i)
    return ar, ai, br + b2r, bi + b2i


def s5_branch(u, h0_re, h0_im, lam_re, lam_im, log_step, b_re, b_im, c_re, c_im, d, w_glu):
    B, T = u.shape[0], u.shape[1]
    uf = u.astype(F32).reshape(B, T, N_GROUPS, SSM_GROUP)
    dt = jnp.exp(log_step.astype(F32))[:, None]
    lr, li = lam_re.astype(F32), lam_im.astype(F32)
    mag = jnp.exp(lr * dt)
    a_re, a_im = mag * jnp.cos(li * dt), mag * jnp.sin(li * dt)
    den = lr * lr + li * li
    nr, ni = a_re - 1.0, a_im
    f_re = (nr * lr + ni * li) / den
    f_im = (ni * lr - nr * li) / den
    br, bi = b_re.astype(F32), b_im.astype(F32)
    bb_re = f_re[..., None] * br - f_im[..., None] * bi
    bb_im = f_re[..., None] * bi + f_im[..., None] * br
    bu_re = jnp.einsum('btgi,gpi->btgp', uf, bb_re)
    bu_im = jnp.einsum('btgi,gpi->btgp', uf, bb_im)
    ar = jnp.broadcast_to(a_re, bu_re.shape)
    ai = jnp.broadcast_to(a_im, bu_re.shape)
    cr, ci, xr, xi = lax.associative_scan(_ssm_combine, (ar, ai, bu_re, bu_im), axis=1)
    h0r = h0_re.astype(F32)[:, None]
    h0i = h0_im.astype(F32)[:, None]
    hr = cr * h0r - ci * h0i + xr
    hi = cr * h0i + ci * h0r + xi
    y = (jnp.einsum('btgp,gip->btgi', hr, c_re.astype(F32))
         - jnp.einsum('btgp,gip->btgi', hi, c_im.astype(F32))
         + d.astype(F32).reshape(N_GROUPS, SSM_GROUP) * uf)
    y = jax.nn.gelu(y.reshape(B, T, SSM_WIDTH))
    y = y * jax.nn.sigmoid(y @ w_glu.astype(F32))
    return y, hr[:, -1], hi[:, -1]


def peer_ffn(h, w_query, sub_keys, expert_u, expert_v):
    N = h.shape[0]
    nb = -(-N // PEER_BLOCK)
    hp = jnp.pad(h, ((0, nb * PEER_BLOCK - N), (0, 0))).reshape(nb, PEER_BLOCK, D_MODEL)
    wq = w_query.astype(F32)
    sk = sub_keys.astype(F32)

    def block(hb):
        n = hb.shape[0]
        q = (hb @ wq).reshape(n, PEER_HEADS, 2, PEER_KEY_DIM // 2)
        s = jnp.einsum('nhcd,hckd->nhck', q, sk)
        sv, si = lax.top_k(s, PEER_TOPK)
        cand = (sv[:, :, 0, :, None] + sv[:, :, 1, None, :]).reshape(n, PEER_HEADS, PEER_TOPK * PEER_TOPK)
        cidx = (si[:, :, 0, :, None] * N_KEYS + si[:, :, 1, None, :]).reshape(n, PEER_HEADS, PEER_TOPK * PEER_TOPK)
        top_s, sel = lax.top_k(cand, PEER_TOPK)
        eidx = jnp.take_along_axis(cidx, sel, axis=-1)
        g = jax.nn.softmax(top_s, axis=-1)
        u = expert_u[eidx].astype(F32)
        v = expert_v[eidx].astype(F32)
        act = jax.nn.gelu(jnp.einsum('nhkd,nd->nhk', u, hb))
        return jnp.einsum('nhk,nhkd->nd', g * act, v)

    out = lax.map(block, hp)
    return out.reshape(nb * PEER_BLOCK, D_MODEL)[:N]


def trunk_layer(x, c, pos, attend, h0_re, h0_im, lam_init, w):
    B, T = x.shape[0], x.shape[1]
    mod = jax.nn.silu(c.astype(F32)) @ w['w_ada'].astype(F32) + w['b_ada'].astype(F32)
    sh1, sc1, g1, sh2, sc2, g2 = jnp.split(mod[:, None, :], N_MOD, axis=-1)

    h = rms_norm(x, w['norm_pre_mix']).astype(F32) * (1.0 + sc1) + sh1
    z = h @ w['w_in'].astype(F32)
    q, k, v, u, ga, gs = jnp.split(z, SPLITS, axis=-1)
    q = rotary(q.reshape(B, T, N_HEADS, 2, HEAD_DIM), pos)
    k = rotary(k.reshape(B, T, N_HEADS, 2, HEAD_DIM), pos)
    v = v.reshape(B, T, N_HEADS, V_DIM)
    lam = (jnp.exp(jnp.sum(w['lambda_q1'].astype(F32) * w['lambda_k1'].astype(F32)))
           - jnp.exp(jnp.sum(w['lambda_q2'].astype(F32) * w['lambda_k2'].astype(F32))) + lam_init)
    o = attend(q, k, v, lam)
    o = rms_norm(o, w['subln']) * (1.0 - lam_init)
    y_attn = o.reshape(B, T, ATTN_WIDTH) @ w['w_attn_out'].astype(F32)
    y_s, hr, hi = s5_branch(u, h0_re, h0_im, w['ssm_lambda_re'], w['ssm_lambda_im'], w['ssm_log_step'],
                            w['ssm_b_re'], w['ssm_b_im'], w['ssm_c_re'], w['ssm_c_im'], w['ssm_d'], w['w_glu'])
    y_ssm = y_s @ w['w_ssm_out'].astype(F32)
    merged = jax.nn.sigmoid(ga) * y_attn + jax.nn.sigmoid(gs) * y_ssm
    mix = merged @ w['w_out'].astype(F32)
    x = x + (g1 * rms_norm(mix, w['norm_post_mix'])).astype(x.dtype)

    h2 = rms_norm(x, w['norm_pre_ffn']).astype(F32) * (1.0 + sc2) + sh2
    f = peer_ffn(h2.reshape(B * T, D_MODEL), w['peer_w_query'], w['peer_sub_keys'],
                 w['peer_u'], w['peer_v']).reshape(B, T, D_MODEL)
    x = x + (g2 * rms_norm(f, w['norm_post_ffn'])).astype(x.dtype)
    return x, k.reshape(B, T, N_HEADS, 2 * HEAD_DIM), v, hr, hi


def setup_inputs(seed: int = 0) -> dict:
    key = jax.random.key(seed)
    keys = jax.random.split(key, 48)
    counter = iter(range(48))

    def nk():
        return keys[next(counter)]

    def nrm(shape, scale):
        return jax.random.normal(nk(), shape, F32) * scale

    n_pages = PAST_LEN // PAGE_SIZE
    n_used = DEC_BATCH * n_pages
    n_phys = n_used + max(1, n_used // 4)
    perm = jax.random.permutation(nk(), n_phys)
    page_table = perm[:n_used].reshape(DEC_BATCH, n_pages).astype(jnp.int32)
    Dm = D_MODEL
    return {
        "x_prompt": nrm((BATCH, SEQ, Dm), 1.0),
        "x_sample": nrm((DEC_BATCH, DEC_SEQ, Dm), 1.0),
        "cache_k": nrm((DEPTH, n_phys, PAGE_SIZE, N_HEADS, 2 * HEAD_DIM), 1.0),
        "cache_v": nrm((DEPTH, n_phys, PAGE_SIZE, N_HEADS, V_DIM), 1.0),
        "state_ssm_re": nrm((DEPTH, DEC_BATCH, N_GROUPS, STATE_DIM), 0.5),
        "state_ssm_im": nrm((DEPTH, DEC_BATCH, N_GROUPS, STATE_DIM), 0.5),
        "page_table": page_table,
        "c_prompt": nrm((BATCH, Dm), 1.0),
        "c_sample": nrm((DEC_BATCH, Dm), 1.0),
        "w_ada": nrm((DEPTH, Dm, N_MOD * Dm), 0.5 * Dm ** -0.5),
        "b_ada": nrm((DEPTH, N_MOD * Dm), 0.02),
        "norm_pre_mix": 1.0 + nrm((DEPTH, Dm), 0.02),
        "norm_post_mix": 1.0 + nrm((DEPTH, Dm), 0.02),
        "norm_pre_ffn": 1.0 + nrm((DEPTH, Dm), 0.02),
        "norm_post_ffn": 1.0 + nrm((DEPTH, Dm), 0.02),
        "w_in": nrm((DEPTH, Dm, IN_COLS), Dm ** -0.5),
        "lambda_q1": nrm((DEPTH, HEAD_DIM), 0.1),
        "lambda_k1": nrm((DEPTH, HEAD_DIM), 0.1),
        "lambda_q2": nrm((DEPTH, HEAD_DIM), 0.1),
        "lambda_k2": nrm((DEPTH, HEAD_DIM), 0.1),
        "subln": 1.0 + nrm((DEPTH, V_DIM), 0.02),
        "w_attn_out": nrm((DEPTH, ATTN_WIDTH, Dm), ATTN_WIDTH ** -0.5),
        "ssm_lambda_re": -0.5 + nrm((DEPTH, N_GROUPS, STATE_DIM), 0.01),
        "ssm_lambda_im": jnp.pi * jnp.arange(STATE_DIM, dtype=F32) + nrm((DEPTH, N_GROUPS, STATE_DIM), 0.01),
        "ssm_log_step": jax.random.uniform(nk(), (DEPTH, N_GROUPS), F32, math.log(DT_MIN), math.log(DT_MAX)),
        "ssm_b_re": nrm((DEPTH, N_GROUPS, STATE_DIM, SSM_GROUP), SSM_GROUP ** -0.5),
        "ssm_b_im": nrm((DEPTH, N_GROUPS, STATE_DIM, SSM_GROUP), SSM_GROUP ** -0.5),
        "ssm_c_re": nrm((DEPTH, N_GROUPS, SSM_GROUP, STATE_DIM), STATE_DIM ** -0.5),
        "ssm_c_im": nrm((DEPTH, N_GROUPS, SSM_GROUP, STATE_DIM), STATE_DIM ** -0.5),
        "ssm_d": nrm((DEPTH, SSM_WIDTH), 1.0),
        "w_glu": nrm((DEPTH, SSM_WIDTH, SSM_WIDTH), SSM_WIDTH ** -0.5),
        "w_ssm_out": nrm((DEPTH, SSM_WIDTH, Dm), SSM_WIDTH ** -0.5),
        "w_out": nrm((DEPTH, Dm, Dm), Dm ** -0.5),
        "peer_w_query": nrm((DEPTH, Dm, PEER_HEADS * PEER_KEY_DIM), Dm ** -0.5),
        "peer_sub_keys": nrm((DEPTH, PEER_HEADS, 2, N_KEYS, PEER_KEY_DIM // 2), (PEER_KEY_DIM // 2) ** -0.5),
        "peer_u": nrm((DEPTH, N_EXPERTS, Dm), Dm ** -0.5),
        "peer_v": nrm((DEPTH, N_EXPERTS, Dm), 1.0),
    }


def reference(x_prompt, x_sample, cache_k, cache_v, state_ssm_re, state_ssm_im, page_table,
              c_prompt, c_sample, w_ada, b_ada, norm_pre_mix, norm_post_mix, norm_pre_ffn,
              norm_post_ffn, w_in, lambda_q1, lambda_k1, lambda_q2, lambda_k2, subln, w_attn_out,
              ssm_lambda_re, ssm_lambda_im, ssm_log_step, ssm_b_re, ssm_b_im, ssm_c_re, ssm_c_im,
              ssm_d, w_glu, w_ssm_out, w_out, peer_w_query, peer_sub_keys, peer_u, peer_v):
    pos_prompt = jnp.arange(x_prompt.shape[1], dtype=jnp.int32)
    pos_sample = PAST_LEN + jnp.arange(x_sample.shape[1], dtype=jnp.int32)
    yp, ys = x_prompt, x_sample
    kp_l, vp_l, hrp_l, hip_l, ks_l, vs_l, hrs_l, his_l = [], [], [], [], [], [], [], []
    for l in range(DEPTH):
        w = {
            'w_ada': w_ada[l], 'b_ada': b_ada[l],
            'norm_pre_mix': norm_pre_mix[l], 'norm_post_mix': norm_post_mix[l],
            'norm_pre_ffn': norm_pre_ffn[l], 'norm_post_ffn': norm_post_ffn[l],
            'w_in': w_in[l], 'lambda_q1': lambda_q1[l], 'lambda_k1': lambda_k1[l],
            'lambda_q2': lambda_q2[l], 'lambda_k2': lambda_k2[l], 'subln': subln[l],
            'w_attn_out': w_attn_out[l], 'ssm_lambda_re': ssm_lambda_re[l],
            'ssm_lambda_im': ssm_lambda_im[l], 'ssm_log_step': ssm_log_step[l],
            'ssm_b_re': ssm_b_re[l], 'ssm_b_im': ssm_b_im[l], 'ssm_c_re': ssm_c_re[l],
            'ssm_c_im': ssm_c_im[l], 'ssm_d': ssm_d[l], 'w_glu': w_glu[l],
            'w_ssm_out': w_ssm_out[l], 'w_out': w_out[l], 'peer_w_query': peer_w_query[l],
            'peer_sub_keys': peer_sub_keys[l], 'peer_u': peer_u[l], 'peer_v': peer_v[l],
        }
        lam_init = 0.8 - 0.6 * math.exp(-0.3 * l)
        h0 = jnp.zeros((x_prompt.shape[0], N_GROUPS, STATE_DIM), F32)
        yp, kp, vp, hrp, hip = trunk_layer(yp, c_prompt, pos_prompt, prompt_attend, h0, h0, lam_init, w)
        attend_s = functools.partial(sample_attend, q_pos=pos_sample, cache_k=cache_k, cache_v=cache_v,
                                     page_table=page_table, layer=l)
        ys, ks, vs, hrs, his = trunk_layer(ys, c_sample, pos_sample, attend_s,
                                           state_ssm_re[l], state_ssm_im[l], lam_init, w)
        kp_l.append(kp); vp_l.append(vp); hrp_l.append(hrp); hip_l.append(hip)
        ks_l.append(ks); vs_l.append(vs); hrs_l.append(hrs); his_l.append(his)
    return (yp, ys, jnp.stack(kp_l), jnp.stack(vp_l), jnp.stack(hrp_l), jnp.stack(hip_l),
            jnp.stack(ks_l), jnp.stack(vs_l), jnp.stack(hrs_l), jnp.stack(his_l))
```

```python
import functools
import math

import jax
import jax.numpy as jnp
from jax import lax
from jax.experimental import pallas as pl
from jax.experimental.pallas import tpu as pltpu

F32 = jnp.float32
BF16 = jnp.bfloat16

D_MODEL = 1024
N_HEADS = 8
HEAD_DIM = 64
V_DIM = 2 * HEAD_DIM
QK_WIDTH = N_HEADS * 2 * HEAD_DIM
ATTN_WIDTH = N_HEADS * V_DIM
ROPE_THETA = 10000.0
PAGE_SIZE = 128
SSM_WIDTH = 512
SSM_GROUP = 16
N_GROUPS = SSM_WIDTH // SSM_GROUP
STATE_DIM = 64
N_STATE = N_GROUPS * STATE_DIM
SSM_CHUNKS = 4
SSM_CHUNK_IN = SSM_WIDTH // SSM_CHUNKS
SSM_CHUNK_ST = N_STATE // SSM_CHUNKS
N_KEYS = 128
N_EXPERTS = N_KEYS * N_KEYS
PEER_HEADS = 8
PEER_KEY_DIM = 128
PEER_TOPK = 16
N_MOD = 6
RMS_EPS = 1e-6

VMEM_LIMIT_BYTES = 56 * 1024 * 1024

NEG_INF = float("-inf")
NEW_ROWS = 16


def _cparams(*sem):
    return pltpu.CompilerParams(dimension_semantics=sem, vmem_limit_bytes=VMEM_LIMIT_BYTES)


def _dot(a, b):
    return jnp.dot(a, b, preferred_element_type=F32)


def _dot_nt(a, b):
    return lax.dot_general(a, b, (((1,), (1,)), ((), ())), preferred_element_type=F32)


def _rms(x, w):
    return x * lax.rsqrt(jnp.mean(x * x, axis=-1, keepdims=True) + RMS_EPS) * w


def _gelu(x):
    return 0.5 * x * (1.0 + jnp.tanh(0.7978845608028654 * (x + 0.044715 * (x * x * x))))


def _sigmoid(x):
    return 1.0 / (1.0 + jnp.exp(-x))


def _mod_spec(rows_per_mod, tm):
    if rows_per_mod == 1:
        return pl.BlockSpec((1, D_MODEL), lambda i: (0, 0))
    return pl.BlockSpec((tm, D_MODEL), lambda i: (i, 0))


def _ada_kernel(c_ref, w_ref, b_ref, o_ref):
    c = c_ref[...]
    s = c * _sigmoid(c)
    o_ref[...] = _dot(s.astype(BF16), w_ref[...].astype(BF16)) + b_ref[...]


def _ada(c_all, w_ada, b_ada):
    rows = c_all.shape[0]
    tn = 1536
    return pl.pallas_call(
        _ada_kernel,
        grid=(N_MOD * D_MODEL // tn,),
        in_specs=[pl.BlockSpec((rows, D_MODEL), lambda j: (0, 0)),
                  pl.BlockSpec((D_MODEL, tn), lambda j: (0, j)),
                  pl.BlockSpec((1, tn), lambda j: (0, j))],
        out_specs=pl.BlockSpec((rows, tn), lambda j: (0, j)),
        out_shape=jax.ShapeDtypeStruct((rows, N_MOD * D_MODEL), F32),
        compiler_params=_cparams("arbitrary"),
        name="ada_mod",
    )(c_all, w_ada, b_ada.reshape(1, -1))


def _inproj_kernel(x_ref, nw_ref, sc_ref, sh_ref, cos_ref, sin_ref, w_ref,
                   q2_ref, k32_ref, kb_ref, v32_ref, vb_ref, u_ref, ga_ref, gs_ref):
    x = x_ref[...]
    h = _rms(x, nw_ref[...]) * (1.0 + sc_ref[...]) + sh_ref[...]
    hb = h.astype(BF16)
    cos = cos_ref[...]
    sin = sin_ref[...]
    lane = lax.broadcasted_iota(jnp.int32, cos.shape, 1)
    first = lane < HEAD_DIM
    W = QK_WIDTH
    for hd in range(N_HEADS):
        c0 = hd * V_DIM
        q = _dot(hb, w_ref[:, c0:c0 + V_DIM])
        qs = _dot(hb, w_ref[:, W + c0:W + c0 + V_DIM])
        qr = (q * cos + qs * sin) * (HEAD_DIM ** -0.5)
        q2_ref[0, :, c0:c0 + V_DIM] = jnp.where(first, qr, 0.0).astype(BF16)
        q2_ref[1, :, c0:c0 + V_DIM] = jnp.where(first, 0.0, qr).astype(BF16)
        k = _dot(hb, w_ref[:, 2 * W + c0:2 * W + c0 + V_DIM])
        ks = _dot(hb, w_ref[:, 3 * W + c0:3 * W + c0 + V_DIM])
        kr = k * cos + ks * sin
        k32_ref[:, c0:c0 + V_DIM] = kr
        kb_ref[:, c0:c0 + V_DIM] = kr.astype(BF16)
    o = 4 * W
    v = _dot(hb, w_ref[:, o:o + ATTN_WIDTH])
    v32_ref[...] = v
    vb_ref[...] = v.astype(BF16)
    o += ATTN_WIDTH
    u_ref[...] = _dot(hb, w_ref[:, o:o + SSM_WIDTH])
    o += SSM_WIDTH
    ga_ref[...] = _dot(hb, w_ref[:, o:o + D_MODEL])
    o += D_MODEL
    gs_ref[...] = _dot(hb, w_ref[:, o:o + D_MODEL])


def _inproj(x, nw, sc, sh, cosf, sinf, w_ext, tm):
    n = x.shape[0]
    wcols = w_ext.shape[1]
    row = lambda i: (i, 0)
    full = lambda i: (0, 0)
    outs = (
        jax.ShapeDtypeStruct((2, n, QK_WIDTH), BF16),
        jax.ShapeDtypeStruct((n, QK_WIDTH), F32),
        jax.ShapeDtypeStruct((n, QK_WIDTH), BF16),
        jax.ShapeDtypeStruct((n, ATTN_WIDTH), F32),
        jax.ShapeDtypeStruct((n, ATTN_WIDTH), BF16),
        jax.ShapeDtypeStruct((n, SSM_WIDTH), F32),
        jax.ShapeDtypeStruct((n, D_MODEL), F32),
        jax.ShapeDtypeStruct((n, D_MODEL), F32),
    )
    out_specs = (
        pl.BlockSpec((2, tm, QK_WIDTH), lambda i: (0, i, 0)),
        pl.BlockSpec((tm, QK_WIDTH), row),
        pl.BlockSpec((tm, QK_WIDTH), row),
        pl.BlockSpec((tm, ATTN_WIDTH), row),
        pl.BlockSpec((tm, ATTN_WIDTH), row),
        pl.BlockSpec((tm, SSM_WIDTH), row),
        pl.BlockSpec((tm, D_MODEL), row),
        pl.BlockSpec((tm, D_MODEL), row),
    )
    return pl.pallas_call(
        _inproj_kernel,
        grid=(n // tm,),
        in_specs=[pl.BlockSpec((tm, D_MODEL), row),
                  pl.BlockSpec((1, D_MODEL), full),
                  _mod_spec(sc.shape[0], tm),
                  _mod_spec(sh.shape[0], tm),
                  pl.BlockSpec((tm, V_DIM), row),
                  pl.BlockSpec((tm, V_DIM), row),
                  pl.BlockSpec((D_MODEL, wcols), full)],
        out_specs=out_specs,
        out_shape=outs,
        compiler_params=_cparams("arbitrary"),
        name="in_proj",
    )(x, nw, sc, sh, cosf, sinf, w_ext)


def _lambda_full(lq1_ref, lk1_ref, lq2_ref, lk2_ref, lam_init):
    s1 = jnp.sum(lq1_ref[...] * lk1_ref[...], axis=-1, keepdims=True)
    s2 = jnp.sum(lq2_ref[...] * lk2_ref[...], axis=-1, keepdims=True)
    return jnp.exp(s1) - jnp.exp(s2) + lam_init


def _subln(o, w, lam_init):
    return _rms(o, w) * (1.0 - lam_init)


def _pattn_kernel(q_ref, k_ref, v_ref, lq1_ref, lk1_ref, lq2_ref, lk2_ref, sub_ref, o_ref,
                  m_ref, l_ref, acc_ref, *, tq, lam_init):
    i = pl.program_id(1)
    q = jnp.concatenate([q_ref[0], q_ref[1]], axis=0)
    m_ref[...] = jnp.full(m_ref.shape, NEG_INF, F32)
    l_ref[...] = jnp.zeros(l_ref.shape, F32)
    acc_ref[...] = jnp.zeros(acc_ref.shape, F32)

    def step(j, masked):
        start = pl.multiple_of(j * tq, tq)
        k = k_ref[pl.ds(start, tq), :]
        v = v_ref[pl.ds(start, tq), :]
        s = _dot_nt(q, k)
        if masked:
            r = lax.broadcasted_iota(jnp.int32, (tq, tq), 0)
            c = lax.broadcasted_iota(jnp.int32, (tq, tq), 1)
            keep = jnp.concatenate([c <= r, c <= r], axis=0)
            s = jnp.where(keep, s, NEG_INF)
        m_old = m_ref[...]
        m_new = jnp.maximum(m_old, jnp.max(s, axis=-1, keepdims=True))
        p = jnp.exp(s - m_new)
        alpha = jnp.exp(m_old - m_new)
        l_ref[...] = alpha * l_ref[...] + jnp.sum(p, axis=-1, keepdims=True)
        acc_ref[...] = alpha * acc_ref[...] + _dot(p.astype(BF16), v)
        m_ref[...] = m_new

    def body(j, carry):
        step(j, False)
        return carry

    lax.fori_loop(0, i, body, 0)
    step(i, True)

    lam = _lambda_full(lq1_ref, lk1_ref, lq2_ref, lk2_ref, lam_init)
    o = acc_ref[...] / l_ref[...]
    d = o[:tq] - lam * o[tq:]
    o_ref[...] = _subln(d, sub_ref[...], lam_init).astype(BF16)


def _prompt_attention(q2, kb, vb, lq1, lk1, lq2, lk2, subln, lam_init, tq):
    t = kb.shape[0]
    vec = lambda h, i: (0, 0)
    kern = functools.partial(_pattn_kernel, tq=tq, lam_init=lam_init)
    return pl.pallas_call(
        kern,
        grid=(N_HEADS, t // tq),
        in_specs=[pl.BlockSpec((2, tq, V_DIM), lambda h, i: (0, i, h)),
                  pl.BlockSpec((t, V_DIM), lambda h, i: (0, h)),
                  pl.BlockSpec((t, V_DIM), lambda h, i: (0, h)),
                  pl.BlockSpec((1, HEAD_DIM), vec),
                  pl.BlockSpec((1, HEAD_DIM), vec),
                  pl.BlockSpec((1, HEAD_DIM), vec),
                  pl.BlockSpec((1, HEAD_DIM), vec),
                  pl.BlockSpec((1, V_DIM), vec)],
        out_specs=pl.BlockSpec((tq, V_DIM), lambda h, i: (i, h)),
        out_shape=jax.ShapeDtypeStruct((t, ATTN_WIDTH), BF16),
        scratch_shapes=[pltpu.VMEM((2 * tq, 1), F32),
                        pltpu.VMEM((2 * tq, 1), F32),
                        pltpu.VMEM((2 * tq, V_DIM), F32)],
        compiler_params=_cparams("arbitrary", "arbitrary"),
        name="prompt_attn",
    )(q2, kb, vb, lq1, lk1, lq2, lk2, subln)


def _sattn_kernel(pt_ref, q_ref, kn_ref, vn_ref, lq1_ref, lk1_ref, lq2_ref, lk2_ref, sub_ref, *rest,
                  pages_per_step, n_new, lam_init):
    g_pages = pages_per_step
    k_refs = rest[:g_pages]
    v_refs = rest[g_pages:2 * g_pages]
    o_ref = rest[2 * g_pages]
    m_ref, l_ref, acc_ref = rest[2 * g_pages + 1:]
    g = pl.program_id(1)
    q = q_ref[0]

    @pl.when(g == 0)
    def _():
        m_ref[...] = jnp.full(m_ref.shape, NEG_INF, F32)
        l_ref[...] = jnp.zeros(l_ref.shape, F32)
        acc_ref[...] = jnp.zeros(acc_ref.shape, F32)

    def update(s, v):
        m_old = m_ref[...]
        m_new = jnp.maximum(m_old, jnp.max(s, axis=-1, keepdims=True))
        p = jnp.exp(s - m_new)
        alpha = jnp.exp(m_old - m_new)
        l_ref[...] = alpha * l_ref[...] + jnp.sum(p, axis=-1, keepdims=True)
        acc_ref[...] = alpha * acc_ref[...] + _dot(p.astype(BF16), v)
        m_ref[...] = m_new

    k = jnp.concatenate([r[0].astype(BF16) for r in k_refs], axis=0)
    v = jnp.concatenate([r[0].astype(BF16) for r in v_refs], axis=0)
    update(_dot_nt(q, k), v)

    @pl.when(g == pl.num_programs(1) - 1)
    def _():
        kn = kn_ref[0].astype(BF16)
        vn = vn_ref[0].astype(BF16)
        s = _dot_nt(q, kn)
        qi = lax.broadcasted_iota(jnp.int32, s.shape, 0) % n_new
        ki = lax.broadcasted_iota(jnp.int32, s.shape, 1)
        s = jnp.where(ki <= qi, s, NEG_INF)
        update(s, vn)
        lam = _lambda_full(lq1_ref, lk1_ref, lq2_ref, lk2_ref, lam_init)
        o = acc_ref[...] / l_ref[...]
        rows = 2 * n_new
        for hd in range(N_HEADS):
            blk = o[hd * rows:(hd + 1) * rows, hd * V_DIM:(hd + 1) * V_DIM]
            d = blk - lam * pltpu.roll(blk, n_new, 0)
            o_ref[0, :, hd * V_DIM:(hd + 1) * V_DIM] = _subln(d, sub_ref[...], lam_init)


def _sample_attention(page_table, qbd, k_new, v_new, cache_k, cache_v, lq1, lk1, lq2, lk2, subln,
                      lam_init, n_new, pages_per_step):
    nb, n_pages = page_table.shape
    gp = pages_per_step
    rows = qbd.shape[1]
    vec = lambda b, g, pt: (0, 0)

    def page_spec(r):
        return pl.BlockSpec((1, PAGE_SIZE, QK_WIDTH),
                            lambda b, g, pt: (pt[b * n_pages + g * gp + r], 0, 0))

    kern = functools.partial(_sattn_kernel, pages_per_step=gp, n_new=n_new, lam_init=lam_init)
    grid_spec = pltpu.PrefetchScalarGridSpec(
        num_scalar_prefetch=1,
        grid=(nb, n_pages // gp),
        in_specs=[pl.BlockSpec((1, rows, QK_WIDTH), lambda b, g, pt: (b, 0, 0)),
                  pl.BlockSpec((1, NEW_ROWS, QK_WIDTH), lambda b, g, pt: (b, 0, 0)),
                  pl.BlockSpec((1, NEW_ROWS, ATTN_WIDTH), lambda b, g, pt: (b, 0, 0)),
                  pl.BlockSpec((1, HEAD_DIM), vec),
                  pl.BlockSpec((1, HEAD_DIM), vec),
                  pl.BlockSpec((1, HEAD_DIM), vec),
                  pl.BlockSpec((1, HEAD_DIM), vec),
                  pl.BlockSpec((1, V_DIM), vec)]
                 + [page_spec(r) for r in range(gp)] + [page_spec(r) for r in range(gp)],
        out_specs=pl.BlockSpec((1, 8, ATTN_WIDTH), lambda b, g, pt: (b, 0, 0)),
        scratch_shapes=[pltpu.VMEM((rows, 1), F32),
                        pltpu.VMEM((rows, 1), F32),
                        pltpu.VMEM((rows, ATTN_WIDTH), F32)],
    )
    return pl.pallas_call(
        kern,
        grid_spec=grid_spec,
        out_shape=jax.ShapeDtypeStruct((nb, 8, ATTN_WIDTH), F32),
        compiler_params=_cparams("arbitrary", "arbitrary"),
        name="sample_attn",
    )(page_table.reshape(-1), qbd, k_new, v_new, lq1, lk1, lq2, lk2, subln,
      *([cache_k] * gp), *([cache_v] * gp))


def _ssm_disc_kernel(lr_ref, li_ref, ls_ref, are_ref, aim_ref, fre_ref, fim_ref, *, n_pow):
    lr = lr_ref[...]
    li = li_ref[...]
    dt = jnp.exp(ls_ref[...])
    mag = jnp.exp(lr * dt)
    a_re = mag * jnp.cos(li * dt)
    a_im = mag * jnp.sin(li * dt)
    den = lr * lr + li * li
    nr = a_re - 1.0
    ni = a_im
    fre_ref[...] = (nr * lr + ni * li) / den
    fim_ref[...] = (ni * lr - nr * li) / den
    pr, pi = a_re, a_im
    for kk in range(n_pow):
        are_ref[kk] = pr
        aim_ref[kk] = pi
        pr, pi = pr * pr - pi * pi, 2.0 * pr * pi


def _ssm_bb_kernel(fre_ref, fim_ref, bre_ref, bim_ref, ore_ref, oim_ref):
    fr = fre_ref[...]
    fi = fim_ref[...]
    br = bre_ref[...]
    bi = bim_ref[...]
    ore_ref[...] = fr * br - fi * bi
    oim_ref[...] = fr * bi + fi * br


def _ssm_prepare(lam_re, lam_im, log_step, b_re, b_im, c_re, c_im, n_pow):
    gshape = jax.ShapeDtypeStruct((N_GROUPS, STATE_DIM), F32)
    pshape = jax.ShapeDtypeStruct((n_pow, N_GROUPS, STATE_DIM), F32)
    a_re, a_im, f_re, f_im = pl.pallas_call(
        functools.partial(_ssm_disc_kernel, n_pow=n_pow),
        out_shape=(pshape, pshape, gshape, gshape),
        name="ssm_discretise",
    )(lam_re, lam_im, log_step.reshape(N_GROUPS, 1))
    bshape = jax.ShapeDtypeStruct((N_STATE, SSM_GROUP), F32)
    bb_re, bb_im = pl.pallas_call(
        _ssm_bb_kernel, out_shape=(bshape, bshape), name="ssm_input_matrix",
    )(f_re.reshape(N_STATE, 1), f_im.reshape(N_STATE, 1),
      b_re.reshape(N_STATE, SSM_GROUP), b_im.reshape(N_STATE, SSM_GROUP))
    gc = N_GROUPS // SSM_CHUNKS
    eye = jnp.eye(gc, dtype=F32)

    def in_mat(bb):
        b4 = bb.reshape(SSM_CHUNKS, gc, STATE_DIM, SSM_GROUP)
        m = jnp.einsum('cgpi,gh->cgihp', b4, eye)
        return m.reshape(SSM_CHUNKS, gc * SSM_GROUP, gc * STATE_DIM)

    def out_mat(cc):
        c4 = cc.reshape(SSM_CHUNKS, gc, SSM_GROUP, STATE_DIM)
        m = jnp.einsum('cgip,gh->cgphi', c4, eye)
        return m.reshape(SSM_CHUNKS, gc * STATE_DIM, gc * SSM_GROUP)

    b_mat = jnp.concatenate([in_mat(bb_re), in_mat(bb_im)], axis=-1).astype(BF16)
    c_mat = jnp.concatenate([out_mat(c_re), out_mat(-c_im)], axis=-2).astype(BF16)
    return (a_re.reshape(n_pow, N_STATE), a_im.reshape(n_pow, N_STATE), b_mat, c_mat)


def _shift_rows(x, s, row):
    return jnp.where(row >= s, pltpu.roll(x, s, 0), 0.0)


def _ssm_scan_kernel(u_ref, are_ref, aim_ref, b_ref, c_ref, d_ref, wg_ref,
                     y_ref, hre_ref, him_ref, cre_ref, cim_ref, *, tm, n_pow):
    @pl.when(pl.program_id(0) == 0)
    def _():
        cre_ref[...] = jnp.zeros(cre_ref.shape, F32)
        cim_ref[...] = jnp.zeros(cim_ref.shape, F32)

    u = u_ref[...]
    ub = u.astype(BF16)
    row = lax.broadcasted_iota(jnp.int32, (tm, SSM_CHUNK_ST), 0)
    ys = []
    for c in range(SSM_CHUNKS):
        lanes = slice(c * SSM_CHUNK_ST, (c + 1) * SSM_CHUNK_ST)
        bu = _dot(ub[:, c * SSM_CHUNK_IN:(c + 1) * SSM_CHUNK_IN], b_ref[c])
        xr = bu[:, :SSM_CHUNK_ST]
        xi = bu[:, SSM_CHUNK_ST:]
        a_r = are_ref[0:1, lanes]
        a_i = aim_ref[0:1, lanes]
        h_r = cre_ref[0:1, lanes]
        h_i = cim_ref[0:1, lanes]
        first = row == 0
        xr = xr + jnp.where(first, a_r * h_r - a_i * h_i, 0.0)
        xi = xi + jnp.where(first, a_r * h_i + a_i * h_r, 0.0)
        for kk in range(n_pow):
            s = 1 << kk
            p_r = are_ref[kk:kk + 1, lanes]
            p_i = aim_ref[kk:kk + 1, lanes]
            sr = _shift_rows(xr, s, row)
            si = _shift_rows(xi, s, row)
            xr, xi = xr + (p_r * sr - p_i * si), xi + (p_r * si + p_i * sr)
        cre_ref[0:1, lanes] = xr[tm - 1:tm]
        cim_ref[0:1, lanes] = xi[tm - 1:tm]
        hcat = jnp.concatenate([xr, xi], axis=1).astype(BF16)
        ys.append(_dot(hcat, c_ref[c]))
    y = jnp.concatenate(ys, axis=1) + d_ref[...] * u
    y = _gelu(y)
    y = y * _sigmoid(_dot(y.astype(BF16), wg_ref[...]))
    y_ref[...] = y.astype(BF16)
    hre_ref[...] = cre_ref[0:1, :]
    him_ref[...] = cim_ref[0:1, :]


def _ssm_prompt(u, a_re, a_im, b_mat, c_mat, d, wg, tm):
    t = u.shape[0]
    n_pow = a_re.shape[0]
    full2 = lambda i: (0, 0)
    full3 = lambda i: (0, 0, 0)
    kern = functools.partial(_ssm_scan_kernel, tm=tm, n_pow=n_pow)
    return pl.pallas_call(
        kern,
        grid=(t // tm,),
        in_specs=[pl.BlockSpec((tm, SSM_WIDTH), lambda i: (i, 0)),
                  pl.BlockSpec((n_pow, N_STATE), full2),
                  pl.BlockSpec((n_pow, N_STATE), full2),
                  pl.BlockSpec(b_mat.shape, full3),
                  pl.BlockSpec(c_mat.shape, full3),
                  pl.BlockSpec((1, SSM_WIDTH), full2),
                  pl.BlockSpec((SSM_WIDTH, SSM_WIDTH), full2)],
        out_specs=(pl.BlockSpec((tm, SSM_WIDTH), lambda i: (i, 0)),
                   pl.BlockSpec((1, N_STATE), full2),
                   pl.BlockSpec((1, N_STATE), full2)),
        out_shape=(jax.ShapeDtypeStruct((t, SSM_WIDTH), BF16),
                   jax.ShapeDtypeStruct((1, N_STATE), F32),
                   jax.ShapeDtypeStruct((1, N_STATE), F32)),
        scratch_shapes=[pltpu.VMEM((8, N_STATE), F32), pltpu.VMEM((8, N_STATE), F32)],
        compiler_params=_cparams("arbitrary"),
        name="ssm_prompt",
    )(u, a_re, a_im, b_mat, c_mat, d, wg)


def _ssm_step_kernel(u_ref, h0re_ref, h0im_ref, are_ref, aim_ref, b_ref, c_ref, d_ref, wg_ref,
                     y_ref, hre_ref, him_ref, *, n_steps):
    a_r = are_ref[0:1, :]
    a_i = aim_ref[0:1, :]
    h_r = h0re_ref[...]
    h_i = h0im_ref[...]
    for t in range(n_steps):
        u = u_ref[t]
        ub = u.astype(BF16)
        brs, bis = [], []
        for c in range(SSM_CHUNKS):
            bu = _dot(ub[:, c * SSM_CHUNK_IN:(c + 1) * SSM_CHUNK_IN], b_ref[c])
            brs.append(bu[:, :SSM_CHUNK_ST])
            bis.append(bu[:, SSM_CHUNK_ST:])
        bu_r = jnp.concatenate(brs, axis=1)
        bu_i = jnp.concatenate(bis, axis=1)
        h_r, h_i = a_r * h_r - a_i * h_i + bu_r, a_r * h_i + a_i * h_r + bu_i
        ys = []
        for c in range(SSM_CHUNKS):
            lanes = slice(c * SSM_CHUNK_ST, (c + 1) * SSM_CHUNK_ST)
            hcat = jnp.concatenate([h_r[:, lanes], h_i[:, lanes]], axis=1).astype(BF16)
            ys.append(_dot(hcat, c_ref[c]))
        y = jnp.concatenate(ys, axis=1) + d_ref[...] * u
        y = _gelu(y)
        y = y * _sigmoid(_dot(y.astype(BF16), wg_ref[...]))
        y_ref[t] = y.astype(BF16)
    hre_ref[...] = h_r
    him_ref[...] = h_i


def _ssm_sample(u_tb, h0_re, h0_im, a_re, a_im, b_mat, c_mat, d, wg):
    n_steps, nb, _ = u_tb.shape
    kern = functools.partial(_ssm_step_kernel, n_steps=n_steps)
    return pl.pallas_call(
        kern,
        out_shape=(jax.ShapeDtypeStruct((n_steps, nb, SSM_WIDTH), BF16),
                   jax.ShapeDtypeStruct((nb, N_STATE), F32),
                   jax.ShapeDtypeStruct((nb, N_STATE), F32)),
        compiler_params=pltpu.CompilerParams(vmem_limit_bytes=VMEM_LIMIT_BYTES),
        name="ssm_sample",
    )(u_tb, h0_re, h0_im, a_re, a_im, b_mat, c_mat, d, wg)


def _kth_largest_rows(c, k):
    cnt = jnp.zeros((1, c.shape[1]), F32)
    tau = jnp.full((1, c.shape[1]), NEG_INF, F32)
    for _ in range(k):
        m = jnp.max(c, axis=0, keepdims=True)
        eq = c == m
        tau = jnp.where(cnt < k, m, tau)
        cnt = cnt + jnp.sum(eq.astype(F32), axis=0, keepdims=True)
        c = jnp.where(eq, NEG_INF, c)
    return tau


def _top_rows(a, k, out_ref):
    for r in range(k):
        m = jnp.max(a, axis=0, keepdims=True)
        out_ref[r:r + 1, :] = m
        a = jnp.where(a == m, NEG_INF, a)


def _mix_kernel(x_ref, o_ref, ys_ref, ga_ref, gs_ref, g1_ref, sc2_ref, sh2_ref, npost_ref, npre_ref,
                wao_ref, wso_ref, wout_ref, wq_ref, skt_ref,
                x1_ref, h2_ref, s_ref, e_ref, tau_ref, v1_ref, v2_ref):
    y_attn = _dot(o_ref[...], wao_ref[...])
    y_ssm = _dot(ys_ref[...], wso_ref[...])
    merged = _sigmoid(ga_ref[...]) * y_attn + _sigmoid(gs_ref[...]) * y_ssm
    mix = _dot(merged.astype(BF16), wout_ref[...])
    x1 = x_ref[...] + g1_ref[...] * _rms(mix, npost_ref[...])
    x1_ref[...] = x1
    h2 = _rms(x1, npre_ref[...]) * (1.0 + sc2_ref[...]) + sh2_ref[...]
    h2b = h2.astype(BF16)
    h2_ref[...] = h2b
    qp = _dot(h2b, wq_ref[...]).astype(BF16)
    s_all = _dot_nt(skt_ref[...], qp)
    s_ref[...] = s_all
    for hd in range(PEER_HEADS):
        r0 = hd * 2 * N_KEYS
        s1 = s_all[r0:r0 + N_KEYS]
        s2 = s_all[r0 + N_KEYS:r0 + 2 * N_KEYS]
        _top_rows(s1, PEER_TOPK, v1_ref)
        _top_rows(s2, PEER_TOPK, v2_ref)
        v1a = v1_ref[...]
        cand = jnp.concatenate([v1a + v2_ref[b:b + 1, :] for b in range(PEER_TOPK)], axis=0)
        tau = _kth_largest_rows(cand, PEER_TOPK)
        m1 = v1_ref[0:1, :]
        m2 = v2_ref[0:1, :]
        z = jnp.sum(jnp.where(cand >= tau, jnp.exp(cand - (m1 + m2)), 0.0), axis=0, keepdims=True)
        e_ref[r0:r0 + N_KEYS, :] = jnp.exp(s1 - m1)
        e_ref[r0 + N_KEYS:r0 + 2 * N_KEYS, :] = jnp.exp(s2 - m2) / z
        tau_ref[hd:hd + 1, :] = tau


def _mix(x, ob, ysb, ga, gs, g1, sc2, sh2, npost, npre, wao, wso, wout, wq, skt, tm):
    n = x.shape[0]
    row = lambda i: (i, 0)
    col = lambda i: (0, i)
    full = lambda i: (0, 0)
    nrow = 2 * N_KEYS * PEER_HEADS
    return pl.pallas_call(
        _mix_kernel,
        grid=(n // tm,),
        in_specs=[pl.BlockSpec((tm, D_MODEL), row),
                  pl.BlockSpec((tm, ATTN_WIDTH), row),
                  pl.BlockSpec((tm, SSM_WIDTH), row),
                  pl.BlockSpec((tm, D_MODEL), row),
                  pl.BlockSpec((tm, D_MODEL), row),
                  _mod_spec(g1.shape[0], tm),
                  _mod_spec(sc2.shape[0], tm),
                  _mod_spec(sh2.shape[0], tm),
                  pl.BlockSpec((1, D_MODEL), full),
                  pl.BlockSpec((1, D_MODEL), full),
                  pl.BlockSpec((ATTN_WIDTH, D_MODEL), full),
                  pl.BlockSpec((SSM_WIDTH, D_MODEL), full),
                  pl.BlockSpec((D_MODEL, D_MODEL), full),
                  pl.BlockSpec((D_MODEL, PEER_HEADS * PEER_KEY_DIM), full),
                  pl.BlockSpec((nrow, PEER_HEADS * PEER_KEY_DIM), full)],
        out_specs=(pl.BlockSpec((tm, D_MODEL), row),
                   pl.BlockSpec((tm, D_MODEL), row),
                   pl.BlockSpec((nrow, tm), col),
                   pl.BlockSpec((nrow, tm), col),
                   pl.BlockSpec((PEER_HEADS, tm), col)),
        out_shape=(jax.ShapeDtypeStruct((n, D_MODEL), F32),
                   jax.ShapeDtypeStruct((n, D_MODEL), BF16),
                   jax.ShapeDtypeStruct((nrow, n), F32),
                   jax.ShapeDtypeStruct((nrow, n), F32),
                   jax.ShapeDtypeStruct((PEER_HEADS, n), F32)),
        scratch_shapes=[pltpu.VMEM((PEER_TOPK, tm), F32), pltpu.VMEM((PEER_TOPK, tm), F32)],
        compiler_params=_cparams("arbitrary"),
        name="mix_route",
    )(x, ob, ysb, ga, gs, g1, sc2, sh2, npost, npre, wao, wso, wout, wq, skt)


def _peer_kernel(h2_ref, u_ref, vt_ref, s_ref, e_ref, tau_ref, x1_ref, g2_ref, nw_ref,
                 y_ref, acc_ref, *, ec):
    c = pl.program_id(1)

    @pl.when(c == 0)
    def _():
        acc_ref[...] = jnp.zeros(acc_ref.shape, F32)

    act = _dot_nt(u_ref[...], h2_ref[...])
    ws = []
    for ii in range(ec // N_KEYS):
        key1 = c * (ec // N_KEYS) + ii
        gate = jnp.zeros((N_KEYS, act.shape[1]), F32)
        for hd in range(PEER_HEADS):
            r0 = hd * 2 * N_KEYS
            a = s_ref[pl.ds(r0 + key1, 1), :]
            e1 = e_ref[pl.ds(r0 + key1, 1), :]
            b = s_ref[r0 + N_KEYS:r0 + 2 * N_KEYS, :]
            e2 = e_ref[r0 + N_KEYS:r0 + 2 * N_KEYS, :]
            gate = gate + jnp.where(a + b >= tau_ref[hd:hd + 1, :], e1 * e2, 0.0)
        ws.append((gate * _gelu(act[ii * N_KEYS:(ii + 1) * N_KEYS])).astype(BF16))
    w = jnp.concatenate(ws, axis=0) if len(ws) > 1 else ws[0]
    acc_ref[...] += _dot(vt_ref[...], w)

    @pl.when(c == pl.num_programs(1) - 1)
    def _():
        f = acc_ref[...].T
        y_ref[...] = x1_ref[...] + g2_ref[...] * _rms(f, nw_ref[...])


def _peer(h2b, u_bf, vt_bf, s_t, e_t, tau, x1, g2, nw, tn, ec):
    n = h2b.shape[0]
    nrow = s_t.shape[0]
    kern = functools.partial(_peer_kernel, ec=ec)
    g2_spec = (pl.BlockSpec((1, D_MODEL), lambda t, c: (0, 0)) if g2.shape[0] == 1
               else pl.BlockSpec((tn, D_MODEL), lambda t, c: (t, 0)))
    return pl.pallas_call(
        kern,
        grid=(n // tn, N_EXPERTS // ec),
        in_specs=[pl.BlockSpec((tn, D_MODEL), lambda t, c: (t, 0)),
                  pl.BlockSpec((ec, D_MODEL), lambda t, c: (c, 0)),
                  pl.BlockSpec((D_MODEL, ec), lambda t, c: (0, c)),
                  pl.BlockSpec((nrow, tn), lambda t, c: (0, t)),
                  pl.BlockSpec((nrow, tn), lambda t, c: (0, t)),
                  pl.BlockSpec((PEER_HEADS, tn), lambda t, c: (0, t)),
                  pl.BlockSpec((tn, D_MODEL), lambda t, c: (t, 0)),
                  g2_spec,
                  pl.BlockSpec((1, D_MODEL), lambda t, c: (0, 0))],
        out_specs=pl.BlockSpec((tn, D_MODEL), lambda t, c: (t, 0)),
        out_shape=jax.ShapeDtypeStruct((n, D_MODEL), F32),
        scratch_shapes=[pltpu.VMEM((D_MODEL, tn), F32)],
        compiler_params=_cparams("arbitrary", "arbitrary"),
        name="peer_experts",
    )(h2b, u_bf, vt_bf, s_t, e_t, tau, x1, g2, nw)


def _rope_tables(pos):
    half = HEAD_DIM // 2
    inv = ROPE_THETA ** (-jnp.arange(half, dtype=F32) / half)
    ang = pos.astype(F32)[:, None] * inv[None, :]
    cos = jnp.cos(ang)
    sin = jnp.sin(ang)
    cosf = jnp.concatenate([cos, cos, cos, cos], axis=1)
    sinf = jnp.concatenate([-sin, sin, -sin, sin], axis=1)
    return cosf, sinf


def _swap_halves(w):
    half = HEAD_DIM // 2
    k = w.shape[0]
    return w.reshape(k, -1, 2, half)[:, :, ::-1, :].reshape(k, -1)


def _tile(n, pref):
    t = min(n, pref)
    assert n % t == 0, (n, pref)
    return t


def kernel(x_prompt, x_sample, cache_k, cache_v, state_ssm_re, state_ssm_im, page_table, c_prompt, c_sample, w_ada, b_ada, norm_pre_mix, norm_post_mix, norm_pre_ffn, norm_post_ffn, w_in, lambda_q1, lambda_k1, lambda_q2, lambda_k2, subln, w_attn_out, ssm_lambda_re, ssm_lambda_im, ssm_log_step, ssm_b_re, ssm_b_im, ssm_c_re, ssm_c_im, ssm_d, w_glu, w_ssm_out, w_out, peer_w_query, peer_sub_keys, peer_u, peer_v):
    depth = w_ada.shape[0]
    bp, tp, _ = x_prompt.shape
    nb, tn_new, _ = x_sample.shape
    assert bp == 1 and 2 * tn_new == 8
    n_pages = page_table.shape[1]
    past_len = n_pages * PAGE_SIZE
    n_phys = cache_k.shape[1]
    ns = nb * tn_new

    pos_p = jnp.arange(tp, dtype=jnp.int32)
    pos_s = jnp.tile(past_len + jnp.arange(tn_new, dtype=jnp.int32), nb)
    cos_p, sin_p = _rope_tables(pos_p)
    cos_s, sin_s = _rope_tables(pos_s)

    tm_p = _tile(tp, 256)
    tq = _tile(tp, 512)
    tscan = _tile(tp, 128)
    n_pow = int(math.log2(tscan))
    assert 1 << n_pow == tscan
    pages_per_step = 8 if n_pages % 8 == 0 else 1

    yp = x_prompt.reshape(tp, D_MODEL)
    ys = x_sample.reshape(ns, D_MODEL)
    c_all = jnp.concatenate([c_prompt, c_sample], axis=0)
    pad = (-c_all.shape[0]) % 8
    c_all = jnp.pad(c_all, ((0, pad), (0, 0)))

    outs = [[] for _ in range(8)]
    for l in range(depth):
        lam_init = 0.8 - 0.6 * math.exp(-0.3 * l)
        mod = _ada(c_all, w_ada[l], b_ada[l])
        mod_p = [mod[0:1, j * D_MODEL:(j + 1) * D_MODEL] for j in range(N_MOD)]
        mod_s = [jnp.repeat(mod[1:1 + nb, j * D_MODEL:(j + 1) * D_MODEL], tn_new, axis=0)
                 for j in range(N_MOD)]

        w = w_in[l]
        wq_, wk_ = w[:, :QK_WIDTH], w[:, QK_WIDTH:2 * QK_WIDTH]
        w_ext = jnp.concatenate([wq_, _swap_halves(wq_), wk_, _swap_halves(wk_), w[:, 2 * QK_WIDTH:]],
                                axis=1).astype(BF16)
        a_re, a_im, b_mat, c_mat = _ssm_prepare(ssm_lambda_re[l], ssm_lambda_im[l], ssm_log_step[l],
                                                ssm_b_re[l], ssm_b_im[l], ssm_c_re[l], ssm_c_im[l],
                                                max(n_pow, 1))
        d_row = ssm_d[l].reshape(1, SSM_WIDTH)
        wg = w_glu[l].astype(BF16)
        wao = w_attn_out[l].astype(BF16)
        wso = w_ssm_out[l].astype(BF16)
        wout = w_out[l].astype(BF16)
        wq = peer_w_query[l].astype(BF16)
        sk = peer_sub_keys[l]
        eye = jnp.eye(PEER_HEADS * 2, dtype=F32)
        skt = jnp.einsum('bkd,bc->bkcd', sk.reshape(PEER_HEADS * 2, N_KEYS, PEER_KEY_DIM // 2), eye)
        skt = skt.reshape(PEER_HEADS * 2 * N_KEYS, PEER_HEADS * PEER_KEY_DIM).astype(BF16)
        u_bf = peer_u[l].astype(BF16)
        vt_bf = peer_v[l].T.astype(BF16)
        lq1 = lambda_q1[l].reshape(1, HEAD_DIM)
        lk1 = lambda_k1[l].reshape(1, HEAD_DIM)
        lq2 = lambda_q2[l].reshape(1, HEAD_DIM)
        lk2 = lambda_k2[l].reshape(1, HEAD_DIM)
        sub = subln[l].reshape(1, V_DIM)
        npre_mix = norm_pre_mix[l].reshape(1, D_MODEL)
        npost_mix = norm_post_mix[l].reshape(1, D_MODEL)
        npre_ffn = norm_pre_ffn[l].reshape(1, D_MODEL)
        npost_ffn = norm_post_ffn[l].reshape(1, D_MODEL)

        sh1, sc1, g1, sh2, sc2, g2 = mod_p
        q2, k32, kb, v32, vb, u, ga, gs = _inproj(yp, npre_mix, sc1, sh1, cos_p, sin_p, w_ext, tm_p)
        ob = _prompt_attention(q2, kb, vb, lq1, lk1, lq2, lk2, sub, lam_init, tq)
        ysb, hre, him = _ssm_prompt(u, a_re, a_im, b_mat, c_mat, d_row, wg, tscan)
        x1, h2b, s_t, e_t, tau = _mix(yp, ob, ysb, ga, gs, g1, sc2, sh2, npost_mix, npre_ffn,
                                      wao, wso, wout, wq, skt, tm_p)
        yp = _peer(h2b, u_bf, vt_bf, s_t, e_t, tau, x1, g2, npost_ffn, _tile(tp, 512), 512)
        outs[0].append(k32.reshape(1, tp, N_HEADS, 2 * HEAD_DIM))
        outs[1].append(v32.reshape(1, tp, N_HEADS, V_DIM))
        outs[2].append(hre.reshape(1, N_GROUPS, STATE_DIM))
        outs[3].append(him.reshape(1, N_GROUPS, STATE_DIM))

        sh1, sc1, g1, sh2, sc2, g2 = mod_s
        q2, k32, _, v32, _, u, ga, gs = _inproj(ys, npre_mix, sc1, sh1, cos_s, sin_s, w_ext, ns)
        q6 = q2.reshape(2, nb, tn_new, 1, N_HEADS, V_DIM)
        hmask = jnp.eye(N_HEADS, dtype=BF16).reshape(1, 1, 1, N_HEADS, N_HEADS, 1)
        qbd = (q6 * hmask).transpose(1, 3, 0, 2, 4, 5).reshape(nb, N_HEADS * 2 * tn_new, QK_WIDTH)
        k_new = jnp.pad(k32.reshape(nb, tn_new, QK_WIDTH), ((0, 0), (0, NEW_ROWS - tn_new), (0, 0)))
        v_new = jnp.pad(v32.reshape(nb, tn_new, ATTN_WIDTH), ((0, 0), (0, NEW_ROWS - tn_new), (0, 0)))
        o_s = _sample_attention(page_table, qbd, k_new, v_new,
                                cache_k[l].reshape(n_phys, PAGE_SIZE, QK_WIDTH),
                                cache_v[l].reshape(n_phys, PAGE_SIZE, ATTN_WIDTH),
                                lq1, lk1, lq2, lk2, sub, lam_init, tn_new, pages_per_step)
        ob = o_s[:, :tn_new].reshape(ns, ATTN_WIDTH).astype(BF16)
        u_tb = u.reshape(nb, tn_new, SSM_WIDTH).transpose(1, 0, 2)
        y_tb, hre_s, him_s = _ssm_sample(u_tb, state_ssm_re[l].reshape(nb, N_STATE),
                                         state_ssm_im[l].reshape(nb, N_STATE),
                                         a_re, a_im, b_mat, c_mat, d_row, wg)
        ysb = y_tb.transpose(1, 0, 2).reshape(ns, SSM_WIDTH)
        x1, h2b, s_t, e_t, tau = _mix(ys, ob, ysb, ga, gs, g1, sc2, sh2, npost_mix, npre_ffn,
                                      wao, wso, wout, wq, skt, ns)
        ys = _peer(h2b, u_bf, vt_bf, s_t, e_t, tau, x1, g2, npost_ffn, ns, 512)
        outs[4].append(k32.reshape(nb, tn_new, N_HEADS, 2 * HEAD_DIM))
        outs[5].append(v32.reshape(nb, tn_new, N_HEADS, V_DIM))
        outs[6].append(hre_s.reshape(nb, N_GROUPS, STATE_DIM))
        outs[7].append(him_s.reshape(nb, N_GROUPS, STATE_DIM))

    st = [jnp.stack(o) for o in outs]
    return (yp.reshape(bp, tp, D_MODEL), ys.reshape(nb, tn_new, D_MODEL),
            st[0], st[1], st[2], st[3], st[4], st[5], st[6], st[7])
```

```python
import functools
import math

import jax
import jax.numpy as jnp
from jax import lax
from jax.experimental import pallas as pl
from jax.experimental.pallas import tpu as pltpu

F32 = jnp.float32
BF16 = jnp.bfloat16

D_MODEL = 1024
N_HEADS = 8
HEAD_DIM = 64
V_DIM = 2 * HEAD_DIM
QK_WIDTH = N_HEADS * 2 * HEAD_DIM
ATTN_WIDTH = N_HEADS * V_DIM
ROPE_THETA = 10000.0
PAGE_SIZE = 128
SSM_WIDTH = 512
SSM_GROUP = 16
N_GROUPS = SSM_WIDTH // SSM_GROUP
STATE_DIM = 64
N_STATE = N_GROUPS * STATE_DIM
SSM_CHUNKS = 4
SSM_CHUNK_IN = SSM_WIDTH // SSM_CHUNKS
SSM_CHUNK_ST = N_STATE // SSM_CHUNKS
N_KEYS = 128
N_EXPERTS = N_KEYS * N_KEYS
PEER_HEADS = 8
PEER_KEY_DIM = 128
PEER_TOPK = 16
N_MOD = 6
RMS_EPS = 1e-6

VMEM_LIMIT_BYTES = 56 * 1024 * 1024

NEG_INF = float("-inf")
Q_SCALE = (HEAD_DIM ** -0.5) * math.log2(math.e)
NEW_ROWS = 16


def _cparams(*sem):
    return pltpu.CompilerParams(dimension_semantics=sem, vmem_limit_bytes=VMEM_LIMIT_BYTES)


def _dot(a, b):
    return jnp.dot(a, b, preferred_element_type=F32)


def _dot_nt(a, b):
    return lax.dot_general(a, b, (((1,), (1,)), ((), ())), preferred_element_type=F32)


def _rms(x, w):
    return x * lax.rsqrt(jnp.mean(x * x, axis=-1, keepdims=True) + RMS_EPS) * w


def _gelu(x):
    return 0.5 * x * (1.0 + jnp.tanh(0.7978845608028654 * (x + 0.044715 * (x * x * x))))


def _sigmoid(x):
    return 1.0 / (1.0 + jnp.exp(-x))


def _mod_spec(rows_per_mod, tm):
    if rows_per_mod == 1:
        return pl.BlockSpec((1, D_MODEL), lambda i: (0, 0))
    return pl.BlockSpec((tm, D_MODEL), lambda i: (i, 0))


def _ada_kernel(c_ref, w_ref, b_ref, o_ref):
    c = c_ref[...]
    s = c * _sigmoid(c)
    o_ref[...] = _dot(s.astype(BF16), w_ref[...].astype(BF16)) + b_ref[...]


def _ada(c_all, w_ada, b_ada):
    rows = c_all.shape[0]
    tn = 1536
    return pl.pallas_call(
        _ada_kernel,
        grid=(N_MOD * D_MODEL // tn,),
        in_specs=[pl.BlockSpec((rows, D_MODEL), lambda j: (0, 0)),
                  pl.BlockSpec((D_MODEL, tn), lambda j: (0, j)),
                  pl.BlockSpec((1, tn), lambda j: (0, j))],
        out_specs=pl.BlockSpec((rows, tn), lambda j: (0, j)),
        out_shape=jax.ShapeDtypeStruct((rows, N_MOD * D_MODEL), F32),
        compiler_params=_cparams("arbitrary"),
        name="ada_mod",
    )(c_all, w_ada, b_ada.reshape(1, -1))


def _inproj_kernel(x_ref, nw_ref, sc_ref, sh_ref, cos_ref, sin_ref, w_ref,
                   q2_ref, k32_ref, kb_ref, v32_ref, vb_ref, u_ref, ga_ref, gs_ref):
    x = x_ref[...]
    h = _rms(x, nw_ref[...]) * (1.0 + sc_ref[...]) + sh_ref[...]
    hb = h.astype(BF16)
    cos = cos_ref[...]
    sin = sin_ref[...]
    lane = lax.broadcasted_iota(jnp.int32, cos.shape, 1)
    first = lane < HEAD_DIM
    W = QK_WIDTH
    for hd in range(N_HEADS):
        c0 = hd * V_DIM
        q = _dot(hb, w_ref[:, c0:c0 + V_DIM])
        qs = _dot(hb, w_ref[:, W + c0:W + c0 + V_DIM])
        qr = (q * cos + qs * sin) * Q_SCALE
        q2_ref[0, :, c0:c0 + V_DIM] = jnp.where(first, qr, 0.0).astype(BF16)
        q2_ref[1, :, c0:c0 + V_DIM] = jnp.where(first, 0.0, qr).astype(BF16)
        k = _dot(hb, w_ref[:, 2 * W + c0:2 * W + c0 + V_DIM])
        ks = _dot(hb, w_ref[:, 3 * W + c0:3 * W + c0 + V_DIM])
        kr = k * cos + ks * sin
        k32_ref[:, c0:c0 + V_DIM] = kr
        kb_ref[:, c0:c0 + V_DIM] = kr.astype(BF16)
    o = 4 * W
    v = _dot(hb, w_ref[:, o:o + ATTN_WIDTH])
    v32_ref[...] = v
    vb_ref[...] = v.astype(BF16)
    o += ATTN_WIDTH
    u_ref[...] = _dot(hb, w_ref[:, o:o + SSM_WIDTH])
    o += SSM_WIDTH
    ga_ref[...] = _dot(hb, w_ref[:, o:o + D_MODEL])
    o += D_MODEL
    gs_ref[...] = _dot(hb, w_ref[:, o:o + D_MODEL])


def _inproj(x, nw, sc, sh, cosf, sinf, w_ext, tm):
    n = x.shape[0]
    wcols = w_ext.shape[1]
    row = lambda i: (i, 0)
    full = lambda i: (0, 0)
    outs = (
        jax.ShapeDtypeStruct((2, n, QK_WIDTH), BF16),
        jax.ShapeDtypeStruct((n, QK_WIDTH), F32),
        jax.ShapeDtypeStruct((n, QK_WIDTH), BF16),
        jax.ShapeDtypeStruct((n, ATTN_WIDTH), F32),
        jax.ShapeDtypeStruct((n, ATTN_WIDTH), BF16),
        jax.ShapeDtypeStruct((n, SSM_WIDTH), F32),
        jax.ShapeDtypeStruct((n, D_MODEL), F32),
        jax.ShapeDtypeStruct((n, D_MODEL), F32),
    )
    out_specs = (
        pl.BlockSpec((2, tm, QK_WIDTH), lambda i: (0, i, 0)),
        pl.BlockSpec((tm, QK_WIDTH), row),
        pl.BlockSpec((tm, QK_WIDTH), row),
        pl.BlockSpec((tm, ATTN_WIDTH), row),
        pl.BlockSpec((tm, ATTN_WIDTH), row),
        pl.BlockSpec((tm, SSM_WIDTH), row),
        pl.BlockSpec((tm, D_MODEL), row),
        pl.BlockSpec((tm, D_MODEL), row),
    )
    return pl.pallas_call(
        _inproj_kernel,
        grid=(n // tm,),
        in_specs=[pl.BlockSpec((tm, D_MODEL), row),
                  pl.BlockSpec((1, D_MODEL), full),
                  _mod_spec(sc.shape[0], tm),
                  _mod_spec(sh.shape[0], tm),
                  pl.BlockSpec((tm, V_DIM), row),
                  pl.BlockSpec((tm, V_DIM), row),
                  pl.BlockSpec((D_MODEL, wcols), full)],
        out_specs=out_specs,
        out_shape=outs,
        compiler_params=_cparams("arbitrary"),
        name="in_proj",
    )(x, nw, sc, sh, cosf, sinf, w_ext)


def _lambda_full(lq1_ref, lk1_ref, lq2_ref, lk2_ref, lam_init):
    s1 = jnp.sum(lq1_ref[...] * lk1_ref[...], axis=-1, keepdims=True)
    s2 = jnp.sum(lq2_ref[...] * lk2_ref[...], axis=-1, keepdims=True)
    return jnp.exp(s1) - jnp.exp(s2) + lam_init


def _subln(o, w, lam_init):
    return _rms(o, w) * (1.0 - lam_init)


def _pattn_kernel(q_ref, k_ref, v_ref, lq1_ref, lk1_ref, lq2_ref, lk2_ref, sub_ref, o_ref,
                  m_ref, acc_ref, *, tq, lam_init):
    i = pl.program_id(1)
    q = jnp.concatenate([q_ref[0], q_ref[1]], axis=0)
    m_ref[...] = jnp.full(m_ref.shape, NEG_INF, F32)
    acc_ref[...] = jnp.zeros(acc_ref.shape, F32)
    ones = jnp.ones((tq, V_DIM), BF16)

    def step(j, masked):
        start = pl.multiple_of(j * tq, tq)
        k = k_ref[pl.ds(start, tq), :]
        v1 = jnp.concatenate([v_ref[pl.ds(start, tq), :], ones], axis=1)
        s = _dot_nt(q, k)
        if masked:
            r = lax.broadcasted_iota(jnp.int32, (tq, tq), 0)
            c = lax.broadcasted_iota(jnp.int32, (tq, tq), 1)
            keep = jnp.concatenate([c <= r, c <= r], axis=0)
            s = jnp.where(keep, s, NEG_INF)
        m_old = m_ref[...]
        m_new = jnp.maximum(m_old, jnp.max(s, axis=-1, keepdims=True))
        p = jnp.exp2(s - jnp.tile(m_new, (1, tq // V_DIM)))
        alpha = jnp.exp2(m_old - m_new)
        acc_ref[...] = jnp.tile(alpha, (1, 2)) * acc_ref[...] + _dot(p.astype(BF16), v1)
        m_ref[...] = m_new

    def body(j, carry):
        step(j, False)
        return carry

    lax.fori_loop(0, i, body, 0)
    step(i, True)

    lam = _lambda_full(lq1_ref, lk1_ref, lq2_ref, lk2_ref, lam_init)
    acc = acc_ref[...]
    o = acc[:, :V_DIM] / acc[:, V_DIM:]
    d = o[:tq] - lam * o[tq:]
    o_ref[...] = _subln(d, sub_ref[...], lam_init).astype(BF16)


def _prompt_attention(q2, kb, vb, lq1, lk1, lq2, lk2, subln, lam_init, tq):
    t = kb.shape[0]
    vec = lambda h, i: (0, 0)
    kern = functools.partial(_pattn_kernel, tq=tq, lam_init=lam_init)
    return pl.pallas_call(
        kern,
        grid=(N_HEADS, t // tq),
        in_specs=[pl.BlockSpec((2, tq, V_DIM), lambda h, i: (0, i, h)),
                  pl.BlockSpec((t, V_DIM), lambda h, i: (0, h)),
                  pl.BlockSpec((t, V_DIM), lambda h, i: (0, h)),
                  pl.BlockSpec((1, HEAD_DIM), vec),
                  pl.BlockSpec((1, HEAD_DIM), vec),
                  pl.BlockSpec((1, HEAD_DIM), vec),
                  pl.BlockSpec((1, HEAD_DIM), vec),
                  pl.BlockSpec((1, V_DIM), vec)],
        out_specs=pl.BlockSpec((tq, V_DIM), lambda h, i: (i, h)),
        out_shape=jax.ShapeDtypeStruct((t, ATTN_WIDTH), BF16),
        scratch_shapes=[pltpu.VMEM((2 * tq, V_DIM), F32),
                        pltpu.VMEM((2 * tq, 2 * V_DIM), F32)],
        compiler_params=_cparams("arbitrary", "arbitrary"),
        name="prompt_attn",
    )(q2, kb, vb, lq1, lk1, lq2, lk2, subln)


def _sattn_kernel(pt_ref, q_ref, kn_ref, vn_ref, lq1_ref, lk1_ref, lq2_ref, lk2_ref, sub_ref, *rest,
                  pages_per_step, n_new, lam_init):
    g_pages = pages_per_step
    k_refs = rest[:g_pages]
    v_refs = rest[g_pages:2 * g_pages]
    o_ref = rest[2 * g_pages]
    m_ref, l_ref, acc_ref, bias_ref = rest[2 * g_pages + 1:]
    b = pl.program_id(0)
    g = pl.program_id(1)
    q = q_ref[0]
    rows_per_head = 2 * n_new

    @pl.when((b == 0) & (g == 0))
    def _():
        r = lax.broadcasted_iota(jnp.int32, bias_ref.shape, 0) // rows_per_head
        c = lax.broadcasted_iota(jnp.int32, bias_ref.shape, 1) % N_HEADS
        bias_ref[...] = jnp.where(r == c, 0.0, NEG_INF)

    @pl.when(g == 0)
    def _():
        m_ref[...] = jnp.full(m_ref.shape, NEG_INF, F32)
        l_ref[...] = jnp.zeros(l_ref.shape, F32)
        acc_ref[...] = jnp.zeros(acc_ref.shape, F32)

    def update(s, v):
        m_old = m_ref[...]
        m_new = jnp.maximum(m_old, jnp.max(s, axis=-1, keepdims=True))
        p = jnp.exp2(s - m_new)
        alpha = jnp.exp2(m_old - m_new)
        l_ref[...] = alpha * l_ref[...] + jnp.sum(p, axis=-1, keepdims=True)
        acc_ref[...] = alpha * acc_ref[...] + _dot(p.astype(BF16), v)
        m_ref[...] = m_new

    k = jnp.concatenate([r[0].astype(BF16) for r in k_refs], axis=0)
    v = jnp.concatenate([r[0].astype(BF16) for r in v_refs], axis=0)
    update(_dot_nt(q, k) + bias_ref[...], v)

    @pl.when(g == pl.num_programs(1) - 1)
    def _():
        kn = kn_ref[0].astype(BF16)
        vn = vn_ref[0].astype(BF16)
        s = _dot_nt(q, kn)
        row = lax.broadcasted_iota(jnp.int32, s.shape, 0)
        col = lax.broadcasted_iota(jnp.int32, s.shape, 1)
        keep = (row // rows_per_head == col % N_HEADS) & (col // N_HEADS <= row % n_new)
        update(jnp.where(keep, s, NEG_INF), vn)
        lam = _lambda_full(lq1_ref, lk1_ref, lq2_ref, lk2_ref, lam_init)
        o = acc_ref[...] / l_ref[...]
        d = o - lam * pltpu.roll(o, o.shape[0] - n_new, 0)
        o_ref[0] = _subln(d, sub_ref[...], lam_init)


def _sample_attention(page_table, q_rows, k_new, v_new, cache_k, cache_v, lq1, lk1, lq2, lk2, subln,
                      lam_init, n_new, pages_per_step):
    nb, n_pages = page_table.shape
    gp = pages_per_step
    rows = q_rows.shape[1]
    page_rows = PAGE_SIZE * N_HEADS
    new_rows = k_new.shape[1]
    vec = lambda b, g, pt: (0, 0)

    def page_spec(r):
        return pl.BlockSpec((1, page_rows, V_DIM),
                            lambda b, g, pt: (pt[b * n_pages + g * gp + r], 0, 0))

    kern = functools.partial(_sattn_kernel, pages_per_step=gp, n_new=n_new, lam_init=lam_init)
    grid_spec = pltpu.PrefetchScalarGridSpec(
        num_scalar_prefetch=1,
        grid=(nb, n_pages // gp),
        in_specs=[pl.BlockSpec((1, rows, V_DIM), lambda b, g, pt: (b, 0, 0)),
                  pl.BlockSpec((1, new_rows, V_DIM), lambda b, g, pt: (b, 0, 0)),
                  pl.BlockSpec((1, new_rows, V_DIM), lambda b, g, pt: (b, 0, 0)),
                  pl.BlockSpec((1, HEAD_DIM), vec),
                  pl.BlockSpec((1, HEAD_DIM), vec),
                  pl.BlockSpec((1, HEAD_DIM), vec),
                  pl.BlockSpec((1, HEAD_DIM), vec),
                  pl.BlockSpec((1, V_DIM), vec)]
                 + [page_spec(r) for r in range(gp)] + [page_spec(r) for r in range(gp)],
        out_specs=pl.BlockSpec((1, rows, V_DIM), lambda b, g, pt: (b, 0, 0)),
        scratch_shapes=[pltpu.VMEM((rows, 1), F32),
                        pltpu.VMEM((rows, 1), F32),
                        pltpu.VMEM((rows, V_DIM), F32),
                        pltpu.VMEM((rows, gp * page_rows), F32)],
    )
    return pl.pallas_call(
        kern,
        grid_spec=grid_spec,
        out_shape=jax.ShapeDtypeStruct((nb, rows, V_DIM), F32),
        compiler_params=_cparams("arbitrary", "arbitrary"),
        name="sample_attn",
    )(page_table.reshape(-1), q_rows, k_new, v_new, lq1, lk1, lq2, lk2, subln,
      *([cache_k] * gp), *([cache_v] * gp))


def _ssm_disc_kernel(lr_ref, li_ref, ls_ref, are_ref, aim_ref, fre_ref, fim_ref, *, n_pow):
    lr = lr_ref[...]
    li = li_ref[...]
    dt = jnp.exp(ls_ref[...])
    mag = jnp.exp(lr * dt)
    a_re = mag * jnp.cos(li * dt)
    a_im = mag * jnp.sin(li * dt)
    den = lr * lr + li * li
    nr = a_re - 1.0
    ni = a_im
    fre_ref[...] = (nr * lr + ni * li) / den
    fim_ref[...] = (ni * lr - nr * li) / den
    pr, pi = a_re, a_im
    for kk in range(n_pow):
        are_ref[kk] = pr
        aim_ref[kk] = pi
        pr, pi = pr * pr - pi * pi, 2.0 * pr * pi


def _ssm_bb_kernel(fre_ref, fim_ref, bre_ref, bim_ref, ore_ref, oim_ref):
    fr = fre_ref[...]
    fi = fim_ref[...]
    br = bre_ref[...]
    bi = bim_ref[...]
    ore_ref[...] = fr * br - fi * bi
    oim_ref[...] = fr * bi + fi * br


def _ssm_prepare(lam_re, lam_im, log_step, b_re, b_im, c_re, c_im, n_pow):
    gshape = jax.ShapeDtypeStruct((N_GROUPS, STATE_DIM), F32)
    pshape = jax.ShapeDtypeStruct((n_pow, N_GROUPS, STATE_DIM), F32)
    a_re, a_im, f_re, f_im = pl.pallas_call(
        functools.partial(_ssm_disc_kernel, n_pow=n_pow),
        out_shape=(pshape, pshape, gshape, gshape),
        name="ssm_discretise",
    )(lam_re, lam_im, log_step.reshape(N_GROUPS, 1))
    bshape = jax.ShapeDtypeStruct((N_STATE, SSM_GROUP), F32)
    bb_re, bb_im = pl.pallas_call(
        _ssm_bb_kernel, out_shape=(bshape, bshape), name="ssm_input_matrix",
    )(f_re.reshape(N_STATE, 1), f_im.reshape(N_STATE, 1),
      b_re.reshape(N_STATE, SSM_GROUP), b_im.reshape(N_STATE, SSM_GROUP))
    gc = N_GROUPS // SSM_CHUNKS
    eye = jnp.eye(gc, dtype=F32)

    def in_mat(bb):
        b4 = bb.reshape(SSM_CHUNKS, gc, STATE_DIM, SSM_GROUP)
        m = jnp.einsum('cgpi,gh->cgihp', b4, eye)
        return m.reshape(SSM_CHUNKS, gc * SSM_GROUP, gc * STATE_DIM)

    def out_mat(cc):
        c4 = cc.reshape(SSM_CHUNKS, gc, SSM_GROUP, STATE_DIM)
        m = jnp.einsum('cgip,gh->cgphi', c4, eye)
        return m.reshape(SSM_CHUNKS, gc * STATE_DIM, gc * SSM_GROUP)

    b_mat = jnp.concatenate([in_mat(bb_re), in_mat(bb_im)], axis=-1).astype(BF16)
    c_mat = jnp.concatenate([out_mat(c_re), out_mat(-c_im)], axis=-2).astype(BF16)
    return (a_re.reshape(n_pow, N_STATE), a_im.reshape(n_pow, N_STATE), b_mat, c_mat)


def _shift_rows(x, s, row):
    return jnp.where(row >= s, pltpu.roll(x, s, 0), 0.0)


def _ssm_scan_kernel(u_ref, are_ref, aim_ref, b_ref, c_ref, d_ref, wg_ref,
                     y_ref, hre_ref, him_ref, cre_ref, cim_ref, *, tm, n_pow):
    @pl.when(pl.program_id(0) == 0)
    def _():
        cre_ref[...] = jnp.zeros(cre_ref.shape, F32)
        cim_ref[...] = jnp.zeros(cim_ref.shape, F32)

    u = u_ref[...]
    ub = u.astype(BF16)
    row = lax.broadcasted_iota(jnp.int32, (tm, SSM_CHUNK_ST), 0)
    ys = []
    for c in range(SSM_CHUNKS):
        lanes = slice(c * SSM_CHUNK_ST, (c + 1) * SSM_CHUNK_ST)
        bu = _dot(ub[:, c * SSM_CHUNK_IN:(c + 1) * SSM_CHUNK_IN], b_ref[c])
        xr = bu[:, :SSM_CHUNK_ST]
        xi = bu[:, SSM_CHUNK_ST:]
        a_r = are_ref[0:1, lanes]
        a_i = aim_ref[0:1, lanes]
        h_r = cre_ref[0:1, lanes]
        h_i = cim_ref[0:1, lanes]
        first = row == 0
        xr = xr + jnp.where(first, a_r * h_r - a_i * h_i, 0.0)
        xi = xi + jnp.where(first, a_r * h_i + a_i * h_r, 0.0)
        for kk in range(n_pow):
            s = 1 << kk
            p_r = are_ref[kk:kk + 1, lanes]
            p_i = aim_ref[kk:kk + 1, lanes]
            sr = _shift_rows(xr, s, row)
            si = _shift_rows(xi, s, row)
            xr, xi = xr + (p_r * sr - p_i * si), xi + (p_r * si + p_i * sr)
        cre_ref[0:1, lanes] = xr[tm - 1:tm]
        cim_ref[0:1, lanes] = xi[tm - 1:tm]
        hcat = jnp.concatenate([xr, xi], axis=1).astype(BF16)
        ys.append(_dot(hcat, c_ref[c]))
    y = jnp.concatenate(ys, axis=1) + d_ref[...] * u
    y = _gelu(y)
    y = y * _sigmoid(_dot(y.astype(BF16), wg_ref[...]))
    y_ref[...] = y.astype(BF16)
    hre_ref[...] = cre_ref[0:1, :]
    him_ref[...] = cim_ref[0:1, :]


def _ssm_prompt(u, a_re, a_im, b_mat, c_mat, d, wg, tm):
    t = u.shape[0]
    n_pow = a_re.shape[0]
    full2 = lambda i: (0, 0)
    full3 = lambda i: (0, 0, 0)
    kern = functools.partial(_ssm_scan_kernel, tm=tm, n_pow=n_pow)
    return pl.pallas_call(
        kern,
        grid=(t // tm,),
        in_specs=[pl.BlockSpec((tm, SSM_WIDTH), lambda i: (i, 0)),
                  pl.BlockSpec((n_pow, N_STATE), full2),
                  pl.BlockSpec((n_pow, N_STATE), full2),
                  pl.BlockSpec(b_mat.shape, full3),
                  pl.BlockSpec(c_mat.shape, full3),
                  pl.BlockSpec((1, SSM_WIDTH), full2),
                  pl.BlockSpec((SSM_WIDTH, SSM_WIDTH), full2)],
        out_specs=(pl.BlockSpec((tm, SSM_WIDTH), lambda i: (i, 0)),
                   pl.BlockSpec((1, N_STATE), full2),
                   pl.BlockSpec((1, N_STATE), full2)),
        out_shape=(jax.ShapeDtypeStruct((t, SSM_WIDTH), BF16),
                   jax.ShapeDtypeStruct((1, N_STATE), F32),
                   jax.ShapeDtypeStruct((1, N_STATE), F32)),
        scratch_shapes=[pltpu.VMEM((8, N_STATE), F32), pltpu.VMEM((8, N_STATE), F32)],
        compiler_params=_cparams("arbitrary"),
        name="ssm_prompt",
    )(u, a_re, a_im, b_mat, c_mat, d, wg)


def _ssm_step_kernel(u_ref, h0re_ref, h0im_ref, are_ref, aim_ref, b_ref, c_ref, d_ref, wg_ref,
                     y_ref, hre_ref, him_ref, *, n_steps):
    a_r = are_ref[0:1, :]
    a_i = aim_ref[0:1, :]
    h_r = h0re_ref[...]
    h_i = h0im_ref[...]
    for t in range(n_steps):
        u = u_ref[t]
        ub = u.astype(BF16)
        brs, bis = [], []
        for c in range(SSM_CHUNKS):
            bu = _dot(ub[:, c * SSM_CHUNK_IN:(c + 1) * SSM_CHUNK_IN], b_ref[c])
            brs.append(bu[:, :SSM_CHUNK_ST])
            bis.append(bu[:, SSM_CHUNK_ST:])
        bu_r = jnp.concatenate(brs, axis=1)
        bu_i = jnp.concatenate(bis, axis=1)
        h_r, h_i = a_r * h_r - a_i * h_i + bu_r, a_r * h_i + a_i * h_r + bu_i
        ys = []
        for c in range(SSM_CHUNKS):
            lanes = slice(c * SSM_CHUNK_ST, (c + 1) * SSM_CHUNK_ST)
            hcat = jnp.concatenate([h_r[:, lanes], h_i[:, lanes]], axis=1).astype(BF16)
            ys.append(_dot(hcat, c_ref[c]))
        y = jnp.concatenate(ys, axis=1) + d_ref[...] * u
        y = _gelu(y)
        y = y * _sigmoid(_dot(y.astype(BF16), wg_ref[...]))
        y_ref[t] = y.astype(BF16)
    hre_ref[...] = h_r
    him_ref[...] = h_i


def _ssm_sample(u_tb, h0_re, h0_im, a_re, a_im, b_mat, c_mat, d, wg):
    n_steps, nb, _ = u_tb.shape
    kern = functools.partial(_ssm_step_kernel, n_steps=n_steps)
    return pl.pallas_call(
        kern,
        out_shape=(jax.ShapeDtypeStruct((n_steps, nb, SSM_WIDTH), BF16),
                   jax.ShapeDtypeStruct((nb, N_STATE), F32),
                   jax.ShapeDtypeStruct((nb, N_STATE), F32)),
        compiler_params=pltpu.CompilerParams(vmem_limit_bytes=VMEM_LIMIT_BYTES),
        name="ssm_sample",
    )(u_tb, h0_re, h0_im, a_re, a_im, b_mat, c_mat, d, wg)


def _kth_largest_rows(c, k):
    cnt = jnp.zeros((1, c.shape[1]), F32)
    tau = jnp.full((1, c.shape[1]), NEG_INF, F32)
    for _ in range(k):
        m = jnp.max(c, axis=0, keepdims=True)
        eq = c == m
        tau = jnp.where(cnt < k, m, tau)
        cnt = cnt + jnp.sum(eq.astype(F32), axis=0, keepdims=True)
        c = jnp.where(eq, NEG_INF, c)
    return tau


def _top_rows(a, k, out_ref):
    for r in range(k):
        m = jnp.max(a, axis=0, keepdims=True)
        out_ref[r:r + 1, :] = m
        a = jnp.where(a == m, NEG_INF, a)


def _mix_kernel(x_ref, o_ref, ys_ref, ga_ref, gs_ref, g1_ref, sc2_ref, sh2_ref, npost_ref, npre_ref,
                wao_ref, wso_ref, wout_ref, wq_ref, skt_ref,
                x1_ref, h2_ref, c1_ref, e1_ref, r2_ref, e2_ref, v1_ref, v2_ref):
    y_attn = _dot(o_ref[...], wao_ref[...])
    y_ssm = _dot(ys_ref[...], wso_ref[...])
    merged = _sigmoid(ga_ref[...]) * y_attn + _sigmoid(gs_ref[...]) * y_ssm
    mix = _dot(merged.astype(BF16), wout_ref[...])
    x1 = x_ref[...] + g1_ref[...] * _rms(mix, npost_ref[...])
    x1_ref[...] = x1
    h2 = _rms(x1, npre_ref[...]) * (1.0 + sc2_ref[...]) + sh2_ref[...]
    h2b = h2.astype(BF16)
    h2_ref[...] = h2b
    qp = _dot(h2b, wq_ref[...]).astype(BF16)
    s_all = _dot_nt(skt_ref[...], qp)
    for hd in range(PEER_HEADS):
        r0 = hd * 2 * N_KEYS
        k0 = hd * N_KEYS
        s1 = s_all[r0:r0 + N_KEYS]
        s2 = s_all[r0 + N_KEYS:r0 + 2 * N_KEYS]
        _top_rows(s1, PEER_TOPK, v1_ref)
        _top_rows(s2, PEER_TOPK, v2_ref)
        v1a = v1_ref[...]
        cands = [v1a + v2_ref[b:b + 1, :] for b in range(PEER_TOPK)]
        cand = jnp.concatenate(cands, axis=0)
        tau = _kth_largest_rows(cand, PEER_TOPK)
        m1 = v1_ref[0:1, :]
        m2 = v2_ref[0:1, :]
        z = jnp.sum(jnp.where(cand >= tau, jnp.exp(cand - (m1 + m2)), 0.0), axis=0, keepdims=True)
        cnt = jnp.zeros(v1a.shape, F32)
        for cb in cands:
            cnt = cnt + (cb >= tau).astype(F32)
        c1 = jnp.zeros(s1.shape, F32)
        r2 = jnp.full(s2.shape, float(N_KEYS - 1), F32)
        for r in range(PEER_TOPK):
            c1 = jnp.where(s1 == v1_ref[r:r + 1, :], cnt[r:r + 1], c1)
            r2 = jnp.where(s2 == v2_ref[r:r + 1, :], float(r), r2)
        c1_ref[k0:k0 + N_KEYS, :] = c1
        e1_ref[k0:k0 + N_KEYS, :] = jnp.exp(s1 - m1)
        r2_ref[k0:k0 + N_KEYS, :] = r2.astype(BF16)
        e2_ref[k0:k0 + N_KEYS, :] = (jnp.exp(s2 - m2) / z).astype(BF16)


def _mix(x, ob, ysb, ga, gs, g1, sc2, sh2, npost, npre, wao, wso, wout, wq, skt, tm):
    n = x.shape[0]
    row = lambda i: (i, 0)
    col = lambda i: (0, i)
    full = lambda i: (0, 0)
    nrow = 2 * N_KEYS * PEER_HEADS
    krow = N_KEYS * PEER_HEADS
    return pl.pallas_call(
        _mix_kernel,
        grid=(n // tm,),
        in_specs=[pl.BlockSpec((tm, D_MODEL), row),
                  pl.BlockSpec((tm, ATTN_WIDTH), row),
                  pl.BlockSpec((tm, SSM_WIDTH), row),
                  pl.BlockSpec((tm, D_MODEL), row),
                  pl.BlockSpec((tm, D_MODEL), row),
                  _mod_spec(g1.shape[0], tm),
                  _mod_spec(sc2.shape[0], tm),
                  _mod_spec(sh2.shape[0], tm),
                  pl.BlockSpec((1, D_MODEL), full),
                  pl.BlockSpec((1, D_MODEL), full),
                  pl.BlockSpec((ATTN_WIDTH, D_MODEL), full),
                  pl.BlockSpec((SSM_WIDTH, D_MODEL), full),
                  pl.BlockSpec((D_MODEL, D_MODEL), full),
                  pl.BlockSpec((D_MODEL, PEER_HEADS * PEER_KEY_DIM), full),
                  pl.BlockSpec((nrow, PEER_HEADS * PEER_KEY_DIM), full)],
        out_specs=(pl.BlockSpec((tm, D_MODEL), row),
                   pl.BlockSpec((tm, D_MODEL), row),
                   pl.BlockSpec((krow, tm), col),
                   pl.BlockSpec((krow, tm), col),
                   pl.BlockSpec((krow, tm), col),
                   pl.BlockSpec((krow, tm), col)),
        out_shape=(jax.ShapeDtypeStruct((n, D_MODEL), F32),
                   jax.ShapeDtypeStruct((n, D_MODEL), BF16),
                   jax.ShapeDtypeStruct((krow, n), F32),
                   jax.ShapeDtypeStruct((krow, n), F32),
                   jax.ShapeDtypeStruct((krow, n), BF16),
                   jax.ShapeDtypeStruct((krow, n), BF16)),
        scratch_shapes=[pltpu.VMEM((PEER_TOPK, tm), F32), pltpu.VMEM((PEER_TOPK, tm), F32)],
        compiler_params=_cparams("arbitrary"),
        name="mix_route",
    )(x, ob, ysb, ga, gs, g1, sc2, sh2, npost, npre, wao, wso, wout, wq, skt)


def _rows_bf16(row, n_rows):
    tile = jnp.broadcast_to(row, (16, row.shape[1])).astype(BF16)
    return jnp.concatenate([tile] * (n_rows // 16), axis=0)


def _peer_kernel(h2_ref, u_ref, vt_ref, c1_ref, e1_ref, r2_ref, e2_ref, x1_ref, g2_ref, nw_ref,
                 y_ref, acc_ref, *, ec):
    c = pl.program_id(1)

    @pl.when(c == 0)
    def _():
        acc_ref[...] = jnp.zeros(acc_ref.shape, F32)

    act = _dot_nt(u_ref[...], h2_ref[...])
    ws = []
    for ii in range(ec // N_KEYS):
        key1 = c * (ec // N_KEYS) + ii
        gate = jnp.zeros((N_KEYS, act.shape[1]), BF16)
        for hd in range(PEER_HEADS):
            k0 = hd * N_KEYS
            cnt = _rows_bf16(c1_ref[pl.ds(k0 + key1, 1), :], N_KEYS)
            e1 = _rows_bf16(e1_ref[pl.ds(k0 + key1, 1), :], N_KEYS)
            r2 = r2_ref[k0:k0 + N_KEYS, :]
            e2 = e2_ref[k0:k0 + N_KEYS, :]
            gate = gate + jnp.where(r2 < cnt, e2, jnp.zeros_like(e2)) * e1
        ws.append(gate * _gelu(act[ii * N_KEYS:(ii + 1) * N_KEYS]).astype(BF16))
    w = jnp.concatenate(ws, axis=0) if len(ws) > 1 else ws[0]
    acc_ref[...] += _dot(vt_ref[...], w)

    @pl.when(c == pl.num_programs(1) - 1)
    def _():
        f = acc_ref[...].T
        y_ref[...] = x1_ref[...] + g2_ref[...] * _rms(f, nw_ref[...])


def _peer(h2b, u_bf, vt_bf, c1, e1, r2, e2, x1, g2, nw, tn, ec):
    n = h2b.shape[0]
    krow = c1.shape[0]
    kern = functools.partial(_peer_kernel, ec=ec)
    g2_spec = (pl.BlockSpec((1, D_MODEL), lambda t, c: (0, 0)) if g2.shape[0] == 1
               else pl.BlockSpec((tn, D_MODEL), lambda t, c: (t, 0)))
    return pl.pallas_call(
        kern,
        grid=(n // tn, N_EXPERTS // ec),
        in_specs=[pl.BlockSpec((tn, D_MODEL), lambda t, c: (t, 0)),
                  pl.BlockSpec((ec, D_MODEL), lambda t, c: (c, 0)),
                  pl.BlockSpec((D_MODEL, ec), lambda t, c: (0, c)),
                  pl.BlockSpec((krow, tn), lambda t, c: (0, t)),
                  pl.BlockSpec((krow, tn), lambda t, c: (0, t)),
                  pl.BlockSpec((krow, tn), lambda t, c: (0, t)),
                  pl.BlockSpec((krow, tn), lambda t, c: (0, t)),
                  pl.BlockSpec((tn, D_MODEL), lambda t, c: (t, 0)),
                  g2_spec,
                  pl.BlockSpec((1, D_MODEL), lambda t, c: (0, 0))],
        out_specs=pl.BlockSpec((tn, D_MODEL), lambda t, c: (t, 0)),
        out_shape=jax.ShapeDtypeStruct((n, D_MODEL), F32),
        scratch_shapes=[pltpu.VMEM((D_MODEL, tn), F32)],
        compiler_params=_cparams("arbitrary", "arbitrary"),
        name="peer_experts",
    )(h2b, u_bf, vt_bf, c1, e1, r2, e2, x1, g2, nw)


def _rope_tables(pos):
    half = HEAD_DIM // 2
    inv = ROPE_THETA ** (-jnp.arange(half, dtype=F32) / half)
    ang = pos.astype(F32)[:, None] * inv[None, :]
    cos = jnp.cos(ang)
    sin = jnp.sin(ang)
    cosf = jnp.concatenate([cos, cos, cos, cos], axis=1)
    sinf = jnp.concatenate([-sin, sin, -sin, sin], axis=1)
    return cosf, sinf


def _swap_halves(w):
    half = HEAD_DIM // 2
    k = w.shape[0]
    return w.reshape(k, -1, 2, half)[:, :, ::-1, :].reshape(k, -1)


def _tile(n, pref):
    t = min(n, pref)
    assert n % t == 0, (n, pref)
    return t


def kernel(x_prompt, x_sample, cache_k, cache_v, state_ssm_re, state_ssm_im, page_table, c_prompt, c_sample, w_ada, b_ada, norm_pre_mix, norm_post_mix, norm_pre_ffn, norm_post_ffn, w_in, lambda_q1, lambda_k1, lambda_q2, lambda_k2, subln, w_attn_out, ssm_lambda_re, ssm_lambda_im, ssm_log_step, ssm_b_re, ssm_b_im, ssm_c_re, ssm_c_im, ssm_d, w_glu, w_ssm_out, w_out, peer_w_query, peer_sub_keys, peer_u, peer_v):
    depth = w_ada.shape[0]
    bp, tp, _ = x_prompt.shape
    nb, tn_new, _ = x_sample.shape
    assert bp == 1 and 2 * tn_new == 8
    n_pages = page_table.shape[1]
    past_len = n_pages * PAGE_SIZE
    n_phys = cache_k.shape[1]
    ns = nb * tn_new

    pos_p = jnp.arange(tp, dtype=jnp.int32)
    pos_s = jnp.tile(past_len + jnp.arange(tn_new, dtype=jnp.int32), nb)
    cos_p, sin_p = _rope_tables(pos_p)
    cos_s, sin_s = _rope_tables(pos_s)

    tm_p = _tile(tp, 256)
    tq = _tile(tp, 512)
    tscan = _tile(tp, 128)
    n_pow = int(math.log2(tscan))
    assert 1 << n_pow == tscan
    pages_per_step = 8 if n_pages % 8 == 0 else 1

    yp = x_prompt.reshape(tp, D_MODEL)
    ys = x_sample.reshape(ns, D_MODEL)
    c_all = jnp.concatenate([c_prompt, c_sample], axis=0)
    pad = (-c_all.shape[0]) % 8
    c_all = jnp.pad(c_all, ((0, pad), (0, 0)))

    outs = [[] for _ in range(8)]
    for l in range(depth):
        lam_init = 0.8 - 0.6 * math.exp(-0.3 * l)
        mod = _ada(c_all, w_ada[l], b_ada[l])
        mod_p = [mod[0:1, j * D_MODEL:(j + 1) * D_MODEL] for j in range(N_MOD)]
        mod_s = [jnp.repeat(mod[1:1 + nb, j * D_MODEL:(j + 1) * D_MODEL], tn_new, axis=0)
                 for j in range(N_MOD)]

        w = w_in[l]
        wq_, wk_ = w[:, :QK_WIDTH], w[:, QK_WIDTH:2 * QK_WIDTH]
        w_ext = jnp.concatenate([wq_, _swap_halves(wq_), wk_, _swap_halves(wk_), w[:, 2 * QK_WIDTH:]],
                                axis=1).astype(BF16)
        a_re, a_im, b_mat, c_mat = _ssm_prepare(ssm_lambda_re[l], ssm_lambda_im[l], ssm_log_step[l],
                                                ssm_b_re[l], ssm_b_im[l], ssm_c_re[l], ssm_c_im[l],
                                                max(n_pow, 1))
        d_row = ssm_d[l].reshape(1, SSM_WIDTH)
        wg = w_glu[l].astype(BF16)
        wao = w_attn_out[l].astype(BF16)
        wso = w_ssm_out[l].astype(BF16)
        wout = w_out[l].astype(BF16)
        wq = peer_w_query[l].astype(BF16)
        sk = peer_sub_keys[l]
        eye = jnp.eye(PEER_HEADS * 2, dtype=F32)
        skt = jnp.einsum('bkd,bc->bkcd', sk.reshape(PEER_HEADS * 2, N_KEYS, PEER_KEY_DIM // 2), eye)
        skt = skt.reshape(PEER_HEADS * 2 * N_KEYS, PEER_HEADS * PEER_KEY_DIM).astype(BF16)
        u_bf = peer_u[l].astype(BF16)
        vt_bf = peer_v[l].T.astype(BF16)
        lq1 = lambda_q1[l].reshape(1, HEAD_DIM)
        lk1 = lambda_k1[l].reshape(1, HEAD_DIM)
        lq2 = lambda_q2[l].reshape(1, HEAD_DIM)
        lk2 = lambda_k2[l].reshape(1, HEAD_DIM)
        sub = subln[l].reshape(1, V_DIM)
        npre_mix = norm_pre_mix[l].reshape(1, D_MODEL)
        npost_mix = norm_post_mix[l].reshape(1, D_MODEL)
        npre_ffn = norm_pre_ffn[l].reshape(1, D_MODEL)
        npost_ffn = norm_post_ffn[l].reshape(1, D_MODEL)

        sh1, sc1, g1, sh2, sc2, g2 = mod_p
        q2, k32, kb, v32, vb, u, ga, gs = _inproj(yp, npre_mix, sc1, sh1, cos_p, sin_p, w_ext, tm_p)
        ob = _prompt_attention(q2, kb, vb, lq1, lk1, lq2, lk2, sub, lam_init, tq)
        ysb, hre, him = _ssm_prompt(u, a_re, a_im, b_mat, c_mat, d_row, wg, tscan)
        x1, h2b, c1, e1, r2, e2 = _mix(yp, ob, ysb, ga, gs, g1, sc2, sh2, npost_mix, npre_ffn,
                                      wao, wso, wout, wq, skt, tm_p)
        yp = _peer(h2b, u_bf, vt_bf, c1, e1, r2, e2, x1, g2, npost_ffn, _tile(tp, 512), 512)
        outs[0].append(k32.reshape(1, tp, N_HEADS, 2 * HEAD_DIM))
        outs[1].append(v32.reshape(1, tp, N_HEADS, V_DIM))
        outs[2].append(hre.reshape(1, N_GROUPS, STATE_DIM))
        outs[3].append(him.reshape(1, N_GROUPS, STATE_DIM))

        sh1, sc1, g1, sh2, sc2, g2 = mod_s
        q2, k32, _, v32, _, u, ga, gs = _inproj(ys, npre_mix, sc1, sh1, cos_s, sin_s, w_ext, ns)
        q_rows = q2.reshape(2, nb, tn_new, N_HEADS, V_DIM).transpose(1, 3, 0, 2, 4)
        q_rows = q_rows.reshape(nb, N_HEADS * 2 * tn_new, V_DIM)
        k_new = jnp.pad(k32.reshape(nb, tn_new * N_HEADS, V_DIM),
                        ((0, 0), (0, (NEW_ROWS - tn_new) * N_HEADS), (0, 0)))
        v_new = jnp.pad(v32.reshape(nb, tn_new * N_HEADS, V_DIM),
                        ((0, 0), (0, (NEW_ROWS - tn_new) * N_HEADS), (0, 0)))
        ck = cache_k.reshape(depth * n_phys, PAGE_SIZE * N_HEADS, V_DIM)
        cv = cache_v.reshape(depth * n_phys, PAGE_SIZE * N_HEADS, V_DIM)
        o_s = _sample_attention(page_table + l * n_phys, q_rows, k_new, v_new, ck, cv,
                                lq1, lk1, lq2, lk2, sub, lam_init, tn_new, pages_per_step)
        ob = o_s.reshape(nb, N_HEADS, 2, tn_new, V_DIM)[:, :, 0].transpose(0, 2, 1, 3)
        ob = ob.reshape(ns, ATTN_WIDTH).astype(BF16)
        u_tb = u.reshape(nb, tn_new, SSM_WIDTH).transpose(1, 0, 2)
        y_tb, hre_s, him_s = _ssm_sample(u_tb, state_ssm_re[l].reshape(nb, N_STATE),
                                         state_ssm_im[l].reshape(nb, N_STATE),
                                         a_re, a_im, b_mat, c_mat, d_row, wg)
        ysb = y_tb.transpose(1, 0, 2).reshape(ns, SSM_WIDTH)
        x1, h2b, c1, e1, r2, e2 = _mix(ys, ob, ysb, ga, gs, g1, sc2, sh2, npost_mix, npre_ffn,
                                      wao, wso, wout, wq, skt, ns)
        ys = _peer(h2b, u_bf, vt_bf, c1, e1, r2, e2, x1, g2, npost_ffn, ns, 512)
        outs[4].append(k32.reshape(nb, tn_new, N_HEADS, 2 * HEAD_DIM))
        outs[5].append(v32.reshape(nb, tn_new, N_HEADS, V_DIM))
        outs[6].append(hre_s.reshape(nb, N_GROUPS, STATE_DIM))
        outs[7].append(him_s.reshape(nb, N_GROUPS, STATE_DIM))

    st = [jnp.stack(o) for o in outs]
    return (yp.reshape(bp, tp, D_MODEL), ys.reshape(nb, tn_new, D_MODEL),
            st[0], st[1], st[2], st[3], st[4], st[5], st[6], st[7])
```

```python
import functools
import math

import jax
import jax.numpy as jnp
from jax import lax
from jax.experimental import pallas as pl
from jax.experimental.pallas import tpu as pltpu

F32 = jnp.float32
BF16 = jnp.bfloat16

D_MODEL = 1024
N_HEADS = 8
HEAD_DIM = 64
V_DIM = 2 * HEAD_DIM
QK_WIDTH = N_HEADS * 2 * HEAD_DIM
ATTN_WIDTH = N_HEADS * V_DIM
ROPE_THETA = 10000.0
PAGE_SIZE = 128
SSM_WIDTH = 512
SSM_GROUP = 16
N_GROUPS = SSM_WIDTH // SSM_GROUP
STATE_DIM = 64
N_STATE = N_GROUPS * STATE_DIM
SSM_CHUNKS = 4
SSM_CHUNK_IN = SSM_WIDTH // SSM_CHUNKS
SSM_CHUNK_ST = N_STATE // SSM_CHUNKS
N_KEYS = 128
N_EXPERTS = N_KEYS * N_KEYS
PEER_HEADS = 8
PEER_KEY_DIM = 128
PEER_TOPK = 16
N_MOD = 6
RMS_EPS = 1e-6

VMEM_LIMIT_BYTES = 56 * 1024 * 1024

NEG_INF = float("-inf")
Q_SCALE = (HEAD_DIM ** -0.5) * math.log2(math.e)
NEW_ROWS = 16


def _cparams(*sem):
    return pltpu.CompilerParams(dimension_semantics=sem, vmem_limit_bytes=VMEM_LIMIT_BYTES)


def _dot(a, b):
    return jnp.dot(a, b, preferred_element_type=F32)


def _dot_nt(a, b):
    return lax.dot_general(a, b, (((1,), (1,)), ((), ())), preferred_element_type=F32)


def _rms(x, w):
    return x * lax.rsqrt(jnp.mean(x * x, axis=-1, keepdims=True) + RMS_EPS) * w


def _gelu(x):
    return 0.5 * x * (1.0 + jnp.tanh(0.7978845608028654 * (x + 0.044715 * (x * x * x))))


def _sigmoid(x):
    return 1.0 / (1.0 + jnp.exp(-x))


def _mod_spec(rows_per_mod, tm):
    if rows_per_mod == 1:
        return pl.BlockSpec((1, D_MODEL), lambda i: (0, 0))
    return pl.BlockSpec((tm, D_MODEL), lambda i: (i, 0))


def _ada_kernel(c_ref, w_ref, b_ref, o_ref):
    c = c_ref[...]
    s = c * _sigmoid(c)
    o_ref[...] = _dot(s.astype(BF16), w_ref[...].astype(BF16)) + b_ref[...]


def _ada(c_all, w_ada, b_ada):
    rows = c_all.shape[0]
    tn = 1536
    return pl.pallas_call(
        _ada_kernel,
        grid=(N_MOD * D_MODEL // tn,),
        in_specs=[pl.BlockSpec((rows, D_MODEL), lambda j: (0, 0)),
                  pl.BlockSpec((D_MODEL, tn), lambda j: (0, j)),
                  pl.BlockSpec((1, tn), lambda j: (0, j))],
        out_specs=pl.BlockSpec((rows, tn), lambda j: (0, j)),
        out_shape=jax.ShapeDtypeStruct((rows, N_MOD * D_MODEL), F32),
        compiler_params=_cparams("arbitrary"),
        name="ada_mod",
    )(c_all, w_ada, b_ada.reshape(1, -1))


def _inproj_kernel(x_ref, nw_ref, sc_ref, sh_ref, cos_ref, sin_ref, w_ref,
                   q2_ref, k32_ref, kb_ref, v32_ref, vb_ref, u_ref, ga_ref, gs_ref):
    x = x_ref[...]
    h = _rms(x, nw_ref[...]) * (1.0 + sc_ref[...]) + sh_ref[...]
    hb = h.astype(BF16)
    cos = cos_ref[...]
    sin = sin_ref[...]
    lane = lax.broadcasted_iota(jnp.int32, cos.shape, 1)
    first = lane < HEAD_DIM
    W = QK_WIDTH
    for hd in range(N_HEADS):
        c0 = hd * V_DIM
        q = _dot(hb, w_ref[:, c0:c0 + V_DIM])
        qs = _dot(hb, w_ref[:, W + c0:W + c0 + V_DIM])
        qr = (q * cos + qs * sin) * Q_SCALE
        q2_ref[0, :, c0:c0 + V_DIM] = jnp.where(first, qr, 0.0).astype(BF16)
        q2_ref[1, :, c0:c0 + V_DIM] = jnp.where(first, 0.0, qr).astype(BF16)
        k = _dot(hb, w_ref[:, 2 * W + c0:2 * W + c0 + V_DIM])
        ks = _dot(hb, w_ref[:, 3 * W + c0:3 * W + c0 + V_DIM])
        kr = k * cos + ks * sin
        k32_ref[:, c0:c0 + V_DIM] = kr
        kb_ref[:, c0:c0 + V_DIM] = kr.astype(BF16)
    o = 4 * W
    v = _dot(hb, w_ref[:, o:o + ATTN_WIDTH])
    v32_ref[...] = v
    vb_ref[...] = v.astype(BF16)
    o += ATTN_WIDTH
    u_ref[...] = _dot(hb, w_ref[:, o:o + SSM_WIDTH])
    o += SSM_WIDTH
    ga_ref[...] = _dot(hb, w_ref[:, o:o + D_MODEL])
    o += D_MODEL
    gs_ref[...] = _dot(hb, w_ref[:, o:o + D_MODEL])


def _inproj(x, nw, sc, sh, cosf, sinf, w_ext, tm):
    n = x.shape[0]
    wcols = w_ext.shape[1]
    row = lambda i: (i, 0)
    full = lambda i: (0, 0)
    outs = (
        jax.ShapeDtypeStruct((2, n, QK_WIDTH), BF16),
        jax.ShapeDtypeStruct((n, QK_WIDTH), F32),
        jax.ShapeDtypeStruct((n, QK_WIDTH), BF16),
        jax.ShapeDtypeStruct((n, ATTN_WIDTH), F32),
        jax.ShapeDtypeStruct((n, ATTN_WIDTH), BF16),
        jax.ShapeDtypeStruct((n, SSM_WIDTH), F32),
        jax.ShapeDtypeStruct((n, D_MODEL), F32),
        jax.ShapeDtypeStruct((n, D_MODEL), F32),
    )
    out_specs = (
        pl.BlockSpec((2, tm, QK_WIDTH), lambda i: (0, i, 0)),
        pl.BlockSpec((tm, QK_WIDTH), row),
        pl.BlockSpec((tm, QK_WIDTH), row),
        pl.BlockSpec((tm, ATTN_WIDTH), row),
        pl.BlockSpec((tm, ATTN_WIDTH), row),
        pl.BlockSpec((tm, SSM_WIDTH), row),
        pl.BlockSpec((tm, D_MODEL), row),
        pl.BlockSpec((tm, D_MODEL), row),
    )
    return pl.pallas_call(
        _inproj_kernel,
        grid=(n // tm,),
        in_specs=[pl.BlockSpec((tm, D_MODEL), row),
                  pl.BlockSpec((1, D_MODEL), full),
                  _mod_spec(sc.shape[0], tm),
                  _mod_spec(sh.shape[0], tm),
                  pl.BlockSpec((tm, V_DIM), row),
                  pl.BlockSpec((tm, V_DIM), row),
                  pl.BlockSpec((D_MODEL, wcols), full)],
        out_specs=out_specs,
        out_shape=outs,
        compiler_params=_cparams("arbitrary"),
        name="in_proj",
    )(x, nw, sc, sh, cosf, sinf, w_ext)


def _lambda_full(lq1_ref, lk1_ref, lq2_ref, lk2_ref, lam_init):
    s1 = jnp.sum(lq1_ref[...] * lk1_ref[...], axis=-1, keepdims=True)
    s2 = jnp.sum(lq2_ref[...] * lk2_ref[...], axis=-1, keepdims=True)
    return jnp.exp(s1) - jnp.exp(s2) + lam_init


def _subln(o, w, lam_init):
    return _rms(o, w) * (1.0 - lam_init)


def _pattn_kernel(q_ref, k_ref, v_ref, lq1_ref, lk1_ref, lq2_ref, lk2_ref, sub_ref, o_ref,
                  s_ref, m_ref, acc_ref, *, tq, lam_init):
    i = pl.program_id(1)
    m_ref[...] = jnp.full(m_ref.shape, NEG_INF, F32)
    acc_ref[...] = jnp.zeros(acc_ref.shape, F32)
    ones = jnp.ones((tq, V_DIM), BF16)

    def produce(comp, chunk):
        start = pl.multiple_of(chunk * tq, tq)
        s_ref[comp] = _dot_nt(q_ref[comp], k_ref[pl.ds(start, tq), :])

    def consume(comp, chunk, masked):
        start = pl.multiple_of(chunk * tq, tq)
        v1 = jnp.concatenate([v_ref[pl.ds(start, tq), :], ones], axis=1)
        s = s_ref[comp]
        if masked:
            r = lax.broadcasted_iota(jnp.int32, (tq, tq), 0)
            c = lax.broadcasted_iota(jnp.int32, (tq, tq), 1)
            s = jnp.where(c <= r, s, NEG_INF)
        m_old = m_ref[comp]
        m_new = jnp.maximum(m_old, jnp.max(s, axis=-1, keepdims=True))
        p = jnp.exp2(s - jnp.tile(m_new, (1, tq // V_DIM)))
        alpha = jnp.exp2(m_old - m_new)
        acc_ref[comp] = jnp.tile(alpha, (1, 2)) * acc_ref[comp] + _dot(p.astype(BF16), v1)
        m_ref[comp] = m_new

    def pair(chunk, last_next):
        produce(1, chunk)
        consume(0, chunk, False)
        produce(0, last_next)
        consume(1, chunk, False)

    def body(t, carry):
        pair(2 * t, 2 * t + 1)
        pair(2 * t + 1, 2 * t + 2)
        return carry

    produce(0, 0)
    lax.fori_loop(0, i // 2, body, 0)

    @pl.when(i % 2 == 1)
    def _():
        pair(i - 1, i)

    produce(1, i)
    consume(0, i, True)
    consume(1, i, True)

    lam = _lambda_full(lq1_ref, lk1_ref, lq2_ref, lk2_ref, lam_init)
    o1 = acc_ref[0, :, :V_DIM] / acc_ref[0, :, V_DIM:]
    o2 = acc_ref[1, :, :V_DIM] / acc_ref[1, :, V_DIM:]
    o_ref[...] = _subln(o1 - lam * o2, sub_ref[...], lam_init).astype(BF16)


def _prompt_attention(q2, kb, vb, lq1, lk1, lq2, lk2, subln, lam_init, tq):
    t = kb.shape[0]
    vec = lambda h, i: (0, 0)
    kern = functools.partial(_pattn_kernel, tq=tq, lam_init=lam_init)
    return pl.pallas_call(
        kern,
        grid=(N_HEADS, t // tq),
        in_specs=[pl.BlockSpec((2, tq, V_DIM), lambda h, i: (0, i, h)),
                  pl.BlockSpec((t, V_DIM), lambda h, i: (0, h)),
                  pl.BlockSpec((t, V_DIM), lambda h, i: (0, h)),
                  pl.BlockSpec((1, HEAD_DIM), vec),
                  pl.BlockSpec((1, HEAD_DIM), vec),
                  pl.BlockSpec((1, HEAD_DIM), vec),
                  pl.BlockSpec((1, HEAD_DIM), vec),
                  pl.BlockSpec((1, V_DIM), vec)],
        out_specs=pl.BlockSpec((tq, V_DIM), lambda h, i: (i, h)),
        out_shape=jax.ShapeDtypeStruct((t, ATTN_WIDTH), BF16),
        scratch_shapes=[pltpu.VMEM((2, tq, tq), F32),
                        pltpu.VMEM((2, tq, V_DIM), F32),
                        pltpu.VMEM((2, tq, 2 * V_DIM), F32)],
        compiler_params=_cparams("arbitrary", "arbitrary"),
        name="prompt_attn",
    )(q2, kb, vb, lq1, lk1, lq2, lk2, subln)


def _sattn_kernel(pt_ref, q_ref, kn_ref, vn_ref, lq1_ref, lk1_ref, lq2_ref, lk2_ref, sub_ref, *rest,
                  pages_per_step, n_new, lam_init):
    g_pages = pages_per_step
    k_refs = rest[:g_pages]
    v_refs = rest[g_pages:2 * g_pages]
    o_ref = rest[2 * g_pages]
    m_ref, l_ref, acc_ref, bias_ref = rest[2 * g_pages + 1:]
    b = pl.program_id(0)
    g = pl.program_id(1)
    q = q_ref[0]
    rows_per_head = 2 * n_new

    @pl.when((b == 0) & (g == 0))
    def _():
        r = lax.broadcasted_iota(jnp.int32, bias_ref.shape, 0) // rows_per_head
        c = lax.broadcasted_iota(jnp.int32, bias_ref.shape, 1) % N_HEADS
        bias_ref[...] = jnp.where(r == c, 0.0, NEG_INF)

    @pl.when(g == 0)
    def _():
        m_ref[...] = jnp.full(m_ref.shape, NEG_INF, F32)
        l_ref[...] = jnp.zeros(l_ref.shape, F32)
        acc_ref[...] = jnp.zeros(acc_ref.shape, F32)

    def update(s, v):
        m_old = m_ref[...]
        m_new = jnp.maximum(m_old, jnp.max(s, axis=-1, keepdims=True))
        p = jnp.exp2(s - m_new)
        alpha = jnp.exp2(m_old - m_new)
        l_ref[...] = alpha * l_ref[...] + jnp.sum(p, axis=-1, keepdims=True)
        acc_ref[...] = alpha * acc_ref[...] + _dot(p.astype(BF16), v)
        m_ref[...] = m_new

    k = jnp.concatenate([r[0].astype(BF16) for r in k_refs], axis=0)
    v = jnp.concatenate([r[0].astype(BF16) for r in v_refs], axis=0)
    update(_dot_nt(q, k) + bias_ref[...], v)

    @pl.when(g == pl.num_programs(1) - 1)
    def _():
        kn = kn_ref[0].astype(BF16)
        vn = vn_ref[0].astype(BF16)
        s = _dot_nt(q, kn)
        row = lax.broadcasted_iota(jnp.int32, s.shape, 0)
        col = lax.broadcasted_iota(jnp.int32, s.shape, 1)
        keep = (row // rows_per_head == col % N_HEADS) & (col // N_HEADS <= row % n_new)
        update(jnp.where(keep, s, NEG_INF), vn)
        lam = _lambda_full(lq1_ref, lk1_ref, lq2_ref, lk2_ref, lam_init)
        o = acc_ref[...] / l_ref[...]
        d = o - lam * pltpu.roll(o, o.shape[0] - n_new, 0)
        o_ref[0] = _subln(d, sub_ref[...], lam_init)


def _sample_attention(page_table, q_rows, k_new, v_new, cache_k, cache_v, lq1, lk1, lq2, lk2, subln,
                      lam_init, n_new, pages_per_step):
    nb, n_pages = page_table.shape
    gp = pages_per_step
    rows = q_rows.shape[1]
    page_rows = PAGE_SIZE * N_HEADS
    new_rows = k_new.shape[1]
    vec = lambda b, g, pt: (0, 0)

    def page_spec(r):
        return pl.BlockSpec((1, page_rows, V_DIM),
                            lambda b, g, pt: (pt[b * n_pages + g * gp + r], 0, 0))

    kern = functools.partial(_sattn_kernel, pages_per_step=gp, n_new=n_new, lam_init=lam_init)
    grid_spec = pltpu.PrefetchScalarGridSpec(
        num_scalar_prefetch=1,
        grid=(nb, n_pages // gp),
        in_specs=[pl.BlockSpec((1, rows, V_DIM), lambda b, g, pt: (b, 0, 0)),
                  pl.BlockSpec((1, new_rows, V_DIM), lambda b, g, pt: (b, 0, 0)),
                  pl.BlockSpec((1, new_rows, V_DIM), lambda b, g, pt: (b, 0, 0)),
                  pl.BlockSpec((1, HEAD_DIM), vec),
                  pl.BlockSpec((1, HEAD_DIM), vec),
                  pl.BlockSpec((1, HEAD_DIM), vec),
                  pl.BlockSpec((1, HEAD_DIM), vec),
                  pl.BlockSpec((1, V_DIM), vec)]
                 + [page_spec(r) for r in range(gp)] + [page_spec(r) for r in range(gp)],
        out_specs=pl.BlockSpec((1, rows, V_DIM), lambda b, g, pt: (b, 0, 0)),
        scratch_shapes=[pltpu.VMEM((rows, 1), F32),
                        pltpu.VMEM((rows, 1), F32),
                        pltpu.VMEM((rows, V_DIM), F32),
                        pltpu.VMEM((rows, gp * page_rows), F32)],
    )
    return pl.pallas_call(
        kern,
        grid_spec=grid_spec,
        out_shape=jax.ShapeDtypeStruct((nb, rows, V_DIM), F32),
        compiler_params=_cparams("arbitrary", "arbitrary"),
        name="sample_attn",
    )(page_table.reshape(-1), q_rows, k_new, v_new, lq1, lk1, lq2, lk2, subln,
      *([cache_k] * gp), *([cache_v] * gp))


def _ssm_disc_kernel(lr_ref, li_ref, ls_ref, are_ref, aim_ref, fre_ref, fim_ref, *, n_pow):
    lr = lr_ref[...]
    li = li_ref[...]
    dt = jnp.exp(ls_ref[...])
    mag = jnp.exp(lr * dt)
    a_re = mag * jnp.cos(li * dt)
    a_im = mag * jnp.sin(li * dt)
    den = lr * lr + li * li
    nr = a_re - 1.0
    ni = a_im
    fre_ref[...] = (nr * lr + ni * li) / den
    fim_ref[...] = (ni * lr - nr * li) / den
    pr, pi = a_re, a_im
    for kk in range(n_pow):
        are_ref[kk] = pr
        aim_ref[kk] = pi
        pr, pi = pr * pr - pi * pi, 2.0 * pr * pi


def _ssm_bb_kernel(fre_ref, fim_ref, bre_ref, bim_ref, ore_ref, oim_ref):
    fr = fre_ref[...]
    fi = fim_ref[...]
    br = bre_ref[...]
    bi = bim_ref[...]
    ore_ref[...] = fr * br - fi * bi
    oim_ref[...] = fr * bi + fi * br


def _ssm_prepare(lam_re, lam_im, log_step, b_re, b_im, c_re, c_im, n_pow):
    gshape = jax.ShapeDtypeStruct((N_GROUPS, STATE_DIM), F32)
    pshape = jax.ShapeDtypeStruct((n_pow, N_GROUPS, STATE_DIM), F32)
    a_re, a_im, f_re, f_im = pl.pallas_call(
        functools.partial(_ssm_disc_kernel, n_pow=n_pow),
        out_shape=(pshape, pshape, gshape, gshape),
        name="ssm_discretise",
    )(lam_re, lam_im, log_step.reshape(N_GROUPS, 1))
    bshape = jax.ShapeDtypeStruct((N_STATE, SSM_GROUP), F32)
    bb_re, bb_im = pl.pallas_call(
        _ssm_bb_kernel, out_shape=(bshape, bshape), name="ssm_input_matrix",
    )(f_re.reshape(N_STATE, 1), f_im.reshape(N_STATE, 1),
      b_re.reshape(N_STATE, SSM_GROUP), b_im.reshape(N_STATE, SSM_GROUP))
    gc = N_GROUPS // SSM_CHUNKS
    eye = jnp.eye(gc, dtype=F32)

    def in_mat(bb):
        b4 = bb.reshape(SSM_CHUNKS, gc, STATE_DIM, SSM_GROUP)
        m = jnp.einsum('cgpi,gh->cgihp', b4, eye)
        return m.reshape(SSM_CHUNKS, gc * SSM_GROUP, gc * STATE_DIM)

    def out_mat(cc):
        c4 = cc.reshape(SSM_CHUNKS, gc, SSM_GROUP, STATE_DIM)
        m = jnp.einsum('cgip,gh->cgphi', c4, eye)
        return m.reshape(SSM_CHUNKS, gc * STATE_DIM, gc * SSM_GROUP)

    b_mat = jnp.concatenate([in_mat(bb_re), in_mat(bb_im)], axis=-1).astype(BF16)
    c_mat = jnp.concatenate([out_mat(c_re), out_mat(-c_im)], axis=-2).astype(BF16)
    return (a_re.reshape(n_pow, N_STATE), a_im.reshape(n_pow, N_STATE), b_mat, c_mat)


def _shift_rows(x, s, row):
    return jnp.where(row >= s, pltpu.roll(x, s, 0), 0.0)


def _ssm_scan_kernel(u_ref, are_ref, aim_ref, b_ref, c_ref, d_ref, wg_ref,
                     y_ref, hre_ref, him_ref, cre_ref, cim_ref, *, tm, n_pow):
    @pl.when(pl.program_id(0) == 0)
    def _():
        cre_ref[...] = jnp.zeros(cre_ref.shape, F32)
        cim_ref[...] = jnp.zeros(cim_ref.shape, F32)

    u = u_ref[...]
    ub = u.astype(BF16)
    row = lax.broadcasted_iota(jnp.int32, (tm, SSM_CHUNK_ST), 0)
    ys = []
    for c in range(SSM_CHUNKS):
        lanes = slice(c * SSM_CHUNK_ST, (c + 1) * SSM_CHUNK_ST)
        bu = _dot(ub[:, c * SSM_CHUNK_IN:(c + 1) * SSM_CHUNK_IN], b_ref[c])
        xr = bu[:, :SSM_CHUNK_ST]
        xi = bu[:, SSM_CHUNK_ST:]
        a_r = are_ref[0:1, lanes]
        a_i = aim_ref[0:1, lanes]
        h_r = cre_ref[0:1, lanes]
        h_i = cim_ref[0:1, lanes]
        first = row == 0
        xr = xr + jnp.where(first, a_r * h_r - a_i * h_i, 0.0)
        xi = xi + jnp.where(first, a_r * h_i + a_i * h_r, 0.0)
        for kk in range(n_pow):
            s = 1 << kk
            p_r = are_ref[kk:kk + 1, lanes]
            p_i = aim_ref[kk:kk + 1, lanes]
            sr = _shift_rows(xr, s, row)
            si = _shift_rows(xi, s, row)
            xr, xi = xr + (p_r * sr - p_i * si), xi + (p_r * si + p_i * sr)
        cre_ref[0:1, lanes] = xr[tm - 1:tm]
        cim_ref[0:1, lanes] = xi[tm - 1:tm]
        hcat = jnp.concatenate([xr, xi], axis=1).astype(BF16)
        ys.append(_dot(hcat, c_ref[c]))
    y = jnp.concatenate(ys, axis=1) + d_ref[...] * u
    y = _gelu(y)
    y = y * _sigmoid(_dot(y.astype(BF16), wg_ref[...]))
    y_ref[...] = y.astype(BF16)
    hre_ref[...] = cre_ref[0:1, :]
    him_ref[...] = cim_ref[0:1, :]


def _ssm_prompt(u, a_re, a_im, b_mat, c_mat, d, wg, tm):
    t = u.shape[0]
    n_pow = a_re.shape[0]
    full2 = lambda i: (0, 0)
    full3 = lambda i: (0, 0, 0)
    kern = functools.partial(_ssm_scan_kernel, tm=tm, n_pow=n_pow)
    return pl.pallas_call(
        kern,
        grid=(t // tm,),
        in_specs=[pl.BlockSpec((tm, SSM_WIDTH), lambda i: (i, 0)),
                  pl.BlockSpec((n_pow, N_STATE), full2),
                  pl.BlockSpec((n_pow, N_STATE), full2),
                  pl.BlockSpec(b_mat.shape, full3),
                  pl.BlockSpec(c_mat.shape, full3),
                  pl.BlockSpec((1, SSM_WIDTH), full2),
                  pl.BlockSpec((SSM_WIDTH, SSM_WIDTH), full2)],
        out_specs=(pl.BlockSpec((tm, SSM_WIDTH), lambda i: (i, 0)),
                   pl.BlockSpec((1, N_STATE), full2),
                   pl.BlockSpec((1, N_STATE), full2)),
        out_shape=(jax.ShapeDtypeStruct((t, SSM_WIDTH), BF16),
                   jax.ShapeDtypeStruct((1, N_STATE), F32),
                   jax.ShapeDtypeStruct((1, N_STATE), F32)),
        scratch_shapes=[pltpu.VMEM((8, N_STATE), F32), pltpu.VMEM((8, N_STATE), F32)],
        compiler_params=_cparams("arbitrary"),
        name="ssm_prompt",
    )(u, a_re, a_im, b_mat, c_mat, d, wg)


def _ssm_step_kernel(u_ref, h0re_ref, h0im_ref, are_ref, aim_ref, b_ref, c_ref, d_ref, wg_ref,
                     y_ref, hre_ref, him_ref, *, n_steps):
    a_r = are_ref[0:1, :]
    a_i = aim_ref[0:1, :]
    h_r = h0re_ref[...]
    h_i = h0im_ref[...]
    for t in range(n_steps):
        u = u_ref[t]
        ub = u.astype(BF16)
        brs, bis = [], []
        for c in range(SSM_CHUNKS):
            bu = _dot(ub[:, c * SSM_CHUNK_IN:(c + 1) * SSM_CHUNK_IN], b_ref[c])
            brs.append(bu[:, :SSM_CHUNK_ST])
            bis.append(bu[:, SSM_CHUNK_ST:])
        bu_r = jnp.concatenate(brs, axis=1)
        bu_i = jnp.concatenate(bis, axis=1)
        h_r, h_i = a_r * h_r - a_i * h_i + bu_r, a_r * h_i + a_i * h_r + bu_i
        ys = []
        for c in range(SSM_CHUNKS):
            lanes = slice(c * SSM_CHUNK_ST, (c + 1) * SSM_CHUNK_ST)
            hcat = jnp.concatenate([h_r[:, lanes], h_i[:, lanes]], axis=1).astype(BF16)
            ys.append(_dot(hcat, c_ref[c]))
        y = jnp.concatenate(ys, axis=1) + d_ref[...] * u
        y = _gelu(y)
        y = y * _sigmoid(_dot(y.astype(BF16), wg_ref[...]))
        y_ref[t] = y.astype(BF16)
    hre_ref[...] = h_r
    him_ref[...] = h_i


def _ssm_sample(u_tb, h0_re, h0_im, a_re, a_im, b_mat, c_mat, d, wg):
    n_steps, nb, _ = u_tb.shape
    kern = functools.partial(_ssm_step_kernel, n_steps=n_steps)
    return pl.pallas_call(
        kern,
        out_shape=(jax.ShapeDtypeStruct((n_steps, nb, SSM_WIDTH), BF16),
                   jax.ShapeDtypeStruct((nb, N_STATE), F32),
                   jax.ShapeDtypeStruct((nb, N_STATE), F32)),
        compiler_params=pltpu.CompilerParams(vmem_limit_bytes=VMEM_LIMIT_BYTES),
        name="ssm_sample",
    )(u_tb, h0_re, h0_im, a_re, a_im, b_mat, c_mat, d, wg)


def _kth_largest_rows(c, k):
    cnt = jnp.zeros((1, c.shape[1]), F32)
    tau = jnp.full((1, c.shape[1]), NEG_INF, F32)
    for _ in range(k):
        m = jnp.max(c, axis=0, keepdims=True)
        eq = c == m
        tau = jnp.where(cnt < k, m, tau)
        cnt = cnt + jnp.sum(eq.astype(F32), axis=0, keepdims=True)
        c = jnp.where(eq, NEG_INF, c)
    return tau


def _top_rows(a, k, out_ref):
    for r in range(k):
        m = jnp.max(a, axis=0, keepdims=True)
        out_ref[r:r + 1, :] = m
        a = jnp.where(a == m, NEG_INF, a)


def _mix_kernel(x_ref, o_ref, ys_ref, ga_ref, gs_ref, g1_ref, sc2_ref, sh2_ref, npost_ref, npre_ref,
                wao_ref, wso_ref, wout_ref, wq_ref, skt_ref,
                x1_ref, h2_ref, c1_ref, e1_ref, r2_ref, e2_ref, v1_ref, v2_ref):
    y_attn = _dot(o_ref[...], wao_ref[...])
    y_ssm = _dot(ys_ref[...], wso_ref[...])
    merged = _sigmoid(ga_ref[...]) * y_attn + _sigmoid(gs_ref[...]) * y_ssm
    mix = _dot(merged.astype(BF16), wout_ref[...])
    x1 = x_ref[...] + g1_ref[...] * _rms(mix, npost_ref[...])
    x1_ref[...] = x1
    h2 = _rms(x1, npre_ref[...]) * (1.0 + sc2_ref[...]) + sh2_ref[...]
    h2b = h2.astype(BF16)
    h2_ref[...] = h2b
    qp = _dot(h2b, wq_ref[...]).astype(BF16)
    s_all = _dot_nt(skt_ref[...], qp)
    for hd in range(PEER_HEADS):
        r0 = hd * 2 * N_KEYS
        k0 = hd * N_KEYS
        s1 = s_all[r0:r0 + N_KEYS]
        s2 = s_all[r0 + N_KEYS:r0 + 2 * N_KEYS]
        _top_rows(s1, PEER_TOPK, v1_ref)
        _top_rows(s2, PEER_TOPK, v2_ref)
        v1a = v1_ref[...]
        cands = [v1a + v2_ref[b:b + 1, :] for b in range(PEER_TOPK)]
        cand = jnp.concatenate(cands, axis=0)
        tau = _kth_largest_rows(cand, PEER_TOPK)
        m1 = v1_ref[0:1, :]
        m2 = v2_ref[0:1, :]
        z = jnp.sum(jnp.where(cand >= tau, jnp.exp(cand - (m1 + m2)), 0.0), axis=0, keepdims=True)
        cnt = jnp.zeros(v1a.shape, F32)
        for cb in cands:
            cnt = cnt + (cb >= tau).astype(F32)
        c1 = jnp.zeros(s1.shape, F32)
        r2 = jnp.full(s2.shape, float(N_KEYS - 1), F32)
        for r in range(PEER_TOPK):
            c1 = jnp.where(s1 == v1_ref[r:r + 1, :], cnt[r:r + 1], c1)
            r2 = jnp.where(s2 == v2_ref[r:r + 1, :], float(r), r2)
        c1_ref[k0:k0 + N_KEYS, :] = c1
        e1_ref[k0:k0 + N_KEYS, :] = jnp.exp(s1 - m1)
        r2_ref[k0:k0 + N_KEYS, :] = r2.astype(BF16)
        e2_ref[k0:k0 + N_KEYS, :] = (jnp.exp(s2 - m2) / z).astype(BF16)


def _mix(x, ob, ysb, ga, gs, g1, sc2, sh2, npost, npre, wao, wso, wout, wq, skt, tm):
    n = x.shape[0]
    row = lambda i: (i, 0)
    col = lambda i: (0, i)
    full = lambda i: (0, 0)
    nrow = 2 * N_KEYS * PEER_HEADS
    krow = N_KEYS * PEER_HEADS
    return pl.pallas_call(
        _mix_kernel,
        grid=(n // tm,),
        in_specs=[pl.BlockSpec((tm, D_MODEL), row),
                  pl.BlockSpec((tm, ATTN_WIDTH), row),
                  pl.BlockSpec((tm, SSM_WIDTH), row),
                  pl.BlockSpec((tm, D_MODEL), row),
                  pl.BlockSpec((tm, D_MODEL), row),
                  _mod_spec(g1.shape[0], tm),
                  _mod_spec(sc2.shape[0], tm),
                  _mod_spec(sh2.shape[0], tm),
                  pl.BlockSpec((1, D_MODEL), full),
                  pl.BlockSpec((1, D_MODEL), full),
                  pl.BlockSpec((ATTN_WIDTH, D_MODEL), full),
                  pl.BlockSpec((SSM_WIDTH, D_MODEL), full),
                  pl.BlockSpec((D_MODEL, D_MODEL), full),
                  pl.BlockSpec((D_MODEL, PEER_HEADS * PEER_KEY_DIM), full),
                  pl.BlockSpec((nrow, PEER_HEADS * PEER_KEY_DIM), full)],
        out_specs=(pl.BlockSpec((tm, D_MODEL), row),
                   pl.BlockSpec((tm, D_MODEL), row),
                   pl.BlockSpec((krow, tm), col),
                   pl.BlockSpec((krow, tm), col),
                   pl.BlockSpec((krow, tm), col),
                   pl.BlockSpec((krow, tm), col)),
        out_shape=(jax.ShapeDtypeStruct((n, D_MODEL), F32),
                   jax.ShapeDtypeStruct((n, D_MODEL), BF16),
                   jax.ShapeDtypeStruct((krow, n), F32),
                   jax.ShapeDtypeStruct((krow, n), F32),
                   jax.ShapeDtypeStruct((krow, n), BF16),
                   jax.ShapeDtypeStruct((krow, n), BF16)),
        scratch_shapes=[pltpu.VMEM((PEER_TOPK, tm), F32), pltpu.VMEM((PEER_TOPK, tm), F32)],
        compiler_params=_cparams("arbitrary"),
        name="mix_route",
    )(x, ob, ysb, ga, gs, g1, sc2, sh2, npost, npre, wao, wso, wout, wq, skt)


BF16_ROWS = 16


def _tile_bf16(row):
    return jnp.broadcast_to(row, (BF16_ROWS, row.shape[1])).astype(BF16)


def _peer_kernel(h2_ref, u0_ref, ua_ref, ub_ref, vt_ref, c1_ref, e1_ref, r2_ref, e2_ref,
                 x1_ref, g2_ref, nw_ref, y_ref, acc_ref, act_ref, *, ec):
    g = pl.program_id(1)
    h2 = h2_ref[...]
    blocks = ec // N_KEYS

    @pl.when(g == 0)
    def _():
        acc_ref[...] = jnp.zeros(acc_ref.shape, F32)
        act_ref[0] = _dot_nt(u0_ref[...], h2)

    def produce(buf, u_ref):
        act_ref[buf] = _dot_nt(u_ref[...], h2)

    def consume(buf, chunk, col0):
        ws = []
        for ii in range(blocks):
            key1 = chunk * blocks + ii
            a = _gelu(act_ref[buf, ii * N_KEYS:(ii + 1) * N_KEYS, :]).astype(BF16)
            rows = [(_tile_bf16(c1_ref[pl.ds(hd * N_KEYS + key1, 1), :]),
                     _tile_bf16(e1_ref[pl.ds(hd * N_KEYS + key1, 1), :])) for hd in range(PEER_HEADS)]
            for t in range(N_KEYS // BF16_ROWS):
                r0 = t * BF16_ROWS
                gate = None
                for hd in range(PEER_HEADS):
                    k0 = hd * N_KEYS + r0
                    cnt, e1 = rows[hd]
                    r2 = r2_ref[k0:k0 + BF16_ROWS, :]
                    e2 = e2_ref[k0:k0 + BF16_ROWS, :]
                    term = jnp.where(r2 < cnt, e2, jnp.zeros_like(e2)) * e1
                    gate = term if gate is None else gate + term
                ws.append(gate * a[r0:r0 + BF16_ROWS])
        w = jnp.concatenate(ws, axis=0)
        acc_ref[...] += _dot(vt_ref[:, col0:col0 + ec], w)

    produce(1, ua_ref)
    consume(0, 2 * g, 0)
    produce(0, ub_ref)
    consume(1, 2 * g + 1, ec)

    @pl.when(g == pl.num_programs(1) - 1)
    def _():
        f = acc_ref[...].T
        y_ref[...] = x1_ref[...] + g2_ref[...] * _rms(f, nw_ref[...])


def _peer(h2b, u_bf, vt_bf, c1, e1, r2, e2, x1, g2, nw, tn, ec):
    n = h2b.shape[0]
    krow = c1.shape[0]
    n_chunks = N_EXPERTS // ec
    kern = functools.partial(_peer_kernel, ec=ec)
    g2_spec = (pl.BlockSpec((1, D_MODEL), lambda t, c: (0, 0)) if g2.shape[0] == 1
               else pl.BlockSpec((tn, D_MODEL), lambda t, c: (t, 0)))
    return pl.pallas_call(
        kern,
        grid=(n // tn, n_chunks // 2),
        in_specs=[pl.BlockSpec((tn, D_MODEL), lambda t, c: (t, 0)),
                  pl.BlockSpec((ec, D_MODEL), lambda t, c: (0, 0)),
                  pl.BlockSpec((ec, D_MODEL), lambda t, c: (2 * c + 1, 0)),
                  pl.BlockSpec((ec, D_MODEL), lambda t, c: (jnp.minimum(2 * c + 2, n_chunks - 1), 0)),
                  pl.BlockSpec((D_MODEL, 2 * ec), lambda t, c: (0, c)),
                  pl.BlockSpec((krow, tn), lambda t, c: (0, t)),
                  pl.BlockSpec((krow, tn), lambda t, c: (0, t)),
                  pl.BlockSpec((krow, tn), lambda t, c: (0, t)),
                  pl.BlockSpec((krow, tn), lambda t, c: (0, t)),
                  pl.BlockSpec((tn, D_MODEL), lambda t, c: (t, 0)),
                  g2_spec,
                  pl.BlockSpec((1, D_MODEL), lambda t, c: (0, 0))],
        out_specs=pl.BlockSpec((tn, D_MODEL), lambda t, c: (t, 0)),
        out_shape=jax.ShapeDtypeStruct((n, D_MODEL), F32),
        scratch_shapes=[pltpu.VMEM((D_MODEL, tn), F32), pltpu.VMEM((2, ec, tn), F32)],
        compiler_params=_cparams("arbitrary", "arbitrary"),
        name="peer_experts",
    )(h2b, u_bf, u_bf, u_bf, vt_bf, c1, e1, r2, e2, x1, g2, nw)


def _rope_tables(pos):
    half = HEAD_DIM // 2
    inv = ROPE_THETA ** (-jnp.arange(half, dtype=F32) / half)
    ang = pos.astype(F32)[:, None] * inv[None, :]
    cos = jnp.cos(ang)
    sin = jnp.sin(ang)
    cosf = jnp.concatenate([cos, cos, cos, cos], axis=1)
    sinf = jnp.concatenate([-sin, sin, -sin, sin], axis=1)
    return cosf, sinf


def _swap_halves(w):
    half = HEAD_DIM // 2
    k = w.shape[0]
    return w.reshape(k, -1, 2, half)[:, :, ::-1, :].reshape(k, -1)


def _tile(n, pref):
    t = min(n, pref)
    assert n % t == 0, (n, pref)
    return t


def kernel(x_prompt, x_sample, cache_k, cache_v, state_ssm_re, state_ssm_im, page_table, c_prompt, c_sample, w_ada, b_ada, norm_pre_mix, norm_post_mix, norm_pre_ffn, norm_post_ffn, w_in, lambda_q1, lambda_k1, lambda_q2, lambda_k2, subln, w_attn_out, ssm_lambda_re, ssm_lambda_im, ssm_log_step, ssm_b_re, ssm_b_im, ssm_c_re, ssm_c_im, ssm_d, w_glu, w_ssm_out, w_out, peer_w_query, peer_sub_keys, peer_u, peer_v):
    depth = w_ada.shape[0]
    bp, tp, _ = x_prompt.shape
    nb, tn_new, _ = x_sample.shape
    assert bp == 1 and 2 * tn_new == 8
    n_pages = page_table.shape[1]
    past_len = n_pages * PAGE_SIZE
    n_phys = cache_k.shape[1]
    ns = nb * tn_new

    pos_p = jnp.arange(tp, dtype=jnp.int32)
    pos_s = jnp.tile(past_len + jnp.arange(tn_new, dtype=jnp.int32), nb)
    cos_p, sin_p = _rope_tables(pos_p)
    cos_s, sin_s = _rope_tables(pos_s)

    tm_p = _tile(tp, 256)
    tq = _tile(tp, 512)
    tscan = _tile(tp, 128)
    n_pow = int(math.log2(tscan))
    assert 1 << n_pow == tscan
    pages_per_step = 8 if n_pages % 8 == 0 else 1

    yp = x_prompt.reshape(tp, D_MODEL)
    ys = x_sample.reshape(ns, D_MODEL)
    c_all = jnp.concatenate([c_prompt, c_sample], axis=0)
    pad = (-c_all.shape[0]) % 8
    c_all = jnp.pad(c_all, ((0, pad), (0, 0)))

    outs = [[] for _ in range(8)]
    for l in range(depth):
        lam_init = 0.8 - 0.6 * math.exp(-0.3 * l)
        mod = _ada(c_all, w_ada[l], b_ada[l])
        mod_p = [mod[0:1, j * D_MODEL:(j + 1) * D_MODEL] for j in range(N_MOD)]
        mod_s = [jnp.repeat(mod[1:1 + nb, j * D_MODEL:(j + 1) * D_MODEL], tn_new, axis=0)
                 for j in range(N_MOD)]

        w = w_in[l]
        wq_, wk_ = w[:, :QK_WIDTH], w[:, QK_WIDTH:2 * QK_WIDTH]
        w_ext = jnp.concatenate([wq_, _swap_halves(wq_), wk_, _swap_halves(wk_), w[:, 2 * QK_WIDTH:]],
                                axis=1).astype(BF16)
        a_re, a_im, b_mat, c_mat = _ssm_prepare(ssm_lambda_re[l], ssm_lambda_im[l], ssm_log_step[l],
                                                ssm_b_re[l], ssm_b_im[l], ssm_c_re[l], ssm_c_im[l],
                                                max(n_pow, 1))
        d_row = ssm_d[l].reshape(1, SSM_WIDTH)
        wg = w_glu[l].astype(BF16)
        wao = w_attn_out[l].astype(BF16)
        wso = w_ssm_out[l].astype(BF16)
        wout = w_out[l].astype(BF16)
        wq = peer_w_query[l].astype(BF16)
        sk = peer_sub_keys[l]
        eye = jnp.eye(PEER_HEADS * 2, dtype=F32)
        skt = jnp.einsum('bkd,bc->bkcd', sk.reshape(PEER_HEADS * 2, N_KEYS, PEER_KEY_DIM // 2), eye)
        skt = skt.reshape(PEER_HEADS * 2 * N_KEYS, PEER_HEADS * PEER_KEY_DIM).astype(BF16)
        u_bf = peer_u[l].astype(BF16)
        vt_bf = peer_v[l].T.astype(BF16)
        lq1 = lambda_q1[l].reshape(1, HEAD_DIM)
        lk1 = lambda_k1[l].reshape(1, HEAD_DIM)
        lq2 = lambda_q2[l].reshape(1, HEAD_DIM)
        lk2 = lambda_k2[l].reshape(1, HEAD_DIM)
        sub = subln[l].reshape(1, V_DIM)
        npre_mix = norm_pre_mix[l].reshape(1, D_MODEL)
        npost_mix = norm_post_mix[l].reshape(1, D_MODEL)
        npre_ffn = norm_pre_ffn[l].reshape(1, D_MODEL)
        npost_ffn = norm_post_ffn[l].reshape(1, D_MODEL)

        sh1, sc1, g1, sh2, sc2, g2 = mod_p
        q2, k32, kb, v32, vb, u, ga, gs = _inproj(yp, npre_mix, sc1, sh1, cos_p, sin_p, w_ext, tm_p)
        ob = _prompt_attention(q2, kb, vb, lq1, lk1, lq2, lk2, sub, lam_init, tq)
        ysb, hre, him = _ssm_prompt(u, a_re, a_im, b_mat, c_mat, d_row, wg, tscan)
        x1, h2b, c1, e1, r2, e2 = _mix(yp, ob, ysb, ga, gs, g1, sc2, sh2, npost_mix, npre_ffn,
                                      wao, wso, wout, wq, skt, tm_p)
        yp = _peer(h2b, u_bf, vt_bf, c1, e1, r2, e2, x1, g2, npost_ffn, _tile(tp, 512), 512)
        outs[0].append(k32.reshape(1, tp, N_HEADS, 2 * HEAD_DIM))
        outs[1].append(v32.reshape(1, tp, N_HEADS, V_DIM))
        outs[2].append(hre.reshape(1, N_GROUPS, STATE_DIM))
        outs[3].append(him.reshape(1, N_GROUPS, STATE_DIM))

        sh1, sc1, g1, sh2, sc2, g2 = mod_s
        q2, k32, _, v32, _, u, ga, gs = _inproj(ys, npre_mix, sc1, sh1, cos_s, sin_s, w_ext, ns)
        q_rows = q2.reshape(2, nb, tn_new, N_HEADS, V_DIM).transpose(1, 3, 0, 2, 4)
        q_rows = q_rows.reshape(nb, N_HEADS * 2 * tn_new, V_DIM)
        k_new = jnp.pad(k32.reshape(nb, tn_new * N_HEADS, V_DIM),
                        ((0, 0), (0, (NEW_ROWS - tn_new) * N_HEADS), (0, 0)))
        v_new = jnp.pad(v32.reshape(nb, tn_new * N_HEADS, V_DIM),
                        ((0, 0), (0, (NEW_ROWS - tn_new) * N_HEADS), (0, 0)))
        ck = cache_k.reshape(depth * n_phys, PAGE_SIZE * N_HEADS, V_DIM)
        cv = cache_v.reshape(depth * n_phys, PAGE_SIZE * N_HEADS, V_DIM)
        o_s = _sample_attention(page_table + l * n_phys, q_rows, k_new, v_new, ck, cv,
                                lq1, lk1, lq2, lk2, sub, lam_init, tn_new, pages_per_step)
        ob = o_s.reshape(nb, N_HEADS, 2, tn_new, V_DIM)[:, :, 0].transpose(0, 2, 1, 3)
        ob = ob.reshape(ns, ATTN_WIDTH).astype(BF16)
        u_tb = u.reshape(nb, tn_new, SSM_WIDTH).transpose(1, 0, 2)
        y_tb, hre_s, him_s = _ssm_sample(u_tb, state_ssm_re[l].reshape(nb, N_STATE),
                                         state_ssm_im[l].reshape(nb, N_STATE),
                                         a_re, a_im, b_mat, c_mat, d_row, wg)
        ysb = y_tb.transpose(1, 0, 2).reshape(ns, SSM_WIDTH)
        x1, h2b, c1, e1, r2, e2 = _mix(ys, ob, ysb, ga, gs, g1, sc2, sh2, npost_mix, npre_ffn,
                                      wao, wso, wout, wq, skt, ns)
        ys = _peer(h2b, u_bf, vt_bf, c1, e1, r2, e2, x1, g2, npost_ffn, ns, 512)
        outs[4].append(k32.reshape(nb, tn_new, N_HEADS, 2 * HEAD_DIM))
        outs[5].append(v32.reshape(nb, tn_new, N_HEADS, V_DIM))
        outs[6].append(hre_s.reshape(nb, N_GROUPS, STATE_DIM))
        outs[7].append(him_s.reshape(nb, N_GROUPS, STATE_DIM))

    st = [jnp.stack(o) for o in outs]
    return (yp.reshape(bp, tp, D_MODEL), ys.reshape(nb, tn_new, D_MODEL),
            st[0], st[1], st[2], st[3], st[4], st[5], st[6], st[7])
```

```python
import functools
import math

import jax
import jax.numpy as jnp
from jax import lax
from jax.experimental import pallas as pl
from jax.experimental.pallas import tpu as pltpu

F32 = jnp.float32
BF16 = jnp.bfloat16

D_MODEL = 1024
N_HEADS = 8
HEAD_DIM = 64
V_DIM = 2 * HEAD_DIM
QK_WIDTH = N_HEADS * 2 * HEAD_DIM
ATTN_WIDTH = N_HEADS * V_DIM
ROPE_THETA = 10000.0
PAGE_SIZE = 128
SSM_WIDTH = 512
SSM_GROUP = 16
N_GROUPS = SSM_WIDTH // SSM_GROUP
STATE_DIM = 64
N_STATE = N_GROUPS * STATE_DIM
SSM_CHUNKS = 4
SSM_CHUNK_IN = SSM_WIDTH // SSM_CHUNKS
SSM_CHUNK_ST = N_STATE // SSM_CHUNKS
N_KEYS = 128
N_EXPERTS = N_KEYS * N_KEYS
PEER_HEADS = 8
PEER_KEY_DIM = 128
PEER_TOPK = 16
N_MOD = 6
RMS_EPS = 1e-6

VMEM_LIMIT_BYTES = 56 * 1024 * 1024

NEG_INF = float("-inf")
Q_SCALE = (HEAD_DIM ** -0.5) * math.log2(math.e)
NEW_ROWS = 16


def _cparams(*sem):
    return pltpu.CompilerParams(dimension_semantics=sem, vmem_limit_bytes=VMEM_LIMIT_BYTES)


def _dot(a, b):
    return jnp.dot(a, b, preferred_element_type=F32)


def _dot_nt(a, b):
    return lax.dot_general(a, b, (((1,), (1,)), ((), ())), preferred_element_type=F32)


def _rms(x, w):
    return x * lax.rsqrt(jnp.mean(x * x, axis=-1, keepdims=True) + RMS_EPS) * w


def _gelu(x):
    return 0.5 * x * (1.0 + jnp.tanh(0.7978845608028654 * (x + 0.044715 * (x * x * x))))


def _sigmoid(x):
    return 1.0 / (1.0 + jnp.exp(-x))


def _mod_spec(rows_per_mod, tm):
    if rows_per_mod == 1:
        return pl.BlockSpec((1, D_MODEL), lambda i: (0, 0))
    return pl.BlockSpec((tm, D_MODEL), lambda i: (i, 0))


def _ada_kernel(c_ref, w_ref, b_ref, o_ref):
    c = c_ref[...]
    s = c * _sigmoid(c)
    o_ref[...] = _dot(s.astype(BF16), w_ref[...].astype(BF16)) + b_ref[...]


def _ada(c_all, w_ada, b_ada):
    rows = c_all.shape[0]
    tn = 1536
    return pl.pallas_call(
        _ada_kernel,
        grid=(N_MOD * D_MODEL // tn,),
        in_specs=[pl.BlockSpec((rows, D_MODEL), lambda j: (0, 0)),
                  pl.BlockSpec((D_MODEL, tn), lambda j: (0, j)),
                  pl.BlockSpec((1, tn), lambda j: (0, j))],
        out_specs=pl.BlockSpec((rows, tn), lambda j: (0, j)),
        out_shape=jax.ShapeDtypeStruct((rows, N_MOD * D_MODEL), F32),
        compiler_params=_cparams("arbitrary"),
        name="ada_mod",
    )(c_all, w_ada, b_ada.reshape(1, -1))


def _inproj_kernel(x_ref, nw_ref, sc_ref, sh_ref, cos_ref, sin_ref, w_ref,
                   q2_ref, k32_ref, kb_ref, v32_ref, vb_ref, u_ref, ga_ref, gs_ref):
    x = x_ref[...]
    h = _rms(x, nw_ref[...]) * (1.0 + sc_ref[...]) + sh_ref[...]
    hb = h.astype(BF16)
    cos = cos_ref[...]
    sin = sin_ref[...]
    lane = lax.broadcasted_iota(jnp.int32, cos.shape, 1)
    first = lane < HEAD_DIM
    W = QK_WIDTH
    for hd in range(N_HEADS):
        c0 = hd * V_DIM
        q = _dot(hb, w_ref[:, c0:c0 + V_DIM])
        qs = _dot(hb, w_ref[:, W + c0:W + c0 + V_DIM])
        qr = (q * cos + qs * sin) * Q_SCALE
        q2_ref[0, :, c0:c0 + V_DIM] = jnp.where(first, qr, 0.0).astype(BF16)
        q2_ref[1, :, c0:c0 + V_DIM] = jnp.where(first, 0.0, qr).astype(BF16)
        k = _dot(hb, w_ref[:, 2 * W + c0:2 * W + c0 + V_DIM])
        ks = _dot(hb, w_ref[:, 3 * W + c0:3 * W + c0 + V_DIM])
        kr = k * cos + ks * sin
        k32_ref[:, c0:c0 + V_DIM] = kr
        kb_ref[:, c0:c0 + V_DIM] = kr.astype(BF16)
    o = 4 * W
    v = _dot(hb, w_ref[:, o:o + ATTN_WIDTH])
    v32_ref[...] = v
    vb_ref[...] = v.astype(BF16)
    o += ATTN_WIDTH
    u_ref[...] = _dot(hb, w_ref[:, o:o + SSM_WIDTH])
    o += SSM_WIDTH
    ga_ref[...] = _dot(hb, w_ref[:, o:o + D_MODEL])
    o += D_MODEL
    gs_ref[...] = _dot(hb, w_ref[:, o:o + D_MODEL])


def _inproj(x, nw, sc, sh, cosf, sinf, w_ext, tm):
    n = x.shape[0]
    wcols = w_ext.shape[1]
    row = lambda i: (i, 0)
    full = lambda i: (0, 0)
    outs = (
        jax.ShapeDtypeStruct((2, n, QK_WIDTH), BF16),
        jax.ShapeDtypeStruct((n, QK_WIDTH), F32),
        jax.ShapeDtypeStruct((n, QK_WIDTH), BF16),
        jax.ShapeDtypeStruct((n, ATTN_WIDTH), F32),
        jax.ShapeDtypeStruct((n, ATTN_WIDTH), BF16),
        jax.ShapeDtypeStruct((n, SSM_WIDTH), F32),
        jax.ShapeDtypeStruct((n, D_MODEL), F32),
        jax.ShapeDtypeStruct((n, D_MODEL), F32),
    )
    out_specs = (
        pl.BlockSpec((2, tm, QK_WIDTH), lambda i: (0, i, 0)),
        pl.BlockSpec((tm, QK_WIDTH), row),
        pl.BlockSpec((tm, QK_WIDTH), row),
        pl.BlockSpec((tm, ATTN_WIDTH), row),
        pl.BlockSpec((tm, ATTN_WIDTH), row),
        pl.BlockSpec((tm, SSM_WIDTH), row),
        pl.BlockSpec((tm, D_MODEL), row),
        pl.BlockSpec((tm, D_MODEL), row),
    )
    return pl.pallas_call(
        _inproj_kernel,
        grid=(n // tm,),
        in_specs=[pl.BlockSpec((tm, D_MODEL), row),
                  pl.BlockSpec((1, D_MODEL), full),
                  _mod_spec(sc.shape[0], tm),
                  _mod_spec(sh.shape[0], tm),
                  pl.BlockSpec((tm, V_DIM), row),
                  pl.BlockSpec((tm, V_DIM), row),
                  pl.BlockSpec((D_MODEL, wcols), full)],
        out_specs=out_specs,
        out_shape=outs,
        compiler_params=_cparams("arbitrary"),
        name="in_proj",
    )(x, nw, sc, sh, cosf, sinf, w_ext)


def _lambda_full(lq1_ref, lk1_ref, lq2_ref, lk2_ref, lam_init):
    s1 = jnp.sum(lq1_ref[...] * lk1_ref[...], axis=-1, keepdims=True)
    s2 = jnp.sum(lq2_ref[...] * lk2_ref[...], axis=-1, keepdims=True)
    return jnp.exp(s1) - jnp.exp(s2) + lam_init


def _subln(o, w, lam_init):
    return _rms(o, w) * (1.0 - lam_init)


def _pattn_kernel(q_ref, k_ref, v_ref, lq1_ref, lk1_ref, lq2_ref, lk2_ref, sub_ref, o_ref,
                  s_ref, m_ref, acc_ref, *, tq, lam_init):
    i = pl.program_id(1)
    m_ref[...] = jnp.full(m_ref.shape, NEG_INF, F32)
    acc_ref[...] = jnp.zeros(acc_ref.shape, F32)
    ones = jnp.ones((tq, V_DIM), BF16)

    def produce(comp, chunk):
        start = pl.multiple_of(chunk * tq, tq)
        s_ref[comp] = _dot_nt(q_ref[comp], k_ref[pl.ds(start, tq), :])

    def consume(comp, chunk, masked):
        start = pl.multiple_of(chunk * tq, tq)
        v1 = jnp.concatenate([v_ref[pl.ds(start, tq), :], ones], axis=1)
        s = s_ref[comp]
        if masked:
            r = lax.broadcasted_iota(jnp.int32, (tq, tq), 0)
            c = lax.broadcasted_iota(jnp.int32, (tq, tq), 1)
            s = jnp.where(c <= r, s, NEG_INF)
        m_old = m_ref[comp]
        m_new = jnp.maximum(m_old, jnp.max(s, axis=-1, keepdims=True))
        p = jnp.exp2(s - jnp.tile(m_new, (1, tq // V_DIM)))
        alpha = jnp.exp2(m_old - m_new)
        acc_ref[comp] = jnp.tile(alpha, (1, 2)) * acc_ref[comp] + _dot(p.astype(BF16), v1)
        m_ref[comp] = m_new

    def pair(chunk, last_next):
        produce(1, chunk)
        consume(0, chunk, False)
        produce(0, last_next)
        consume(1, chunk, False)

    def body(t, carry):
        for w in range(4):
            pair(4 * t + w, 4 * t + w + 1)
        return carry

    produce(0, 0)
    lax.fori_loop(0, i // 4, body, 0)
    done = (i // 4) * 4

    @pl.when(i % 4 >= 2)
    def _():
        pair(done, done + 1)
        pair(done + 1, done + 2)

    @pl.when(i % 2 == 1)
    def _():
        pair(i - 1, i)

    produce(1, i)
    consume(0, i, True)
    consume(1, i, True)

    lam = _lambda_full(lq1_ref, lk1_ref, lq2_ref, lk2_ref, lam_init)
    o1 = acc_ref[0, :, :V_DIM] / acc_ref[0, :, V_DIM:]
    o2 = acc_ref[1, :, :V_DIM] / acc_ref[1, :, V_DIM:]
    o_ref[...] = _subln(o1 - lam * o2, sub_ref[...], lam_init).astype(BF16)


def _prompt_attention(q2, kb, vb, lq1, lk1, lq2, lk2, subln, lam_init, tq):
    t = kb.shape[0]
    vec = lambda h, i: (0, 0)
    kern = functools.partial(_pattn_kernel, tq=tq, lam_init=lam_init)
    return pl.pallas_call(
        kern,
        grid=(N_HEADS, t // tq),
        in_specs=[pl.BlockSpec((2, tq, V_DIM), lambda h, i: (0, i, h)),
                  pl.BlockSpec((t, V_DIM), lambda h, i: (0, h)),
                  pl.BlockSpec((t, V_DIM), lambda h, i: (0, h)),
                  pl.BlockSpec((1, HEAD_DIM), vec),
                  pl.BlockSpec((1, HEAD_DIM), vec),
                  pl.BlockSpec((1, HEAD_DIM), vec),
                  pl.BlockSpec((1, HEAD_DIM), vec),
                  pl.BlockSpec((1, V_DIM), vec)],
        out_specs=pl.BlockSpec((tq, V_DIM), lambda h, i: (i, h)),
        out_shape=jax.ShapeDtypeStruct((t, ATTN_WIDTH), BF16),
        scratch_shapes=[pltpu.VMEM((2, tq, tq), F32),
                        pltpu.VMEM((2, tq, V_DIM), F32),
                        pltpu.VMEM((2, tq, 2 * V_DIM), F32)],
        compiler_params=_cparams("arbitrary", "arbitrary"),
        name="prompt_attn",
    )(q2, kb, vb, lq1, lk1, lq2, lk2, subln)


def _sattn_kernel(pt_ref, q_ref, kn_ref, vn_ref, lq1_ref, lk1_ref, lq2_ref, lk2_ref, sub_ref, *rest,
                  pages_per_step, n_new, lam_init):
    g_pages = pages_per_step
    k_refs = rest[:g_pages]
    v_refs = rest[g_pages:2 * g_pages]
    o_ref = rest[2 * g_pages]
    m_ref, l_ref, acc_ref, bias_ref = rest[2 * g_pages + 1:]
    b = pl.program_id(0)
    g = pl.program_id(1)
    q = q_ref[0]
    rows_per_head = 2 * n_new

    @pl.when((b == 0) & (g == 0))
    def _():
        r = lax.broadcasted_iota(jnp.int32, bias_ref.shape, 0) // rows_per_head
        c = lax.broadcasted_iota(jnp.int32, bias_ref.shape, 1) % N_HEADS
        bias_ref[...] = jnp.where(r == c, 0.0, NEG_INF)

    @pl.when(g == 0)
    def _():
        m_ref[...] = jnp.full(m_ref.shape, NEG_INF, F32)
        l_ref[...] = jnp.zeros(l_ref.shape, F32)
        acc_ref[...] = jnp.zeros(acc_ref.shape, F32)

    def update(s, v):
        m_old = m_ref[...]
        m_new = jnp.maximum(m_old, jnp.max(s, axis=-1, keepdims=True))
        p = jnp.exp2(s - m_new)
        alpha = jnp.exp2(m_old - m_new)
        l_ref[...] = alpha * l_ref[...] + jnp.sum(p, axis=-1, keepdims=True)
        acc_ref[...] = alpha * acc_ref[...] + _dot(p.astype(BF16), v)
        m_ref[...] = m_new

    half = g_pages // 2
    scores = []
    for grp in range(2):
        k = jnp.concatenate([r[0].astype(BF16) for r in k_refs[grp * half:(grp + 1) * half]], axis=0)
        scores.append(_dot_nt(q, k))
    for grp in range(2):
        v = jnp.concatenate([r[0].astype(BF16) for r in v_refs[grp * half:(grp + 1) * half]], axis=0)
        update(scores[grp] + bias_ref[...], v)

    @pl.when(g == pl.num_programs(1) - 1)
    def _():
        kn = kn_ref[0].astype(BF16)
        vn = vn_ref[0].astype(BF16)
        s = _dot_nt(q, kn)
        row = lax.broadcasted_iota(jnp.int32, s.shape, 0)
        col = lax.broadcasted_iota(jnp.int32, s.shape, 1)
        keep = (row // rows_per_head == col % N_HEADS) & (col // N_HEADS <= row % n_new)
        update(jnp.where(keep, s, NEG_INF), vn)
        lam = _lambda_full(lq1_ref, lk1_ref, lq2_ref, lk2_ref, lam_init)
        o = acc_ref[...] / l_ref[...]
        d = o - lam * pltpu.roll(o, o.shape[0] - n_new, 0)
        o_ref[0] = _subln(d, sub_ref[...], lam_init)


def _sample_attention(page_table, q_rows, k_new, v_new, cache_k, cache_v, lq1, lk1, lq2, lk2, subln,
                      lam_init, n_new, pages_per_step):
    nb, n_pages = page_table.shape
    gp = pages_per_step
    assert gp % 2 == 0
    rows = q_rows.shape[1]
    page_rows = PAGE_SIZE * N_HEADS
    new_rows = k_new.shape[1]
    vec = lambda b, g, pt: (0, 0)

    def page_spec(r):
        return pl.BlockSpec((1, page_rows, V_DIM),
                            lambda b, g, pt: (pt[b * n_pages + g * gp + r], 0, 0))

    kern = functools.partial(_sattn_kernel, pages_per_step=gp, n_new=n_new, lam_init=lam_init)
    grid_spec = pltpu.PrefetchScalarGridSpec(
        num_scalar_prefetch=1,
        grid=(nb, n_pages // gp),
        in_specs=[pl.BlockSpec((1, rows, V_DIM), lambda b, g, pt: (b, 0, 0)),
                  pl.BlockSpec((1, new_rows, V_DIM), lambda b, g, pt: (b, 0, 0)),
                  pl.BlockSpec((1, new_rows, V_DIM), lambda b, g, pt: (b, 0, 0)),
                  pl.BlockSpec((1, HEAD_DIM), vec),
                  pl.BlockSpec((1, HEAD_DIM), vec),
                  pl.BlockSpec((1, HEAD_DIM), vec),
                  pl.BlockSpec((1, HEAD_DIM), vec),
                  pl.BlockSpec((1, V_DIM), vec)]
                 + [page_spec(r) for r in range(gp)] + [page_spec(r) for r in range(gp)],
        out_specs=pl.BlockSpec((1, rows, V_DIM), lambda b, g, pt: (b, 0, 0)),
        scratch_shapes=[pltpu.VMEM((rows, 1), F32),
                        pltpu.VMEM((rows, 1), F32),
                        pltpu.VMEM((rows, V_DIM), F32),
                        pltpu.VMEM((rows, (gp // 2) * page_rows), F32)],
    )
    return pl.pallas_call(
        kern,
        grid_spec=grid_spec,
        out_shape=jax.ShapeDtypeStruct((nb, rows, V_DIM), F32),
        compiler_params=_cparams("arbitrary", "arbitrary"),
        name="sample_attn",
    )(page_table.reshape(-1), q_rows, k_new, v_new, lq1, lk1, lq2, lk2, subln,
      *([cache_k] * gp), *([cache_v] * gp))


def _ssm_disc_kernel(lr_ref, li_ref, ls_ref, are_ref, aim_ref, fre_ref, fim_ref, *, n_pow):
    lr = lr_ref[...]
    li = li_ref[...]
    dt = jnp.exp(ls_ref[...])
    mag = jnp.exp(lr * dt)
    a_re = mag * jnp.cos(li * dt)
    a_im = mag * jnp.sin(li * dt)
    den = lr * lr + li * li
    nr = a_re - 1.0
    ni = a_im
    fre_ref[...] = (nr * lr + ni * li) / den
    fim_ref[...] = (ni * lr - nr * li) / den
    pr, pi = a_re, a_im
    for kk in range(n_pow):
        are_ref[kk] = pr
        aim_ref[kk] = pi
        pr, pi = pr * pr - pi * pi, 2.0 * pr * pi


def _ssm_bb_kernel(fre_ref, fim_ref, bre_ref, bim_ref, ore_ref, oim_ref):
    fr = fre_ref[...]
    fi = fim_ref[...]
    br = bre_ref[...]
    bi = bim_ref[...]
    ore_ref[...] = fr * br - fi * bi
    oim_ref[...] = fr * bi + fi * br


def _ssm_prepare(lam_re, lam_im, log_step, b_re, b_im, c_re, c_im, n_pow):
    gshape = jax.ShapeDtypeStruct((N_GROUPS, STATE_DIM), F32)
    pshape = jax.ShapeDtypeStruct((n_pow, N_GROUPS, STATE_DIM), F32)
    a_re, a_im, f_re, f_im = pl.pallas_call(
        functools.partial(_ssm_disc_kernel, n_pow=n_pow),
        out_shape=(pshape, pshape, gshape, gshape),
        name="ssm_discretise",
    )(lam_re, lam_im, log_step.reshape(N_GROUPS, 1))
    bshape = jax.ShapeDtypeStruct((N_STATE, SSM_GROUP), F32)
    bb_re, bb_im = pl.pallas_call(
        _ssm_bb_kernel, out_shape=(bshape, bshape), name="ssm_input_matrix",
    )(f_re.reshape(N_STATE, 1), f_im.reshape(N_STATE, 1),
      b_re.reshape(N_STATE, SSM_GROUP), b_im.reshape(N_STATE, SSM_GROUP))
    gc = N_GROUPS // SSM_CHUNKS
    eye = jnp.eye(gc, dtype=F32)

    def in_mat(bb):
        b4 = bb.reshape(SSM_CHUNKS, gc, STATE_DIM, SSM_GROUP)
        m = jnp.einsum('cgpi,gh->cgihp', b4, eye)
        return m.reshape(SSM_CHUNKS, gc * SSM_GROUP, gc * STATE_DIM)

    def out_mat(cc):
        c4 = cc.reshape(SSM_CHUNKS, gc, SSM_GROUP, STATE_DIM)
        m = jnp.einsum('cgip,gh->cgphi', c4, eye)
        return m.reshape(SSM_CHUNKS, gc * STATE_DIM, gc * SSM_GROUP)

    b_mat = jnp.concatenate([in_mat(bb_re), in_mat(bb_im)], axis=-1).astype(BF16)
    c_mat = jnp.concatenate([out_mat(c_re), out_mat(-c_im)], axis=-2).astype(BF16)
    return (a_re.reshape(n_pow, N_STATE), a_im.reshape(n_pow, N_STATE), b_mat, c_mat)


def _shift_rows(x, s, row):
    return jnp.where(row >= s, pltpu.roll(x, s, 0), 0.0)


def _ssm_scan_kernel(u_ref, are_ref, aim_ref, b_ref, c_ref, d_ref, wg_ref,
                     y_ref, hre_ref, him_ref, cre_ref, cim_ref, *, tm, n_pow):
    @pl.when(pl.program_id(0) == 0)
    def _():
        cre_ref[...] = jnp.zeros(cre_ref.shape, F32)
        cim_ref[...] = jnp.zeros(cim_ref.shape, F32)

    u = u_ref[...]
    ub = u.astype(BF16)
    row = lax.broadcasted_iota(jnp.int32, (tm, SSM_CHUNK_ST), 0)
    ys = []
    for c in range(SSM_CHUNKS):
        lanes = slice(c * SSM_CHUNK_ST, (c + 1) * SSM_CHUNK_ST)
        bu = _dot(ub[:, c * SSM_CHUNK_IN:(c + 1) * SSM_CHUNK_IN], b_ref[c])
        xr = bu[:, :SSM_CHUNK_ST]
        xi = bu[:, SSM_CHUNK_ST:]
        a_r = are_ref[0:1, lanes]
        a_i = aim_ref[0:1, lanes]
        h_r = cre_ref[0:1, lanes]
        h_i = cim_ref[0:1, lanes]
        first = row == 0
        xr = xr + jnp.where(first, a_r * h_r - a_i * h_i, 0.0)
        xi = xi + jnp.where(first, a_r * h_i + a_i * h_r, 0.0)
        for kk in range(n_pow):
            s = 1 << kk
            p_r = are_ref[kk:kk + 1, lanes]
            p_i = aim_ref[kk:kk + 1, lanes]
            sr = _shift_rows(xr, s, row)
            si = _shift_rows(xi, s, row)
            xr, xi = xr + (p_r * sr - p_i * si), xi + (p_r * si + p_i * sr)
        cre_ref[0:1, lanes] = xr[tm - 1:tm]
        cim_ref[0:1, lanes] = xi[tm - 1:tm]
        hcat = jnp.concatenate([xr, xi], axis=1).astype(BF16)
        ys.append(_dot(hcat, c_ref[c]))
    y = jnp.concatenate(ys, axis=1) + d_ref[...] * u
    y = _gelu(y)
    y = y * _sigmoid(_dot(y.astype(BF16), wg_ref[...]))
    y_ref[...] = y.astype(BF16)
    hre_ref[...] = cre_ref[0:1, :]
    him_ref[...] = cim_ref[0:1, :]


def _ssm_prompt(u, a_re, a_im, b_mat, c_mat, d, wg, tm):
    t = u.shape[0]
    n_pow = a_re.shape[0]
    full2 = lambda i: (0, 0)
    full3 = lambda i: (0, 0, 0)
    kern = functools.partial(_ssm_scan_kernel, tm=tm, n_pow=n_pow)
    return pl.pallas_call(
        kern,
        grid=(t // tm,),
        in_specs=[pl.BlockSpec((tm, SSM_WIDTH), lambda i: (i, 0)),
                  pl.BlockSpec((n_pow, N_STATE), full2),
                  pl.BlockSpec((n_pow, N_STATE), full2),
                  pl.BlockSpec(b_mat.shape, full3),
                  pl.BlockSpec(c_mat.shape, full3),
                  pl.BlockSpec((1, SSM_WIDTH), full2),
                  pl.BlockSpec((SSM_WIDTH, SSM_WIDTH), full2)],
        out_specs=(pl.BlockSpec((tm, SSM_WIDTH), lambda i: (i, 0)),
                   pl.BlockSpec((1, N_STATE), full2),
                   pl.BlockSpec((1, N_STATE), full2)),
        out_shape=(jax.ShapeDtypeStruct((t, SSM_WIDTH), BF16),
                   jax.ShapeDtypeStruct((1, N_STATE), F32),
                   jax.ShapeDtypeStruct((1, N_STATE), F32)),
        scratch_shapes=[pltpu.VMEM((8, N_STATE), F32), pltpu.VMEM((8, N_STATE), F32)],
        compiler_params=_cparams("arbitrary"),
        name="ssm_prompt",
    )(u, a_re, a_im, b_mat, c_mat, d, wg)


def _ssm_step_kernel(u_ref, h0re_ref, h0im_ref, are_ref, aim_ref, b_ref, c_ref, d_ref, wg_ref,
                     y_ref, hre_ref, him_ref, *, n_steps):
    a_r = are_ref[0:1, :]
    a_i = aim_ref[0:1, :]
    h_r = h0re_ref[...]
    h_i = h0im_ref[...]
    for t in range(n_steps):
        u = u_ref[t]
        ub = u.astype(BF16)
        brs, bis = [], []
        for c in range(SSM_CHUNKS):
            bu = _dot(ub[:, c * SSM_CHUNK_IN:(c + 1) * SSM_CHUNK_IN], b_ref[c])
            brs.append(bu[:, :SSM_CHUNK_ST])
            bis.append(bu[:, SSM_CHUNK_ST:])
        bu_r = jnp.concatenate(brs, axis=1)
        bu_i = jnp.concatenate(bis, axis=1)
        h_r, h_i = a_r * h_r - a_i * h_i + bu_r, a_r * h_i + a_i * h_r + bu_i
        ys = []
        for c in range(SSM_CHUNKS):
            lanes = slice(c * SSM_CHUNK_ST, (c + 1) * SSM_CHUNK_ST)
            hcat = jnp.concatenate([h_r[:, lanes], h_i[:, lanes]], axis=1).astype(BF16)
            ys.append(_dot(hcat, c_ref[c]))
        y = jnp.concatenate(ys, axis=1) + d_ref[...] * u
        y = _gelu(y)
        y = y * _sigmoid(_dot(y.astype(BF16), wg_ref[...]))
        y_ref[t] = y.astype(BF16)
    hre_ref[...] = h_r
    him_ref[...] = h_i


def _ssm_sample(u_tb, h0_re, h0_im, a_re, a_im, b_mat, c_mat, d, wg):
    n_steps, nb, _ = u_tb.shape
    kern = functools.partial(_ssm_step_kernel, n_steps=n_steps)
    return pl.pallas_call(
        kern,
        out_shape=(jax.ShapeDtypeStruct((n_steps, nb, SSM_WIDTH), BF16),
                   jax.ShapeDtypeStruct((nb, N_STATE), F32),
                   jax.ShapeDtypeStruct((nb, N_STATE), F32)),
        compiler_params=pltpu.CompilerParams(vmem_limit_bytes=VMEM_LIMIT_BYTES),
        name="ssm_sample",
    )(u_tb, h0_re, h0_im, a_re, a_im, b_mat, c_mat, d, wg)


def _kth_largest_rows(c, k):
    cnt = jnp.zeros((1, c.shape[1]), F32)
    tau = jnp.full((1, c.shape[1]), NEG_INF, F32)
    for _ in range(k):
        m = jnp.max(c, axis=0, keepdims=True)
        eq = c == m
        tau = jnp.where(cnt < k, m, tau)
        cnt = cnt + jnp.sum(eq.astype(F32), axis=0, keepdims=True)
        c = jnp.where(eq, NEG_INF, c)
    return tau


def _top_rows(a, k, out_ref):
    for r in range(k):
        m = jnp.max(a, axis=0, keepdims=True)
        out_ref[r:r + 1, :] = m
        a = jnp.where(a == m, NEG_INF, a)


def _mix_kernel(x_ref, o_ref, ys_ref, ga_ref, gs_ref, g1_ref, sc2_ref, sh2_ref, npost_ref, npre_ref,
                wao_ref, wso_ref, wout_ref, wq_ref, skt_ref,
                x1_ref, h2_ref, c1_ref, e1_ref, r2_ref, e2_ref, v1_ref, v2_ref, cnt_ref):
    y_attn = _dot(o_ref[...], wao_ref[...])
    y_ssm = _dot(ys_ref[...], wso_ref[...])
    merged = _sigmoid(ga_ref[...]) * y_attn + _sigmoid(gs_ref[...]) * y_ssm
    mix = _dot(merged.astype(BF16), wout_ref[...])
    x1 = x_ref[...] + g1_ref[...] * _rms(mix, npost_ref[...])
    x1_ref[...] = x1
    h2 = _rms(x1, npre_ref[...]) * (1.0 + sc2_ref[...]) + sh2_ref[...]
    h2b = h2.astype(BF16)
    h2_ref[...] = h2b
    qp = _dot(h2b, wq_ref[...]).astype(BF16)
    s_all = _dot_nt(skt_ref[...], qp)
    for hd in range(PEER_HEADS):
        r0 = hd * 2 * N_KEYS
        k0 = hd * N_KEYS
        s1 = s_all[r0:r0 + N_KEYS]
        s2 = s_all[r0 + N_KEYS:r0 + 2 * N_KEYS]
        _top_rows(s1, PEER_TOPK, v1_ref)
        _top_rows(s2, PEER_TOPK, v2_ref)
        v1lo = v1_ref[0:8, :]
        v2lo = v2_ref[0:8, :]
        r8 = lax.broadcasted_iota(jnp.int32, v1lo.shape, 0)
        m1 = v1_ref[0:1, :]
        m2 = v2_ref[0:1, :]
        pieces = [
            v1lo + m2,
            v1_ref[8:16, :] + m2,
            v1lo + v2_ref[1:2, :],
            jnp.where((r8 >= 2) & (r8 <= 4), v1lo + v2_ref[2:3, :], NEG_INF),
            jnp.where((r8 >= 2) & (r8 <= 3), v1lo + v2_ref[3:4, :], NEG_INF),
            jnp.where(r8 == 2, v1lo + v2_ref[4:5, :], NEG_INF),
            jnp.where(r8 >= 2, m1 + v2lo, NEG_INF),
            m1 + v2_ref[8:16, :],
            jnp.where(r8 >= 2, v1_ref[1:2, :] + v2lo, NEG_INF),
        ]
        cand = jnp.concatenate(pieces, axis=0)
        tau = _kth_largest_rows(cand, PEER_TOPK)
        z = jnp.sum(jnp.where(cand >= tau, jnp.exp(cand - (m1 + m2)), 0.0), axis=0, keepdims=True)
        sel = [(p >= tau).astype(F32) for p in pieces]
        n0 = jnp.sum(sel[6] + sel[7], axis=0, keepdims=True)
        n1 = jnp.sum(sel[8], axis=0, keepdims=True)
        cnt_ref[0:8, :] = (sel[0] + sel[2] + sel[3] + sel[4] + sel[5]
                           + jnp.where(r8 == 0, n0, 0.0) + jnp.where(r8 == 1, n1, 0.0))
        cnt_ref[8:16, :] = sel[1]
        c1 = jnp.zeros(s1.shape, F32)
        r2 = jnp.full(s2.shape, float(N_KEYS - 1), F32)
        for r in range(PEER_TOPK):
            c1 = jnp.where(s1 == v1_ref[r:r + 1, :], cnt_ref[r:r + 1, :], c1)
            r2 = jnp.where(s2 == v2_ref[r:r + 1, :], float(r), r2)
        c1_ref[k0:k0 + N_KEYS, :] = c1
        e1_ref[k0:k0 + N_KEYS, :] = jnp.exp(s1 - m1)
        r2_ref[k0:k0 + N_KEYS, :] = r2.astype(BF16)
        e2_ref[k0:k0 + N_KEYS, :] = (jnp.exp(s2 - m2) / z).astype(BF16)


def _mix(x, ob, ysb, ga, gs, g1, sc2, sh2, npost, npre, wao, wso, wout, wq, skt, tm):
    n = x.shape[0]
    row = lambda i: (i, 0)
    col = lambda i: (0, i)
    full = lambda i: (0, 0)
    nrow = 2 * N_KEYS * PEER_HEADS
    krow = N_KEYS * PEER_HEADS
    return pl.pallas_call(
        _mix_kernel,
        grid=(n // tm,),
        in_specs=[pl.BlockSpec((tm, D_MODEL), row),
                  pl.BlockSpec((tm, ATTN_WIDTH), row),
                  pl.BlockSpec((tm, SSM_WIDTH), row),
                  pl.BlockSpec((tm, D_MODEL), row),
                  pl.BlockSpec((tm, D_MODEL), row),
                  _mod_spec(g1.shape[0], tm),
                  _mod_spec(sc2.shape[0], tm),
                  _mod_spec(sh2.shape[0], tm),
                  pl.BlockSpec((1, D_MODEL), full),
                  pl.BlockSpec((1, D_MODEL), full),
                  pl.BlockSpec((ATTN_WIDTH, D_MODEL), full),
                  pl.BlockSpec((SSM_WIDTH, D_MODEL), full),
                  pl.BlockSpec((D_MODEL, D_MODEL), full),
                  pl.BlockSpec((D_MODEL, PEER_HEADS * PEER_KEY_DIM), full),
                  pl.BlockSpec((nrow, PEER_HEADS * PEER_KEY_DIM), full)],
        out_specs=(pl.BlockSpec((tm, D_MODEL), row),
                   pl.BlockSpec((tm, D_MODEL), row),
                   pl.BlockSpec((krow, tm), col),
                   pl.BlockSpec((krow, tm), col),
                   pl.BlockSpec((krow, tm), col),
                   pl.BlockSpec((krow, tm), col)),
        out_shape=(jax.ShapeDtypeStruct((n, D_MODEL), F32),
                   jax.ShapeDtypeStruct((n, D_MODEL), BF16),
                   jax.ShapeDtypeStruct((krow, n), F32),
                   jax.ShapeDtypeStruct((krow, n), F32),
                   jax.ShapeDtypeStruct((krow, n), BF16),
                   jax.ShapeDtypeStruct((krow, n), BF16)),
        scratch_shapes=[pltpu.VMEM((PEER_TOPK, tm), F32)] * 3,
        compiler_params=_cparams("arbitrary"),
        name="mix_route",
    )(x, ob, ysb, ga, gs, g1, sc2, sh2, npost, npre, wao, wso, wout, wq, skt)


BF16_ROWS = 16


def _tile_bf16(row):
    return jnp.broadcast_to(row, (BF16_ROWS, row.shape[1])).astype(BF16)


def _peer_kernel(h2_ref, u0_ref, ua_ref, ub_ref, vt_ref, c1_ref, e1_ref, r2_ref, e2_ref,
                 x1_ref, g2_ref, nw_ref, y_ref, acc_ref, act_ref, *, ec):
    g = pl.program_id(1)
    h2 = h2_ref[...]
    blocks = ec // N_KEYS

    @pl.when(g == 0)
    def _():
        acc_ref[...] = jnp.zeros(acc_ref.shape, F32)
        act_ref[0] = _dot_nt(u0_ref[...], h2)

    def produce(buf, u_ref):
        act_ref[buf] = _dot_nt(u_ref[...], h2)

    def consume(buf, chunk, col0):
        ws = []
        for ii in range(blocks):
            key1 = chunk * blocks + ii
            a = _gelu(act_ref[buf, ii * N_KEYS:(ii + 1) * N_KEYS, :]).astype(BF16)
            rows = [(_tile_bf16(c1_ref[pl.ds(hd * N_KEYS + key1, 1), :]),
                     _tile_bf16(e1_ref[pl.ds(hd * N_KEYS + key1, 1), :])) for hd in range(PEER_HEADS)]
            for t in range(N_KEYS // BF16_ROWS):
                r0 = t * BF16_ROWS
                gate = None
                for hd in range(PEER_HEADS):
                    k0 = hd * N_KEYS + r0
                    cnt, e1 = rows[hd]
                    r2 = r2_ref[k0:k0 + BF16_ROWS, :]
                    e2 = e2_ref[k0:k0 + BF16_ROWS, :]
                    term = jnp.where(r2 < cnt, e2, jnp.zeros_like(e2)) * e1
                    gate = term if gate is None else gate + term
                ws.append(gate * a[r0:r0 + BF16_ROWS])
        w = jnp.concatenate(ws, axis=0)
        acc_ref[...] += _dot(vt_ref[:, col0:col0 + ec], w)

    produce(1, ua_ref)
    consume(0, 2 * g, 0)
    produce(0, ub_ref)
    consume(1, 2 * g + 1, ec)

    @pl.when(g == pl.num_programs(1) - 1)
    def _():
        f = acc_ref[...].T
        y_ref[...] = x1_ref[...] + g2_ref[...] * _rms(f, nw_ref[...])


def _peer(h2b, u_bf, vt_bf, c1, e1, r2, e2, x1, g2, nw, tn, ec):
    n = h2b.shape[0]
    krow = c1.shape[0]
    n_chunks = N_EXPERTS // ec
    kern = functools.partial(_peer_kernel, ec=ec)
    g2_spec = (pl.BlockSpec((1, D_MODEL), lambda t, c: (0, 0)) if g2.shape[0] == 1
               else pl.BlockSpec((tn, D_MODEL), lambda t, c: (t, 0)))
    return pl.pallas_call(
        kern,
        grid=(n // tn, n_chunks // 2),
        in_specs=[pl.BlockSpec((tn, D_MODEL), lambda t, c: (t, 0)),
                  pl.BlockSpec((ec, D_MODEL), lambda t, c: (0, 0)),
                  pl.BlockSpec((ec, D_MODEL), lambda t, c: (2 * c + 1, 0)),
                  pl.BlockSpec((ec, D_MODEL), lambda t, c: (jnp.minimum(2 * c + 2, n_chunks - 1), 0)),
                  pl.BlockSpec((D_MODEL, 2 * ec), lambda t, c: (0, c)),
                  pl.BlockSpec((krow, tn), lambda t, c: (0, t)),
                  pl.BlockSpec((krow, tn), lambda t, c: (0, t)),
                  pl.BlockSpec((krow, tn), lambda t, c: (0, t)),
                  pl.BlockSpec((krow, tn), lambda t, c: (0, t)),
                  pl.BlockSpec((tn, D_MODEL), lambda t, c: (t, 0)),
                  g2_spec,
                  pl.BlockSpec((1, D_MODEL), lambda t, c: (0, 0))],
        out_specs=pl.BlockSpec((tn, D_MODEL), lambda t, c: (t, 0)),
        out_shape=jax.ShapeDtypeStruct((n, D_MODEL), F32),
        scratch_shapes=[pltpu.VMEM((D_MODEL, tn), F32), pltpu.VMEM((2, ec, tn), F32)],
        compiler_params=_cparams("arbitrary", "arbitrary"),
        name="peer_experts",
    )(h2b, u_bf, u_bf, u_bf, vt_bf, c1, e1, r2, e2, x1, g2, nw)


def _rope_tables(pos):
    half = HEAD_DIM // 2
    inv = ROPE_THETA ** (-jnp.arange(half, dtype=F32) / half)
    ang = pos.astype(F32)[:, None] * inv[None, :]
    cos = jnp.cos(ang)
    sin = jnp.sin(ang)
    cosf = jnp.concatenate([cos, cos, cos, cos], axis=1)
    sinf = jnp.concatenate([-sin, sin, -sin, sin], axis=1)
    return cosf, sinf


def _swap_halves(w):
    half = HEAD_DIM // 2
    k = w.shape[0]
    return w.reshape(k, -1, 2, half)[:, :, ::-1, :].reshape(k, -1)


def _tile(n, pref):
    t = min(n, pref)
    assert n % t == 0, (n, pref)
    return t


def kernel(x_prompt, x_sample, cache_k, cache_v, state_ssm_re, state_ssm_im, page_table, c_prompt, c_sample, w_ada, b_ada, norm_pre_mix, norm_post_mix, norm_pre_ffn, norm_post_ffn, w_in, lambda_q1, lambda_k1, lambda_q2, lambda_k2, subln, w_attn_out, ssm_lambda_re, ssm_lambda_im, ssm_log_step, ssm_b_re, ssm_b_im, ssm_c_re, ssm_c_im, ssm_d, w_glu, w_ssm_out, w_out, peer_w_query, peer_sub_keys, peer_u, peer_v):
    depth = w_ada.shape[0]
    bp, tp, _ = x_prompt.shape
    nb, tn_new, _ = x_sample.shape
    assert bp == 1 and 2 * tn_new == 8
    n_pages = page_table.shape[1]
    past_len = n_pages * PAGE_SIZE
    n_phys = cache_k.shape[1]
    ns = nb * tn_new

    pos_p = jnp.arange(tp, dtype=jnp.int32)
    pos_s = jnp.tile(past_len + jnp.arange(tn_new, dtype=jnp.int32), nb)
    cos_p, sin_p = _rope_tables(pos_p)
    cos_s, sin_s = _rope_tables(pos_s)

    tm_p = _tile(tp, 256)
    tq = _tile(tp, 512)
    tscan = _tile(tp, 128)
    n_pow = int(math.log2(tscan))
    assert 1 << n_pow == tscan
    pages_per_step = 8 if n_pages % 8 == 0 else 1

    yp = x_prompt.reshape(tp, D_MODEL)
    ys = x_sample.reshape(ns, D_MODEL)
    c_all = jnp.concatenate([c_prompt, c_sample], axis=0)
    pad = (-c_all.shape[0]) % 8
    c_all = jnp.pad(c_all, ((0, pad), (0, 0)))

    outs = [[] for _ in range(8)]
    for l in range(depth):
        lam_init = 0.8 - 0.6 * math.exp(-0.3 * l)
        mod = _ada(c_all, w_ada[l], b_ada[l])
        mod_p = [mod[0:1, j * D_MODEL:(j + 1) * D_MODEL] for j in range(N_MOD)]
        mod_s = [jnp.repeat(mod[1:1 + nb, j * D_MODEL:(j + 1) * D_MODEL], tn_new, axis=0)
                 for j in range(N_MOD)]

        w = w_in[l]
        wq_, wk_ = w[:, :QK_WIDTH], w[:, QK_WIDTH:2 * QK_WIDTH]
        w_ext = jnp.concatenate([wq_, _swap_halves(wq_), wk_, _swap_halves(wk_), w[:, 2 * QK_WIDTH:]],
                                axis=1).astype(BF16)
        a_re, a_im, b_mat, c_mat = _ssm_prepare(ssm_lambda_re[l], ssm_lambda_im[l], ssm_log_step[l],
                                                ssm_b_re[l], ssm_b_im[l], ssm_c_re[l], ssm_c_im[l],
                                                max(n_pow, 1))
        d_row = ssm_d[l].reshape(1, SSM_WIDTH)
        wg = w_glu[l].astype(BF16)
        wao = w_attn_out[l].astype(BF16)
        wso = w_ssm_out[l].astype(BF16)
        wout = w_out[l].astype(BF16)
        wq = peer_w_query[l].astype(BF16)
        sk = peer_sub_keys[l]
        eye = jnp.eye(PEER_HEADS * 2, dtype=F32)
        skt = jnp.einsum('bkd,bc->bkcd', sk.reshape(PEER_HEADS * 2, N_KEYS, PEER_KEY_DIM // 2), eye)
        skt = skt.reshape(PEER_HEADS * 2 * N_KEYS, PEER_HEADS * PEER_KEY_DIM).astype(BF16)
        u_bf = peer_u[l].astype(BF16)
        vt_bf = peer_v[l].T.astype(BF16)
        lq1 = lambda_q1[l].reshape(1, HEAD_DIM)
        lk1 = lambda_k1[l].reshape(1, HEAD_DIM)
        lq2 = lambda_q2[l].reshape(1, HEAD_DIM)
        lk2 = lambda_k2[l].reshape(1, HEAD_DIM)
        sub = subln[l].reshape(1, V_DIM)
        npre_mix = norm_pre_mix[l].reshape(1, D_MODEL)
        npost_mix = norm_post_mix[l].reshape(1, D_MODEL)
        npre_ffn = norm_pre_ffn[l].reshape(1, D_MODEL)
        npost_ffn = norm_post_ffn[l].reshape(1, D_MODEL)

        sh1, sc1, g1, sh2, sc2, g2 = mod_p
        q2, k32, kb, v32, vb, u, ga, gs = _inproj(yp, npre_mix, sc1, sh1, cos_p, sin_p, w_ext, tm_p)
        ob = _prompt_attention(q2, kb, vb, lq1, lk1, lq2, lk2, sub, lam_init, tq)
        ysb, hre, him = _ssm_prompt(u, a_re, a_im, b_mat, c_mat, d_row, wg, tscan)
        x1, h2b, c1, e1, r2, e2 = _mix(yp, ob, ysb, ga, gs, g1, sc2, sh2, npost_mix, npre_ffn,
                                      wao, wso, wout, wq, skt, tm_p)
        yp = _peer(h2b, u_bf, vt_bf, c1, e1, r2, e2, x1, g2, npost_ffn, _tile(tp, 512), 512)
        outs[0].append(k32.reshape(1, tp, N_HEADS, 2 * HEAD_DIM))
        outs[1].append(v32.reshape(1, tp, N_HEADS, V_DIM))
        outs[2].append(hre.reshape(1, N_GROUPS, STATE_DIM))
        outs[3].append(him.reshape(1, N_GROUPS, STATE_DIM))

        sh1, sc1, g1, sh2, sc2, g2 = mod_s
        q2, k32, _, v32, _, u, ga, gs = _inproj(ys, npre_mix, sc1, sh1, cos_s, sin_s, w_ext, ns)
        q_rows = q2.reshape(2, nb, tn_new, N_HEADS, V_DIM).transpose(1, 3, 0, 2, 4)
        q_rows = q_rows.reshape(nb, N_HEADS * 2 * tn_new, V_DIM)
        k_new = jnp.pad(k32.reshape(nb, tn_new * N_HEADS, V_DIM),
                        ((0, 0), (0, (NEW_ROWS - tn_new) * N_HEADS), (0, 0)))
        v_new = jnp.pad(v32.reshape(nb, tn_new * N_HEADS, V_DIM),
                        ((0, 0), (0, (NEW_ROWS - tn_new) * N_HEADS), (0, 0)))
        ck = cache_k.reshape(depth * n_phys, PAGE_SIZE * N_HEADS, V_DIM)
        cv = cache_v.reshape(depth * n_phys, PAGE_SIZE * N_HEADS, V_DIM)
        o_s = _sample_attention(page_table + l * n_phys, q_rows, k_new, v_new, ck, cv,
                                lq1, lk1, lq2, lk2, sub, lam_init, tn_new, pages_per_step)
        ob = o_s.reshape(nb, N_HEADS, 2, tn_new, V_DIM)[:, :, 0].transpose(0, 2, 1, 3)
        ob = ob.reshape(ns, ATTN_WIDTH).astype(BF16)
        u_tb = u.reshape(nb, tn_new, SSM_WIDTH).transpose(1, 0, 2)
        y_tb, hre_s, him_s = _ssm_sample(u_tb, state_ssm_re[l].reshape(nb, N_STATE),
                                         state_ssm_im[l].reshape(nb, N_STATE),
                                         a_re, a_im, b_mat, c_mat, d_row, wg)
        ysb = y_tb.transpose(1, 0, 2).reshape(ns, SSM_WIDTH)
        x1, h2b, c1, e1, r2, e2 = _mix(ys, ob, ysb, ga, gs, g1, sc2, sh2, npost_mix, npre_ffn,
                                      wao, wso, wout, wq, skt, ns)
        ys = _peer(h2b, u_bf, vt_bf, c1, e1, r2, e2, x1, g2, npost_ffn, ns, 512)
        outs[4].append(k32.reshape(nb, tn_new, N_HEADS, 2 * HEAD_DIM))
        outs[5].append(v32.reshape(nb, tn_new, N_HEADS, V_DIM))
        outs[6].append(hre_s.reshape(nb, N_GROUPS, STATE_DIM))
        outs[7].append(him_s.reshape(nb, N_GROUPS, STATE_DIM))

    st = [jnp.stack(o) for o in outs]
    return (yp.reshape(bp, tp, D_MODEL), ys.reshape(nb, tn_new, D_MODEL),
            st[0], st[1], st[2], st[3], st[4], st[5], st[6], st[7])
```

```python
import functools
import math

import jax
import jax.numpy as jnp
from jax import lax
from jax.experimental import pallas as pl
from jax.experimental.pallas import tpu as pltpu

F32 = jnp.float32
BF16 = jnp.bfloat16

D_MODEL = 1024
N_HEADS = 8
HEAD_DIM = 64
V_DIM = 2 * HEAD_DIM
QK_WIDTH = N_HEADS * 2 * HEAD_DIM
ATTN_WIDTH = N_HEADS * V_DIM
ROPE_THETA = 10000.0
PAGE_SIZE = 128
SSM_WIDTH = 512
SSM_GROUP = 16
N_GROUPS = SSM_WIDTH // SSM_GROUP
STATE_DIM = 64
N_STATE = N_GROUPS * STATE_DIM
SSM_CHUNKS = 4
SSM_CHUNK_IN = SSM_WIDTH // SSM_CHUNKS
SSM_CHUNK_ST = N_STATE // SSM_CHUNKS
N_KEYS = 128
N_EXPERTS = N_KEYS * N_KEYS
PEER_HEADS = 8
PEER_KEY_DIM = 128
PEER_TOPK = 16
N_MOD = 6
RMS_EPS = 1e-6

VMEM_LIMIT_BYTES = 56 * 1024 * 1024

NEG_INF = float("-inf")
Q_SCALE = (HEAD_DIM ** -0.5) * math.log2(math.e)
NEW_ROWS = 16


def _cparams(*sem):
    return pltpu.CompilerParams(dimension_semantics=sem, vmem_limit_bytes=VMEM_LIMIT_BYTES)


def _dot(a, b):
    return jnp.dot(a, b, preferred_element_type=F32)


def _dot_nt(a, b):
    return lax.dot_general(a, b, (((1,), (1,)), ((), ())), preferred_element_type=F32)


def _rms(x, w):
    return x * lax.rsqrt(jnp.mean(x * x, axis=-1, keepdims=True) + RMS_EPS) * w


def _gelu(x):
    return 0.5 * x * (1.0 + jnp.tanh(0.7978845608028654 * (x + 0.044715 * (x * x * x))))


def _sigmoid(x):
    return 1.0 / (1.0 + jnp.exp(-x))


def _mod_spec(rows_per_mod, tm):
    if rows_per_mod == 1:
        return pl.BlockSpec((1, D_MODEL), lambda i: (0, 0))
    return pl.BlockSpec((tm, D_MODEL), lambda i: (i, 0))


def _ada_kernel(c_ref, w_ref, b_ref, o_ref):
    c = c_ref[...]
    s = c * _sigmoid(c)
    o_ref[...] = _dot(s.astype(BF16), w_ref[...].astype(BF16)) + b_ref[...]


def _ada(c_all, w_ada, b_ada):
    rows = c_all.shape[0]
    tn = 1536
    return pl.pallas_call(
        _ada_kernel,
        grid=(N_MOD * D_MODEL // tn,),
        in_specs=[pl.BlockSpec((rows, D_MODEL), lambda j: (0, 0)),
                  pl.BlockSpec((D_MODEL, tn), lambda j: (0, j)),
                  pl.BlockSpec((1, tn), lambda j: (0, j))],
        out_specs=pl.BlockSpec((rows, tn), lambda j: (0, j)),
        out_shape=jax.ShapeDtypeStruct((rows, N_MOD * D_MODEL), F32),
        compiler_params=_cparams("arbitrary"),
        name="ada_mod",
    )(c_all, w_ada, b_ada.reshape(1, -1))


def _inproj_kernel(x_ref, nw_ref, sc_ref, sh_ref, cos_ref, sin_ref, w_ref,
                   q2_ref, k32_ref, kb_ref, v32_ref, vb_ref, u_ref, ga_ref, gs_ref):
    x = x_ref[...]
    h = _rms(x, nw_ref[...]) * (1.0 + sc_ref[...]) + sh_ref[...]
    hb = h.astype(BF16)
    cos = cos_ref[...]
    sin = sin_ref[...]
    lane = lax.broadcasted_iota(jnp.int32, cos.shape, 1)
    first = lane < HEAD_DIM
    low = lane % HEAD_DIM < HEAD_DIM // 2
    W = QK_WIDTH

    def rotate(x):
        partner = jnp.where(low, pltpu.roll(x, V_DIM - HEAD_DIM // 2, 1), pltpu.roll(x, HEAD_DIM // 2, 1))
        return x * cos + partner * sin

    for hd in range(N_HEADS):
        c0 = hd * V_DIM
        qr = rotate(_dot(hb, w_ref[:, c0:c0 + V_DIM])) * Q_SCALE
        q2_ref[0, :, c0:c0 + V_DIM] = jnp.where(first, qr, 0.0).astype(BF16)
        q2_ref[1, :, c0:c0 + V_DIM] = jnp.where(first, 0.0, qr).astype(BF16)
        kr = rotate(_dot(hb, w_ref[:, W + c0:W + c0 + V_DIM]))
        k32_ref[:, c0:c0 + V_DIM] = kr
        kb_ref[:, c0:c0 + V_DIM] = kr.astype(BF16)
    o = 2 * W
    v = _dot(hb, w_ref[:, o:o + ATTN_WIDTH])
    v32_ref[...] = v
    vb_ref[...] = v.astype(BF16)
    o += ATTN_WIDTH
    u_ref[...] = _dot(hb, w_ref[:, o:o + SSM_WIDTH])
    o += SSM_WIDTH
    ga_ref[...] = _dot(hb, w_ref[:, o:o + D_MODEL])
    o += D_MODEL
    gs_ref[...] = _dot(hb, w_ref[:, o:o + D_MODEL])


def _inproj(x, nw, sc, sh, cosf, sinf, w_ext, tm):
    n = x.shape[0]
    wcols = w_ext.shape[1]
    row = lambda i: (i, 0)
    full = lambda i: (0, 0)
    outs = (
        jax.ShapeDtypeStruct((2, n, QK_WIDTH), BF16),
        jax.ShapeDtypeStruct((n, QK_WIDTH), F32),
        jax.ShapeDtypeStruct((n, QK_WIDTH), BF16),
        jax.ShapeDtypeStruct((n, ATTN_WIDTH), F32),
        jax.ShapeDtypeStruct((n, ATTN_WIDTH), BF16),
        jax.ShapeDtypeStruct((n, SSM_WIDTH), F32),
        jax.ShapeDtypeStruct((n, D_MODEL), F32),
        jax.ShapeDtypeStruct((n, D_MODEL), F32),
    )
    out_specs = (
        pl.BlockSpec((2, tm, QK_WIDTH), lambda i: (0, i, 0)),
        pl.BlockSpec((tm, QK_WIDTH), row),
        pl.BlockSpec((tm, QK_WIDTH), row),
        pl.BlockSpec((tm, ATTN_WIDTH), row),
        pl.BlockSpec((tm, ATTN_WIDTH), row),
        pl.BlockSpec((tm, SSM_WIDTH), row),
        pl.BlockSpec((tm, D_MODEL), row),
        pl.BlockSpec((tm, D_MODEL), row),
    )
    return pl.pallas_call(
        _inproj_kernel,
        grid=(n // tm,),
        in_specs=[pl.BlockSpec((tm, D_MODEL), row),
                  pl.BlockSpec((1, D_MODEL), full),
                  _mod_spec(sc.shape[0], tm),
                  _mod_spec(sh.shape[0], tm),
                  pl.BlockSpec((tm, V_DIM), row),
                  pl.BlockSpec((tm, V_DIM), row),
                  pl.BlockSpec((D_MODEL, wcols), full)],
        out_specs=out_specs,
        out_shape=outs,
        compiler_params=_cparams("arbitrary"),
        name="in_proj",
    )(x, nw, sc, sh, cosf, sinf, w_ext)


def _lambda_full(lq1_ref, lk1_ref, lq2_ref, lk2_ref, lam_init):
    s1 = jnp.sum(lq1_ref[...] * lk1_ref[...], axis=-1, keepdims=True)
    s2 = jnp.sum(lq2_ref[...] * lk2_ref[...], axis=-1, keepdims=True)
    return jnp.exp(s1) - jnp.exp(s2) + lam_init


def _subln(o, w, lam_init):
    return _rms(o, w) * (1.0 - lam_init)


def _pattn_kernel(q_ref, k_ref, v_ref, lq1_ref, lk1_ref, lq2_ref, lk2_ref, sub_ref, o_ref,
                  s_ref, m_ref, acc_ref, *, tq, lam_init):
    i = pl.program_id(1)
    m_ref[...] = jnp.full(m_ref.shape, NEG_INF, F32)
    acc_ref[...] = jnp.zeros(acc_ref.shape, F32)
    ones = jnp.ones((tq, V_DIM), BF16)

    def produce(comp, chunk):
        start = pl.multiple_of(chunk * tq, tq)
        s_ref[comp] = _dot_nt(q_ref[comp], k_ref[pl.ds(start, tq), :])

    def consume(comp, chunk, masked):
        start = pl.multiple_of(chunk * tq, tq)
        v1 = jnp.concatenate([v_ref[pl.ds(start, tq), :], ones], axis=1)
        s = s_ref[comp]
        if masked:
            r = lax.broadcasted_iota(jnp.int32, (tq, tq), 0)
            c = lax.broadcasted_iota(jnp.int32, (tq, tq), 1)
            s = jnp.where(c <= r, s, NEG_INF)
        m_old = m_ref[comp]
        m_new = jnp.maximum(m_old, jnp.max(s, axis=-1, keepdims=True))
        p = jnp.exp2(s - jnp.tile(m_new, (1, tq // V_DIM)))
        alpha = jnp.exp2(m_old - m_new)
        acc_ref[comp] = jnp.tile(alpha, (1, 2)) * acc_ref[comp] + _dot(p.astype(BF16), v1)
        m_ref[comp] = m_new

    def pair(chunk, last_next):
        produce(1, chunk)
        consume(0, chunk, False)
        produce(0, last_next)
        consume(1, chunk, False)

    def body(t, carry):
        for w in range(4):
            pair(4 * t + w, 4 * t + w + 1)
        return carry

    produce(0, 0)
    lax.fori_loop(0, i // 4, body, 0)
    done = (i // 4) * 4

    @pl.when(i % 4 >= 2)
    def _():
        pair(done, done + 1)
        pair(done + 1, done + 2)

    @pl.when(i % 2 == 1)
    def _():
        pair(i - 1, i)

    produce(1, i)
    consume(0, i, True)
    consume(1, i, True)

    lam = _lambda_full(lq1_ref, lk1_ref, lq2_ref, lk2_ref, lam_init)
    o1 = acc_ref[0, :, :V_DIM] / acc_ref[0, :, V_DIM:]
    o2 = acc_ref[1, :, :V_DIM] / acc_ref[1, :, V_DIM:]
    o_ref[...] = _subln(o1 - lam * o2, sub_ref[...], lam_init).astype(BF16)


def _prompt_attention(q2, kb, vb, lq1, lk1, lq2, lk2, subln, lam_init, tq):
    t = kb.shape[0]
    vec = lambda h, i: (0, 0)
    kern = functools.partial(_pattn_kernel, tq=tq, lam_init=lam_init)
    return pl.pallas_call(
        kern,
        grid=(N_HEADS, t // tq),
        in_specs=[pl.BlockSpec((2, tq, V_DIM), lambda h, i: (0, i, h)),
                  pl.BlockSpec((t, V_DIM), lambda h, i: (0, h)),
                  pl.BlockSpec((t, V_DIM), lambda h, i: (0, h)),
                  pl.BlockSpec((1, HEAD_DIM), vec),
                  pl.BlockSpec((1, HEAD_DIM), vec),
                  pl.BlockSpec((1, HEAD_DIM), vec),
                  pl.BlockSpec((1, HEAD_DIM), vec),
                  pl.BlockSpec((1, V_DIM), vec)],
        out_specs=pl.BlockSpec((tq, V_DIM), lambda h, i: (i, h)),
        out_shape=jax.ShapeDtypeStruct((t, ATTN_WIDTH), BF16),
        scratch_shapes=[pltpu.VMEM((2, tq, tq), F32),
                        pltpu.VMEM((2, tq, V_DIM), F32),
                        pltpu.VMEM((2, tq, 2 * V_DIM), F32)],
        compiler_params=_cparams("arbitrary", "arbitrary"),
        name="prompt_attn",
    )(q2, kb, vb, lq1, lk1, lq2, lk2, subln)


def _sattn_kernel(pt_ref, q_ref, kn_ref, vn_ref, lq1_ref, lk1_ref, lq2_ref, lk2_ref, sub_ref, *rest,
                  pages_per_step, n_new, lam_init):
    g_pages = pages_per_step
    k_refs = rest[:g_pages]
    v_refs = rest[g_pages:2 * g_pages]
    o_ref = rest[2 * g_pages]
    m_ref, l_ref, acc_ref, bias_ref = rest[2 * g_pages + 1:]
    b = pl.program_id(0)
    g = pl.program_id(1)
    q = q_ref[0]
    rows_per_head = 2 * n_new

    @pl.when((b == 0) & (g == 0))
    def _():
        r = lax.broadcasted_iota(jnp.int32, bias_ref.shape, 0) // rows_per_head
        c = lax.broadcasted_iota(jnp.int32, bias_ref.shape, 1) % N_HEADS
        bias_ref[...] = jnp.where(r == c, 0.0, NEG_INF)

    @pl.when(g == 0)
    def _():
        m_ref[...] = jnp.full(m_ref.shape, NEG_INF, F32)
        l_ref[...] = jnp.zeros(l_ref.shape, F32)
        acc_ref[...] = jnp.zeros(acc_ref.shape, F32)

    def update(s, v):
        m_old = m_ref[...]
        m_new = jnp.maximum(m_old, jnp.max(s, axis=-1, keepdims=True))
        p = jnp.exp2(s - m_new)
        alpha = jnp.exp2(m_old - m_new)
        l_ref[...] = alpha * l_ref[...] + jnp.sum(p, axis=-1, keepdims=True)
        acc_ref[...] = alpha * acc_ref[...] + _dot(p.astype(BF16), v)
        m_ref[...] = m_new

    half = g_pages // 2
    scores = []
    for grp in range(2):
        k = jnp.concatenate([r[0].astype(BF16) for r in k_refs[grp * half:(grp + 1) * half]], axis=0)
        scores.append(_dot_nt(q, k))
    for grp in range(2):
        v = jnp.concatenate([r[0].astype(BF16) for r in v_refs[grp * half:(grp + 1) * half]], axis=0)
        update(scores[grp] + bias_ref[...], v)

    @pl.when(g == pl.num_programs(1) - 1)
    def _():
        kn = kn_ref[0].astype(BF16)
        vn = vn_ref[0].astype(BF16)
        s = _dot_nt(q, kn)
        row = lax.broadcasted_iota(jnp.int32, s.shape, 0)
        col = lax.broadcasted_iota(jnp.int32, s.shape, 1)
        keep = (row // rows_per_head == col % N_HEADS) & (col // N_HEADS <= row % n_new)
        update(jnp.where(keep, s, NEG_INF), vn)
        lam = _lambda_full(lq1_ref, lk1_ref, lq2_ref, lk2_ref, lam_init)
        o = acc_ref[...] / l_ref[...]
        d = o - lam * pltpu.roll(o, o.shape[0] - n_new, 0)
        o_ref[0] = _subln(d, sub_ref[...], lam_init)


def _sample_attention(page_table, q_rows, k_new, v_new, cache_k, cache_v, lq1, lk1, lq2, lk2, subln,
                      lam_init, n_new, pages_per_step):
    nb, n_pages = page_table.shape
    gp = pages_per_step
    assert gp % 2 == 0
    rows = q_rows.shape[1]
    page_rows = PAGE_SIZE * N_HEADS
    new_rows = k_new.shape[1]
    vec = lambda b, g, pt: (0, 0)

    def page_spec(r):
        return pl.BlockSpec((1, page_rows, V_DIM),
                            lambda b, g, pt: (pt[b * n_pages + g * gp + r], 0, 0))

    kern = functools.partial(_sattn_kernel, pages_per_step=gp, n_new=n_new, lam_init=lam_init)
    grid_spec = pltpu.PrefetchScalarGridSpec(
        num_scalar_prefetch=1,
        grid=(nb, n_pages // gp),
        in_specs=[pl.BlockSpec((1, rows, V_DIM), lambda b, g, pt: (b, 0, 0)),
                  pl.BlockSpec((1, new_rows, V_DIM), lambda b, g, pt: (b, 0, 0)),
                  pl.BlockSpec((1, new_rows, V_DIM), lambda b, g, pt: (b, 0, 0)),
                  pl.BlockSpec((1, HEAD_DIM), vec),
                  pl.BlockSpec((1, HEAD_DIM), vec),
                  pl.BlockSpec((1, HEAD_DIM), vec),
                  pl.BlockSpec((1, HEAD_DIM), vec),
                  pl.BlockSpec((1, V_DIM), vec)]
                 + [page_spec(r) for r in range(gp)] + [page_spec(r) for r in range(gp)],
        out_specs=pl.BlockSpec((1, rows, V_DIM), lambda b, g, pt: (b, 0, 0)),
        scratch_shapes=[pltpu.VMEM((rows, 1), F32),
                        pltpu.VMEM((rows, 1), F32),
                        pltpu.VMEM((rows, V_DIM), F32),
                        pltpu.VMEM((rows, (gp // 2) * page_rows), F32)],
    )
    return pl.pallas_call(
        kern,
        grid_spec=grid_spec,
        out_shape=jax.ShapeDtypeStruct((nb, rows, V_DIM), F32),
        compiler_params=_cparams("arbitrary", "arbitrary"),
        name="sample_attn",
    )(page_table.reshape(-1), q_rows, k_new, v_new, lq1, lk1, lq2, lk2, subln,
      *([cache_k] * gp), *([cache_v] * gp))


def _ssm_disc_kernel(lr_ref, li_ref, ls_ref, are_ref, aim_ref, fre_ref, fim_ref, *, n_pow):
    lr = lr_ref[...]
    li = li_ref[...]
    dt = jnp.exp(ls_ref[...])
    mag = jnp.exp(lr * dt)
    a_re = mag * jnp.cos(li * dt)
    a_im = mag * jnp.sin(li * dt)
    den = lr * lr + li * li
    nr = a_re - 1.0
    ni = a_im
    fre_ref[...] = (nr * lr + ni * li) / den
    fim_ref[...] = (ni * lr - nr * li) / den
    pr, pi = a_re, a_im
    for kk in range(n_pow):
        are_ref[kk] = pr
        aim_ref[kk] = pi
        pr, pi = pr * pr - pi * pi, 2.0 * pr * pi


def _ssm_bb_kernel(fre_ref, fim_ref, bre_ref, bim_ref, ore_ref, oim_ref):
    fr = fre_ref[...]
    fi = fim_ref[...]
    br = bre_ref[...]
    bi = bim_ref[...]
    ore_ref[...] = fr * br - fi * bi
    oim_ref[...] = fr * bi + fi * br


def _ssm_prepare(lam_re, lam_im, log_step, b_re, b_im, c_re, c_im, n_pow):
    gshape = jax.ShapeDtypeStruct((N_GROUPS, STATE_DIM), F32)
    pshape = jax.ShapeDtypeStruct((n_pow, N_GROUPS, STATE_DIM), F32)
    a_re, a_im, f_re, f_im = pl.pallas_call(
        functools.partial(_ssm_disc_kernel, n_pow=n_pow),
        out_shape=(pshape, pshape, gshape, gshape),
        name="ssm_discretise",
    )(lam_re, lam_im, log_step.reshape(N_GROUPS, 1))
    bshape = jax.ShapeDtypeStruct((N_STATE, SSM_GROUP), F32)
    bb_re, bb_im = pl.pallas_call(
        _ssm_bb_kernel, out_shape=(bshape, bshape), name="ssm_input_matrix",
    )(f_re.reshape(N_STATE, 1), f_im.reshape(N_STATE, 1),
      b_re.reshape(N_STATE, SSM_GROUP), b_im.reshape(N_STATE, SSM_GROUP))
    gc = N_GROUPS // SSM_CHUNKS
    eye = jnp.eye(gc, dtype=F32)

    def in_mat(bb):
        b4 = bb.reshape(SSM_CHUNKS, gc, STATE_DIM, SSM_GROUP)
        m = jnp.einsum('cgpi,gh->cgihp', b4, eye)
        return m.reshape(SSM_CHUNKS, gc * SSM_GROUP, gc * STATE_DIM)

    def out_mat(cc):
        c4 = cc.reshape(SSM_CHUNKS, gc, SSM_GROUP, STATE_DIM)
        m = jnp.einsum('cgip,gh->cgphi', c4, eye)
        return m.reshape(SSM_CHUNKS, gc * STATE_DIM, gc * SSM_GROUP)

    b_mat = jnp.concatenate([in_mat(bb_re), in_mat(bb_im)], axis=-1).astype(BF16)
    c_mat = jnp.concatenate([out_mat(c_re), out_mat(-c_im)], axis=-2).astype(BF16)
    return (a_re.reshape(n_pow, N_STATE), a_im.reshape(n_pow, N_STATE), b_mat, c_mat)


def _shift_rows(x, s, row):
    return jnp.where(row >= s, pltpu.roll(x, s, 0), 0.0)


def _ssm_scan_kernel(u_ref, are_ref, aim_ref, b_ref, c_ref, d_ref, wg_ref,
                     y_ref, hre_ref, him_ref, cre_ref, cim_ref, *, tm, n_pow):
    @pl.when(pl.program_id(0) == 0)
    def _():
        cre_ref[...] = jnp.zeros(cre_ref.shape, F32)
        cim_ref[...] = jnp.zeros(cim_ref.shape, F32)

    u = u_ref[...]
    ub = u.astype(BF16)
    row = lax.broadcasted_iota(jnp.int32, (tm, SSM_CHUNK_ST), 0)
    ys = []
    for c in range(SSM_CHUNKS):
        lanes = slice(c * SSM_CHUNK_ST, (c + 1) * SSM_CHUNK_ST)
        bu = _dot(ub[:, c * SSM_CHUNK_IN:(c + 1) * SSM_CHUNK_IN], b_ref[c])
        xr = bu[:, :SSM_CHUNK_ST]
        xi = bu[:, SSM_CHUNK_ST:]
        a_r = are_ref[0:1, lanes]
        a_i = aim_ref[0:1, lanes]
        h_r = cre_ref[0:1, lanes]
        h_i = cim_ref[0:1, lanes]
        first = row == 0
        xr = xr + jnp.where(first, a_r * h_r - a_i * h_i, 0.0)
        xi = xi + jnp.where(first, a_r * h_i + a_i * h_r, 0.0)
        for kk in range(n_pow):
            s = 1 << kk
            p_r = are_ref[kk:kk + 1, lanes]
            p_i = aim_ref[kk:kk + 1, lanes]
            sr = _shift_rows(xr, s, row)
            si = _shift_rows(xi, s, row)
            xr, xi = xr + (p_r * sr - p_i * si), xi + (p_r * si + p_i * sr)
        cre_ref[0:1, lanes] = xr[tm - 1:tm]
        cim_ref[0:1, lanes] = xi[tm - 1:tm]
        hcat = jnp.concatenate([xr, xi], axis=1).astype(BF16)
        ys.append(_dot(hcat, c_ref[c]))
    y = jnp.concatenate(ys, axis=1) + d_ref[...] * u
    y = _gelu(y)
    y = y * _sigmoid(_dot(y.astype(BF16), wg_ref[...]))
    y_ref[...] = y.astype(BF16)
    hre_ref[...] = cre_ref[0:1, :]
    him_ref[...] = cim_ref[0:1, :]


def _ssm_prompt(u, a_re, a_im, b_mat, c_mat, d, wg, tm):
    t = u.shape[0]
    n_pow = a_re.shape[0]
    full2 = lambda i: (0, 0)
    full3 = lambda i: (0, 0, 0)
    kern = functools.partial(_ssm_scan_kernel, tm=tm, n_pow=n_pow)
    return pl.pallas_call(
        kern,
        grid=(t // tm,),
        in_specs=[pl.BlockSpec((tm, SSM_WIDTH), lambda i: (i, 0)),
                  pl.BlockSpec((n_pow, N_STATE), full2),
                  pl.BlockSpec((n_pow, N_STATE), full2),
                  pl.BlockSpec(b_mat.shape, full3),
                  pl.BlockSpec(c_mat.shape, full3),
                  pl.BlockSpec((1, SSM_WIDTH), full2),
                  pl.BlockSpec((SSM_WIDTH, SSM_WIDTH), full2)],
        out_specs=(pl.BlockSpec((tm, SSM_WIDTH), lambda i: (i, 0)),
                   pl.BlockSpec((1, N_STATE), full2),
                   pl.BlockSpec((1, N_STATE), full2)),
        out_shape=(jax.ShapeDtypeStruct((t, SSM_WIDTH), BF16),
                   jax.ShapeDtypeStruct((1, N_STATE), F32),
                   jax.ShapeDtypeStruct((1, N_STATE), F32)),
        scratch_shapes=[pltpu.VMEM((8, N_STATE), F32), pltpu.VMEM((8, N_STATE), F32)],
        compiler_params=_cparams("arbitrary"),
        name="ssm_prompt",
    )(u, a_re, a_im, b_mat, c_mat, d, wg)


def _ssm_step_kernel(u_ref, h0re_ref, h0im_ref, are_ref, aim_ref, b_ref, c_ref, d_ref, wg_ref,
                     y_ref, hre_ref, him_ref, *, n_steps):
    a_r = are_ref[0:1, :]
    a_i = aim_ref[0:1, :]
    h_r = h0re_ref[...]
    h_i = h0im_ref[...]
    for t in range(n_steps):
        u = u_ref[t]
        ub = u.astype(BF16)
        brs, bis = [], []
        for c in range(SSM_CHUNKS):
            bu = _dot(ub[:, c * SSM_CHUNK_IN:(c + 1) * SSM_CHUNK_IN], b_ref[c])
            brs.append(bu[:, :SSM_CHUNK_ST])
            bis.append(bu[:, SSM_CHUNK_ST:])
        bu_r = jnp.concatenate(brs, axis=1)
        bu_i = jnp.concatenate(bis, axis=1)
        h_r, h_i = a_r * h_r - a_i * h_i + bu_r, a_r * h_i + a_i * h_r + bu_i
        ys = []
        for c in range(SSM_CHUNKS):
            lanes = slice(c * SSM_CHUNK_ST, (c + 1) * SSM_CHUNK_ST)
            hcat = jnp.concatenate([h_r[:, lanes], h_i[:, lanes]], axis=1).astype(BF16)
            ys.append(_dot(hcat, c_ref[c]))
        y = jnp.concatenate(ys, axis=1) + d_ref[...] * u
        y = _gelu(y)
        y = y * _sigmoid(_dot(y.astype(BF16), wg_ref[...]))
        y_ref[t] = y.astype(BF16)
    hre_ref[...] = h_r
    him_ref[...] = h_i


def _ssm_sample(u_tb, h0_re, h0_im, a_re, a_im, b_mat, c_mat, d, wg):
    n_steps, nb, _ = u_tb.shape
    kern = functools.partial(_ssm_step_kernel, n_steps=n_steps)
    return pl.pallas_call(
        kern,
        out_shape=(jax.ShapeDtypeStruct((n_steps, nb, SSM_WIDTH), BF16),
                   jax.ShapeDtypeStruct((nb, N_STATE), F32),
                   jax.ShapeDtypeStruct((nb, N_STATE), F32)),
        compiler_params=pltpu.CompilerParams(vmem_limit_bytes=VMEM_LIMIT_BYTES),
        name="ssm_sample",
    )(u_tb, h0_re, h0_im, a_re, a_im, b_mat, c_mat, d, wg)


def _kth_largest_rows(c, k):
    cnt = jnp.zeros((1, c.shape[1]), F32)
    tau = jnp.full((1, c.shape[1]), NEG_INF, F32)
    for _ in range(k):
        m = jnp.max(c, axis=0, keepdims=True)
        eq = c == m
        tau = jnp.where(cnt < k, m, tau)
        cnt = cnt + jnp.sum(eq.astype(F32), axis=0, keepdims=True)
        c = jnp.where(eq, NEG_INF, c)
    return tau


def _top_rows(a, k, out_ref):
    for r in range(k):
        m = jnp.max(a, axis=0, keepdims=True)
        out_ref[r:r + 1, :] = m
        a = jnp.where(a == m, NEG_INF, a)


def _mix_kernel(x_ref, o_ref, ys_ref, ga_ref, gs_ref, g1_ref, sc2_ref, sh2_ref, npost_ref, npre_ref,
                wao_ref, wso_ref, wout_ref, wq_ref, skt_ref,
                x1_ref, h2_ref, c1_ref, e1_ref, r2_ref, e2_ref, v1_ref, v2_ref, cnt_ref):
    y_attn = _dot(o_ref[...], wao_ref[...])
    y_ssm = _dot(ys_ref[...], wso_ref[...])
    merged = _sigmoid(ga_ref[...]) * y_attn + _sigmoid(gs_ref[...]) * y_ssm
    mix = _dot(merged.astype(BF16), wout_ref[...])
    x1 = x_ref[...] + g1_ref[...] * _rms(mix, npost_ref[...])
    x1_ref[...] = x1
    h2 = _rms(x1, npre_ref[...]) * (1.0 + sc2_ref[...]) + sh2_ref[...]
    h2b = h2.astype(BF16)
    h2_ref[...] = h2b
    qp = _dot(h2b, wq_ref[...]).astype(BF16)
    s_all = _dot_nt(skt_ref[...], qp)
    for hd in range(PEER_HEADS):
        r0 = hd * 2 * N_KEYS
        k0 = hd * N_KEYS
        s1 = s_all[r0:r0 + N_KEYS]
        s2 = s_all[r0 + N_KEYS:r0 + 2 * N_KEYS]
        _top_rows(s1, PEER_TOPK, v1_ref)
        _top_rows(s2, PEER_TOPK, v2_ref)
        v1lo = v1_ref[0:8, :]
        v2lo = v2_ref[0:8, :]
        r8 = lax.broadcasted_iota(jnp.int32, v1lo.shape, 0)
        m1 = v1_ref[0:1, :]
        m2 = v2_ref[0:1, :]
        pieces = [
            v1lo + m2,
            v1_ref[8:16, :] + m2,
            v1lo + v2_ref[1:2, :],
            jnp.where((r8 >= 2) & (r8 <= 4), v1lo + v2_ref[2:3, :], NEG_INF),
            jnp.where((r8 >= 2) & (r8 <= 3), v1lo + v2_ref[3:4, :], NEG_INF),
            jnp.where(r8 == 2, v1lo + v2_ref[4:5, :], NEG_INF),
            jnp.where(r8 >= 2, m1 + v2lo, NEG_INF),
            m1 + v2_ref[8:16, :],
            jnp.where(r8 >= 2, v1_ref[1:2, :] + v2lo, NEG_INF),
        ]
        cand = jnp.concatenate(pieces, axis=0)
        tau = _kth_largest_rows(cand, PEER_TOPK)
        z = jnp.sum(jnp.where(cand >= tau, jnp.exp(cand - (m1 + m2)), 0.0), axis=0, keepdims=True)
        sel = [(p >= tau).astype(F32) for p in pieces]
        n0 = jnp.sum(sel[6] + sel[7], axis=0, keepdims=True)
        n1 = jnp.sum(sel[8], axis=0, keepdims=True)
        cnt_ref[0:8, :] = (sel[0] + sel[2] + sel[3] + sel[4] + sel[5]
                           + jnp.where(r8 == 0, n0, 0.0) + jnp.where(r8 == 1, n1, 0.0))
        cnt_ref[8:16, :] = sel[1]
        c1 = jnp.zeros(s1.shape, F32)
        r2 = jnp.full(s2.shape, float(N_KEYS - 1), F32)
        for r in range(PEER_TOPK):
            c1 = jnp.where(s1 == v1_ref[r:r + 1, :], cnt_ref[r:r + 1, :], c1)
            r2 = jnp.where(s2 == v2_ref[r:r + 1, :], float(r), r2)
        c1_ref[k0:k0 + N_KEYS, :] = c1
        e1_ref[k0:k0 + N_KEYS, :] = jnp.exp(s1 - m1)
        r2_ref[k0:k0 + N_KEYS, :] = r2.astype(BF16)
        e2_ref[k0:k0 + N_KEYS, :] = (jnp.exp(s2 - m2) / z).astype(BF16)


def _mix(x, ob, ysb, ga, gs, g1, sc2, sh2, npost, npre, wao, wso, wout, wq, skt, tm):
    n = x.shape[0]
    row = lambda i: (i, 0)
    col = lambda i: (0, i)
    full = lambda i: (0, 0)
    nrow = 2 * N_KEYS * PEER_HEADS
    krow = N_KEYS * PEER_HEADS
    return pl.pallas_call(
        _mix_kernel,
        grid=(n // tm,),
        in_specs=[pl.BlockSpec((tm, D_MODEL), row),
                  pl.BlockSpec((tm, ATTN_WIDTH), row),
                  pl.BlockSpec((tm, SSM_WIDTH), row),
                  pl.BlockSpec((tm, D_MODEL), row),
                  pl.BlockSpec((tm, D_MODEL), row),
                  _mod_spec(g1.shape[0], tm),
                  _mod_spec(sc2.shape[0], tm),
                  _mod_spec(sh2.shape[0], tm),
                  pl.BlockSpec((1, D_MODEL), full),
                  pl.BlockSpec((1, D_MODEL), full),
                  pl.BlockSpec((ATTN_WIDTH, D_MODEL), full),
                  pl.BlockSpec((SSM_WIDTH, D_MODEL), full),
                  pl.BlockSpec((D_MODEL, D_MODEL), full),
                  pl.BlockSpec((D_MODEL, PEER_HEADS * PEER_KEY_DIM), full),
                  pl.BlockSpec((nrow, PEER_HEADS * PEER_KEY_DIM), full)],
        out_specs=(pl.BlockSpec((tm, D_MODEL), row),
                   pl.BlockSpec((tm, D_MODEL), row),
                   pl.BlockSpec((krow, tm), col),
                   pl.BlockSpec((krow, tm), col),
                   pl.BlockSpec((krow, tm), col),
                   pl.BlockSpec((krow, tm), col)),
        out_shape=(jax.ShapeDtypeStruct((n, D_MODEL), F32),
                   jax.ShapeDtypeStruct((n, D_MODEL), BF16),
                   jax.ShapeDtypeStruct((krow, n), F32),
                   jax.ShapeDtypeStruct((krow, n), F32),
                   jax.ShapeDtypeStruct((krow, n), BF16),
                   jax.ShapeDtypeStruct((krow, n), BF16)),
        scratch_shapes=[pltpu.VMEM((PEER_TOPK, tm), F32)] * 3,
        compiler_params=_cparams("arbitrary"),
        name="mix_route",
    )(x, ob, ysb, ga, gs, g1, sc2, sh2, npost, npre, wao, wso, wout, wq, skt)


BF16_ROWS = 16
PEER_CHUNKS_PER_STEP = 2


def _tile_bf16(row):
    return jnp.broadcast_to(row, (BF16_ROWS, row.shape[1])).astype(BF16)


def _peer_kernel(h2_ref, u0_ref, *rest, ec, n_sub):
    u_refs = rest[:n_sub]
    (vt_ref, c1_ref, e1_ref, r2_ref, e2_ref, x1_ref, g2_ref, nw_ref,
     y_ref, acc_ref, act0_ref, act1_ref) = rest[n_sub:]
    acts = (act0_ref, act1_ref)
    g = pl.program_id(1)
    h2 = h2_ref[...]
    blocks = ec // N_KEYS

    @pl.when(g == 0)
    def _():
        acc_ref[...] = jnp.zeros(acc_ref.shape, F32)
        act0_ref[...] = _dot_nt(u0_ref[...], h2)

    def consume(act_ref, chunk, col0):
        ws = []
        for ii in range(blocks):
            key1 = chunk * blocks + ii
            a = _gelu(act_ref[ii * N_KEYS:(ii + 1) * N_KEYS, :]).astype(BF16)
            tiles = N_KEYS // BF16_ROWS
            totals = [None] * tiles
            for hd in range(PEER_HEADS):
                cnt = _tile_bf16(c1_ref[pl.ds(hd * N_KEYS + key1, 1), :])
                e1 = _tile_bf16(e1_ref[pl.ds(hd * N_KEYS + key1, 1), :])
                for t in range(tiles):
                    k0 = hd * N_KEYS + t * BF16_ROWS
                    r2 = r2_ref[k0:k0 + BF16_ROWS, :]
                    e2 = e2_ref[k0:k0 + BF16_ROWS, :]
                    term = jnp.where(r2 < cnt, e2, jnp.zeros_like(e2)) * e1
                    totals[t] = term if totals[t] is None else totals[t] + term
            for t in range(tiles):
                ws.append(totals[t] * a[t * BF16_ROWS:(t + 1) * BF16_ROWS])
        w = jnp.concatenate(ws, axis=0)
        acc_ref[...] += _dot(vt_ref[:, col0:col0 + ec], w)

    for s in range(n_sub):
        acts[(s + 1) % 2][...] = _dot_nt(u_refs[s][...], h2)
        consume(acts[s % 2], n_sub * g + s, s * ec)

    @pl.when(g == pl.num_programs(1) - 1)
    def _():
        f = acc_ref[...].T
        y_ref[...] = x1_ref[...] + g2_ref[...] * _rms(f, nw_ref[...])


def _peer(h2b, u_bf, vt_bf, c1, e1, r2, e2, x1, g2, nw, tn, ec):
    n = h2b.shape[0]
    krow = c1.shape[0]
    n_chunks = N_EXPERTS // ec
    n_sub = PEER_CHUNKS_PER_STEP
    kern = functools.partial(_peer_kernel, ec=ec, n_sub=n_sub)
    g2_spec = (pl.BlockSpec((1, D_MODEL), lambda t, c: (0, 0)) if g2.shape[0] == 1
               else pl.BlockSpec((tn, D_MODEL), lambda t, c: (t, 0)))

    def lookahead_spec(s):
        return pl.BlockSpec((ec, D_MODEL), lambda t, c: (jnp.minimum(n_sub * c + s + 1, n_chunks - 1), 0))

    return pl.pallas_call(
        kern,
        grid=(n // tn, n_chunks // n_sub),
        in_specs=[pl.BlockSpec((tn, D_MODEL), lambda t, c: (t, 0)),
                  pl.BlockSpec((ec, D_MODEL), lambda t, c: (0, 0))]
                 + [lookahead_spec(s) for s in range(n_sub)]
                 + [pl.BlockSpec((D_MODEL, n_sub * ec), lambda t, c: (0, c)),
                  pl.BlockSpec((krow, tn), lambda t, c: (0, t)),
                  pl.BlockSpec((krow, tn), lambda t, c: (0, t)),
                  pl.BlockSpec((krow, tn), lambda t, c: (0, t)),
                  pl.BlockSpec((krow, tn), lambda t, c: (0, t)),
                  pl.BlockSpec((tn, D_MODEL), lambda t, c: (t, 0)),
                  g2_spec,
                  pl.BlockSpec((1, D_MODEL), lambda t, c: (0, 0))],
        out_specs=pl.BlockSpec((tn, D_MODEL), lambda t, c: (t, 0)),
        out_shape=jax.ShapeDtypeStruct((n, D_MODEL), F32),
        scratch_shapes=[pltpu.VMEM((D_MODEL, tn), F32),
                        pltpu.VMEM((ec, tn), F32), pltpu.VMEM((ec, tn), F32)],
        compiler_params=_cparams("arbitrary", "arbitrary"),
        name="peer_experts",
    )(h2b, u_bf, *([u_bf] * n_sub), vt_bf, c1, e1, r2, e2, x1, g2, nw)


def _rope_tables(pos):
    half = HEAD_DIM // 2
    inv = ROPE_THETA ** (-jnp.arange(half, dtype=F32) / half)
    ang = pos.astype(F32)[:, None] * inv[None, :]
    cos = jnp.cos(ang)
    sin = jnp.sin(ang)
    cosf = jnp.concatenate([cos, cos, cos, cos], axis=1)
    sinf = jnp.concatenate([-sin, sin, -sin, sin], axis=1)
    return cosf, sinf


def _tile(n, pref):
    t = min(n, pref)
    assert n % t == 0, (n, pref)
    return t


def kernel(x_prompt, x_sample, cache_k, cache_v, state_ssm_re, state_ssm_im, page_table, c_prompt, c_sample, w_ada, b_ada, norm_pre_mix, norm_post_mix, norm_pre_ffn, norm_post_ffn, w_in, lambda_q1, lambda_k1, lambda_q2, lambda_k2, subln, w_attn_out, ssm_lambda_re, ssm_lambda_im, ssm_log_step, ssm_b_re, ssm_b_im, ssm_c_re, ssm_c_im, ssm_d, w_glu, w_ssm_out, w_out, peer_w_query, peer_sub_keys, peer_u, peer_v):
    depth = w_ada.shape[0]
    bp, tp, _ = x_prompt.shape
    nb, tn_new, _ = x_sample.shape
    assert bp == 1 and 2 * tn_new == 8
    n_pages = page_table.shape[1]
    past_len = n_pages * PAGE_SIZE
    n_phys = cache_k.shape[1]
    ns = nb * tn_new

    pos_p = jnp.arange(tp, dtype=jnp.int32)
    pos_s = jnp.tile(past_len + jnp.arange(tn_new, dtype=jnp.int32), nb)
    cos_p, sin_p = _rope_tables(pos_p)
    cos_s, sin_s = _rope_tables(pos_s)

    tm_p = _tile(tp, 256)
    tq = _tile(tp, 512)
    tscan = _tile(tp, 128)
    n_pow = int(math.log2(tscan))
    assert 1 << n_pow == tscan
    pages_per_step = next(p for p in (16, 8, 4, 2) if n_pages % p == 0)

    yp = x_prompt.reshape(tp, D_MODEL)
    ys = x_sample.reshape(ns, D_MODEL)
    c_all = jnp.concatenate([c_prompt, c_sample], axis=0)
    pad = (-c_all.shape[0]) % 8
    c_all = jnp.pad(c_all, ((0, pad), (0, 0)))

    outs = [[] for _ in range(8)]
    for l in range(depth):
        lam_init = 0.8 - 0.6 * math.exp(-0.3 * l)
        mod = _ada(c_all, w_ada[l], b_ada[l])
        mod_p = [mod[0:1, j * D_MODEL:(j + 1) * D_MODEL] for j in range(N_MOD)]
        mod_s = [jnp.repeat(mod[1:1 + nb, j * D_MODEL:(j + 1) * D_MODEL], tn_new, axis=0)
                 for j in range(N_MOD)]

        w_ext = w_in[l].astype(BF16)
        a_re, a_im, b_mat, c_mat = _ssm_prepare(ssm_lambda_re[l], ssm_lambda_im[l], ssm_log_step[l],
                                                ssm_b_re[l], ssm_b_im[l], ssm_c_re[l], ssm_c_im[l],
                                                max(n_pow, 1))
        d_row = ssm_d[l].reshape(1, SSM_WIDTH)
        wg = w_glu[l].astype(BF16)
        wao = w_attn_out[l].astype(BF16)
        wso = w_ssm_out[l].astype(BF16)
        wout = w_out[l].astype(BF16)
        wq = peer_w_query[l].astype(BF16)
        sk = peer_sub_keys[l]
        eye = jnp.eye(PEER_HEADS * 2, dtype=F32)
        skt = jnp.einsum('bkd,bc->bkcd', sk.reshape(PEER_HEADS * 2, N_KEYS, PEER_KEY_DIM // 2), eye)
        skt = skt.reshape(PEER_HEADS * 2 * N_KEYS, PEER_HEADS * PEER_KEY_DIM).astype(BF16)
        u_bf = peer_u[l].astype(BF16)
        vt_bf = peer_v[l].T.astype(BF16)
        lq1 = lambda_q1[l].reshape(1, HEAD_DIM)
        lk1 = lambda_k1[l].reshape(1, HEAD_DIM)
        lq2 = lambda_q2[l].reshape(1, HEAD_DIM)
        lk2 = lambda_k2[l].reshape(1, HEAD_DIM)
        sub = subln[l].reshape(1, V_DIM)
        npre_mix = norm_pre_mix[l].reshape(1, D_MODEL)
        npost_mix = norm_post_mix[l].reshape(1, D_MODEL)
        npre_ffn = norm_pre_ffn[l].reshape(1, D_MODEL)
        npost_ffn = norm_post_ffn[l].reshape(1, D_MODEL)

        sh1, sc1, g1, sh2, sc2, g2 = mod_p
        q2, k32, kb, v32, vb, u, ga, gs = _inproj(yp, npre_mix, sc1, sh1, cos_p, sin_p, w_ext, tm_p)
        ob = _prompt_attention(q2, kb, vb, lq1, lk1, lq2, lk2, sub, lam_init, tq)
        ysb, hre, him = _ssm_prompt(u, a_re, a_im, b_mat, c_mat, d_row, wg, tscan)
        x1, h2b, c1, e1, r2, e2 = _mix(yp, ob, ysb, ga, gs, g1, sc2, sh2, npost_mix, npre_ffn,
                                      wao, wso, wout, wq, skt, tm_p)
        yp = _peer(h2b, u_bf, vt_bf, c1, e1, r2, e2, x1, g2, npost_ffn, _tile(tp, 512), 512)
        outs[0].append(k32.reshape(1, tp, N_HEADS, 2 * HEAD_DIM))
        outs[1].append(v32.reshape(1, tp, N_HEADS, V_DIM))
        outs[2].append(hre.reshape(1, N_GROUPS, STATE_DIM))
        outs[3].append(him.reshape(1, N_GROUPS, STATE_DIM))

        sh1, sc1, g1, sh2, sc2, g2 = mod_s
        q2, k32, _, v32, _, u, ga, gs = _inproj(ys, npre_mix, sc1, sh1, cos_s, sin_s, w_ext, ns)
        q_rows = q2.reshape(2, nb, tn_new, N_HEADS, V_DIM).transpose(1, 3, 0, 2, 4)
        q_rows = q_rows.reshape(nb, N_HEADS * 2 * tn_new, V_DIM)
        k_new = jnp.pad(k32.reshape(nb, tn_new * N_HEADS, V_DIM),
                        ((0, 0), (0, (NEW_ROWS - tn_new) * N_HEADS), (0, 0)))
        v_new = jnp.pad(v32.reshape(nb, tn_new * N_HEADS, V_DIM),
                        ((0, 0), (0, (NEW_ROWS - tn_new) * N_HEADS), (0, 0)))
        ck = cache_k.reshape(depth * n_phys, PAGE_SIZE * N_HEADS, V_DIM)
        cv = cache_v.reshape(depth * n_phys, PAGE_SIZE * N_HEADS, V_DIM)
        o_s = _sample_attention(page_table + l * n_phys, q_rows, k_new, v_new, ck, cv,
                                lq1, lk1, lq2, lk2, sub, lam_init, tn_new, pages_per_step)
        ob = o_s.reshape(nb, N_HEADS, 2, tn_new, V_DIM)[:, :, 0].transpose(0, 2, 1, 3)
        ob = ob.reshape(ns, ATTN_WIDTH).astype(BF16)
        u_tb = u.reshape(nb, tn_new, SSM_WIDTH).transpose(1, 0, 2)
        y_tb, hre_s, him_s = _ssm_sample(u_tb, state_ssm_re[l].reshape(nb, N_STATE),
                                         state_ssm_im[l].reshape(nb, N_STATE),
                                         a_re, a_im, b_mat, c_mat, d_row, wg)
        ysb = y_tb.transpose(1, 0, 2).reshape(ns, SSM_WIDTH)
        x1, h2b, c1, e1, r2, e2 = _mix(ys, ob, ysb, ga, gs, g1, sc2, sh2, npost_mix, npre_ffn,
                                      wao, wso, wout, wq, skt, ns)
        ys = _peer(h2b, u_bf, vt_bf, c1, e1, r2, e2, x1, g2, npost_ffn, ns, 512)
        outs[4].append(k32.reshape(nb, tn_new, N_HEADS, 2 * HEAD_DIM))
        outs[5].append(v32.reshape(nb, tn_new, N_HEADS, V_DIM))
        outs[6].append(hre_s.reshape(nb, N_GROUPS, STATE_DIM))
        outs[7].append(him_s.reshape(nb, N_GROUPS, STATE_DIM))

    st = [jnp.stack(o) for o in outs]
    return (yp.reshape(bp, tp, D_MODEL), ys.reshape(nb, tn_new, D_MODEL),
            st[0], st[1], st[2], st[3], st[4], st[5], st[6], st[7])
```

```python
import functools
import math

import jax
import jax.numpy as jnp
from jax import lax
from jax.experimental import pallas as pl
from jax.experimental.pallas import tpu as pltpu

F32 = jnp.float32
BF16 = jnp.bfloat16

D_MODEL = 1024
N_HEADS = 8
HEAD_DIM = 64
V_DIM = 2 * HEAD_DIM
QK_WIDTH = N_HEADS * 2 * HEAD_DIM
ATTN_WIDTH = N_HEADS * V_DIM
ROPE_THETA = 10000.0
PAGE_SIZE = 128
SSM_WIDTH = 512
SSM_GROUP = 16
N_GROUPS = SSM_WIDTH // SSM_GROUP
STATE_DIM = 64
N_STATE = N_GROUPS * STATE_DIM
SSM_CHUNKS = 4
SSM_CHUNK_IN = SSM_WIDTH // SSM_CHUNKS
SSM_CHUNK_ST = N_STATE // SSM_CHUNKS
SCAN_ROWS = 8
N_KEYS = 128
N_EXPERTS = N_KEYS * N_KEYS
PEER_HEADS = 8
PEER_KEY_DIM = 128
PEER_TOPK = 16
N_MOD = 6
RMS_EPS = 1e-6

VMEM_LIMIT_BYTES = 56 * 1024 * 1024

NEG_INF = float("-inf")
Q_SCALE = (HEAD_DIM ** -0.5) * math.log2(math.e)
NEW_ROWS = 16


def _cparams(*sem):
    return pltpu.CompilerParams(dimension_semantics=sem, vmem_limit_bytes=VMEM_LIMIT_BYTES)


def _dot(a, b):
    return jnp.dot(a, b, preferred_element_type=F32)


def _dot_nt(a, b):
    return lax.dot_general(a, b, (((1,), (1,)), ((), ())), preferred_element_type=F32)


def _rms(x, w):
    return x * lax.rsqrt(jnp.mean(x * x, axis=-1, keepdims=True) + RMS_EPS) * w


def _gelu(x):
    return 0.5 * x * (1.0 + jnp.tanh(0.7978845608028654 * (x + 0.044715 * (x * x * x))))


def _sigmoid(x):
    return 1.0 / (1.0 + jnp.exp(-x))


def _mod_spec(rows_per_mod, tm):
    if rows_per_mod == 1:
        return pl.BlockSpec((1, D_MODEL), lambda i: (0, 0))
    return pl.BlockSpec((tm, D_MODEL), lambda i: (i, 0))


def _ada_kernel(c_ref, w_ref, b_ref, o_ref):
    c = c_ref[...]
    s = c * _sigmoid(c)
    o_ref[...] = _dot(s.astype(BF16), w_ref[...].astype(BF16)) + b_ref[...]


def _ada(c_all, w_ada, b_ada):
    rows = c_all.shape[0]
    tn = 1536
    return pl.pallas_call(
        _ada_kernel,
        grid=(N_MOD * D_MODEL // tn,),
        in_specs=[pl.BlockSpec((rows, D_MODEL), lambda j: (0, 0)),
                  pl.BlockSpec((D_MODEL, tn), lambda j: (0, j)),
                  pl.BlockSpec((1, tn), lambda j: (0, j))],
        out_specs=pl.BlockSpec((rows, tn), lambda j: (0, j)),
        out_shape=jax.ShapeDtypeStruct((rows, N_MOD * D_MODEL), F32),
        compiler_params=_cparams("arbitrary"),
        name="ada_mod",
    )(c_all, w_ada, b_ada.reshape(1, -1))


def _inproj_kernel(x_ref, nw_ref, sc_ref, sh_ref, cos_ref, sin_ref, w_ref,
                   q2_ref, k32_ref, kb_ref, v32_ref, vb_ref, u_ref, ga_ref, gs_ref):
    x = x_ref[...]
    h = _rms(x, nw_ref[...]) * (1.0 + sc_ref[...]) + sh_ref[...]
    hb = h.astype(BF16)
    cos = cos_ref[...]
    sin = sin_ref[...]
    lane = lax.broadcasted_iota(jnp.int32, cos.shape, 1)
    first = lane < HEAD_DIM
    low = lane % HEAD_DIM < HEAD_DIM // 2
    W = QK_WIDTH

    def rotate(x):
        partner = jnp.where(low, pltpu.roll(x, V_DIM - HEAD_DIM // 2, 1), pltpu.roll(x, HEAD_DIM // 2, 1))
        return x * cos + partner * sin

    for hd in range(N_HEADS):
        c0 = hd * V_DIM
        qr = rotate(_dot(hb, w_ref[:, c0:c0 + V_DIM])) * Q_SCALE
        q2_ref[0, :, c0:c0 + V_DIM] = jnp.where(first, qr, 0.0).astype(BF16)
        q2_ref[1, :, c0:c0 + V_DIM] = jnp.where(first, 0.0, qr).astype(BF16)
        kr = rotate(_dot(hb, w_ref[:, W + c0:W + c0 + V_DIM]))
        k32_ref[:, c0:c0 + V_DIM] = kr
        kb_ref[:, c0:c0 + V_DIM] = kr.astype(BF16)
    o = 2 * W
    v = _dot(hb, w_ref[:, o:o + ATTN_WIDTH])
    v32_ref[...] = v
    vb_ref[...] = v.astype(BF16)
    o += ATTN_WIDTH
    u_ref[...] = _dot(hb, w_ref[:, o:o + SSM_WIDTH])
    o += SSM_WIDTH
    ga_ref[...] = _dot(hb, w_ref[:, o:o + D_MODEL])
    o += D_MODEL
    gs_ref[...] = _dot(hb, w_ref[:, o:o + D_MODEL])


def _inproj(x, nw, sc, sh, cosf, sinf, w_ext, tm):
    n = x.shape[0]
    wcols = w_ext.shape[1]
    row = lambda i: (i, 0)
    full = lambda i: (0, 0)
    outs = (
        jax.ShapeDtypeStruct((2, n, QK_WIDTH), BF16),
        jax.ShapeDtypeStruct((n, QK_WIDTH), F32),
        jax.ShapeDtypeStruct((n, QK_WIDTH), BF16),
        jax.ShapeDtypeStruct((n, ATTN_WIDTH), F32),
        jax.ShapeDtypeStruct((n, ATTN_WIDTH), BF16),
        jax.ShapeDtypeStruct((n, SSM_WIDTH), F32),
        jax.ShapeDtypeStruct((n, D_MODEL), F32),
        jax.ShapeDtypeStruct((n, D_MODEL), F32),
    )
    out_specs = (
        pl.BlockSpec((2, tm, QK_WIDTH), lambda i: (0, i, 0)),
        pl.BlockSpec((tm, QK_WIDTH), row),
        pl.BlockSpec((tm, QK_WIDTH), row),
        pl.BlockSpec((tm, ATTN_WIDTH), row),
        pl.BlockSpec((tm, ATTN_WIDTH), row),
        pl.BlockSpec((tm, SSM_WIDTH), row),
        pl.BlockSpec((tm, D_MODEL), row),
        pl.BlockSpec((tm, D_MODEL), row),
    )
    return pl.pallas_call(
        _inproj_kernel,
        grid=(n // tm,),
        in_specs=[pl.BlockSpec((tm, D_MODEL), row),
                  pl.BlockSpec((1, D_MODEL), full),
                  _mod_spec(sc.shape[0], tm),
                  _mod_spec(sh.shape[0], tm),
                  pl.BlockSpec((tm, V_DIM), row),
                  pl.BlockSpec((tm, V_DIM), row),
                  pl.BlockSpec((D_MODEL, wcols), full)],
        out_specs=out_specs,
        out_shape=outs,
        compiler_params=_cparams("arbitrary"),
        name="in_proj",
    )(x, nw, sc, sh, cosf, sinf, w_ext)


def _lambda_full(lq1_ref, lk1_ref, lq2_ref, lk2_ref, lam_init):
    s1 = jnp.sum(lq1_ref[...] * lk1_ref[...], axis=-1, keepdims=True)
    s2 = jnp.sum(lq2_ref[...] * lk2_ref[...], axis=-1, keepdims=True)
    return jnp.exp(s1) - jnp.exp(s2) + lam_init


def _subln(o, w, lam_init):
    return _rms(o, w) * (1.0 - lam_init)


def _pattn_kernel(q_ref, k_ref, v_ref, lq1_ref, lk1_ref, lq2_ref, lk2_ref, sub_ref, o_ref,
                  s_ref, m_ref, acc_ref, *, tq, lam_init):
    i = pl.program_id(1)
    m_ref[...] = jnp.full(m_ref.shape, NEG_INF, F32)
    acc_ref[...] = jnp.zeros(acc_ref.shape, F32)
    ones = jnp.ones((tq, V_DIM), BF16)

    def produce(comp, chunk):
        start = pl.multiple_of(chunk * tq, tq)
        s_ref[comp] = _dot_nt(q_ref[comp], k_ref[pl.ds(start, tq), :])

    def consume(comp, chunk, masked):
        start = pl.multiple_of(chunk * tq, tq)
        v1 = jnp.concatenate([v_ref[pl.ds(start, tq), :], ones], axis=1)
        s = s_ref[comp]
        if masked:
            r = lax.broadcasted_iota(jnp.int32, (tq, tq), 0)
            c = lax.broadcasted_iota(jnp.int32, (tq, tq), 1)
            s = jnp.where(c <= r, s, NEG_INF)
        m_old = m_ref[comp]
        m_new = jnp.maximum(m_old, jnp.max(s, axis=-1, keepdims=True))
        p = jnp.exp2(s - jnp.tile(m_new, (1, tq // V_DIM)))
        alpha = jnp.exp2(m_old - m_new)
        acc_ref[comp] = jnp.tile(alpha, (1, 2)) * acc_ref[comp] + _dot(p.astype(BF16), v1)
        m_ref[comp] = m_new

    def pair(chunk, last_next):
        produce(1, chunk)
        consume(0, chunk, False)
        produce(0, last_next)
        consume(1, chunk, False)

    def body(t, carry):
        for w in range(4):
            pair(4 * t + w, 4 * t + w + 1)
        return carry

    produce(0, 0)
    lax.fori_loop(0, i // 4, body, 0)
    done = (i // 4) * 4

    @pl.when(i % 4 >= 2)
    def _():
        pair(done, done + 1)
        pair(done + 1, done + 2)

    @pl.when(i % 2 == 1)
    def _():
        pair(i - 1, i)

    produce(1, i)
    consume(0, i, True)
    consume(1, i, True)

    lam = _lambda_full(lq1_ref, lk1_ref, lq2_ref, lk2_ref, lam_init)
    o1 = acc_ref[0, :, :V_DIM] / acc_ref[0, :, V_DIM:]
    o2 = acc_ref[1, :, :V_DIM] / acc_ref[1, :, V_DIM:]
    o_ref[...] = _subln(o1 - lam * o2, sub_ref[...], lam_init).astype(BF16)


def _prompt_attention(q2, kb, vb, lq1, lk1, lq2, lk2, subln, lam_init, tq):
    t = kb.shape[0]
    vec = lambda h, i: (0, 0)
    kern = functools.partial(_pattn_kernel, tq=tq, lam_init=lam_init)
    return pl.pallas_call(
        kern,
        grid=(N_HEADS, t // tq),
        in_specs=[pl.BlockSpec((2, tq, V_DIM), lambda h, i: (0, i, h)),
                  pl.BlockSpec((t, V_DIM), lambda h, i: (0, h)),
                  pl.BlockSpec((t, V_DIM), lambda h, i: (0, h)),
                  pl.BlockSpec((1, HEAD_DIM), vec),
                  pl.BlockSpec((1, HEAD_DIM), vec),
                  pl.BlockSpec((1, HEAD_DIM), vec),
                  pl.BlockSpec((1, HEAD_DIM), vec),
                  pl.BlockSpec((1, V_DIM), vec)],
        out_specs=pl.BlockSpec((tq, V_DIM), lambda h, i: (i, h)),
        out_shape=jax.ShapeDtypeStruct((t, ATTN_WIDTH), BF16),
        scratch_shapes=[pltpu.VMEM((2, tq, tq), F32),
                        pltpu.VMEM((2, tq, V_DIM), F32),
                        pltpu.VMEM((2, tq, 2 * V_DIM), F32)],
        compiler_params=_cparams("arbitrary", "arbitrary"),
        name="prompt_attn",
    )(q2, kb, vb, lq1, lk1, lq2, lk2, subln)


def _sattn_kernel(pt_ref, q_ref, kn_ref, vn_ref, lq1_ref, lk1_ref, lq2_ref, lk2_ref, sub_ref, *rest,
                  pages_per_step, n_new, lam_init):
    g_pages = pages_per_step
    k_refs = rest[:g_pages]
    v_refs = rest[g_pages:2 * g_pages]
    o_ref = rest[2 * g_pages]
    m_ref, l_ref, acc_ref, bias_ref = rest[2 * g_pages + 1:]
    b = pl.program_id(0)
    g = pl.program_id(1)
    q = q_ref[0]
    rows_per_head = 2 * n_new

    @pl.when((b == 0) & (g == 0))
    def _():
        r = lax.broadcasted_iota(jnp.int32, bias_ref.shape, 0) // rows_per_head
        c = lax.broadcasted_iota(jnp.int32, bias_ref.shape, 1) % N_HEADS
        bias_ref[...] = jnp.where(r == c, 0.0, NEG_INF)

    @pl.when(g == 0)
    def _():
        m_ref[...] = jnp.full(m_ref.shape, NEG_INF, F32)
        l_ref[...] = jnp.zeros(l_ref.shape, F32)
        acc_ref[...] = jnp.zeros(acc_ref.shape, F32)

    def update(s, v):
        m_old = m_ref[...]
        m_new = jnp.maximum(m_old, jnp.max(s, axis=-1, keepdims=True))
        p = jnp.exp2(s - m_new)
        alpha = jnp.exp2(m_old - m_new)
        l_ref[...] = alpha * l_ref[...] + jnp.sum(p, axis=-1, keepdims=True)
        acc_ref[...] = alpha * acc_ref[...] + _dot(p.astype(BF16), v)
        m_ref[...] = m_new

    half = g_pages // 2
    scores = []
    for grp in range(2):
        k = jnp.concatenate([r[0].astype(BF16) for r in k_refs[grp * half:(grp + 1) * half]], axis=0)
        scores.append(_dot_nt(q, k))
    for grp in range(2):
        v = jnp.concatenate([r[0].astype(BF16) for r in v_refs[grp * half:(grp + 1) * half]], axis=0)
        update(scores[grp] + bias_ref[...], v)

    @pl.when(g == pl.num_programs(1) - 1)
    def _():
        kn = kn_ref[0].astype(BF16)
        vn = vn_ref[0].astype(BF16)
        s = _dot_nt(q, kn)
        row = lax.broadcasted_iota(jnp.int32, s.shape, 0)
        col = lax.broadcasted_iota(jnp.int32, s.shape, 1)
        keep = (row // rows_per_head == col % N_HEADS) & (col // N_HEADS <= row % n_new)
        update(jnp.where(keep, s, NEG_INF), vn)
        lam = _lambda_full(lq1_ref, lk1_ref, lq2_ref, lk2_ref, lam_init)
        o = acc_ref[...] / l_ref[...]
        d = o - lam * pltpu.roll(o, o.shape[0] - n_new, 0)
        o_ref[0] = _subln(d, sub_ref[...], lam_init)


def _sample_attention(page_table, q_rows, k_new, v_new, cache_k, cache_v, lq1, lk1, lq2, lk2, subln,
                      lam_init, n_new, pages_per_step):
    nb, n_pages = page_table.shape
    gp = pages_per_step
    assert gp % 2 == 0
    rows = q_rows.shape[1]
    page_rows = PAGE_SIZE * N_HEADS
    new_rows = k_new.shape[1]
    vec = lambda b, g, pt: (0, 0)

    def page_spec(r):
        return pl.BlockSpec((1, page_rows, V_DIM),
                            lambda b, g, pt: (pt[b * n_pages + g * gp + r], 0, 0))

    kern = functools.partial(_sattn_kernel, pages_per_step=gp, n_new=n_new, lam_init=lam_init)
    grid_spec = pltpu.PrefetchScalarGridSpec(
        num_scalar_prefetch=1,
        grid=(nb, n_pages // gp),
        in_specs=[pl.BlockSpec((1, rows, V_DIM), lambda b, g, pt: (b, 0, 0)),
                  pl.BlockSpec((1, new_rows, V_DIM), lambda b, g, pt: (b, 0, 0)),
                  pl.BlockSpec((1, new_rows, V_DIM), lambda b, g, pt: (b, 0, 0)),
                  pl.BlockSpec((1, HEAD_DIM), vec),
                  pl.BlockSpec((1, HEAD_DIM), vec),
                  pl.BlockSpec((1, HEAD_DIM), vec),
                  pl.BlockSpec((1, HEAD_DIM), vec),
                  pl.BlockSpec((1, V_DIM), vec)]
                 + [page_spec(r) for r in range(gp)] + [page_spec(r) for r in range(gp)],
        out_specs=pl.BlockSpec((1, rows, V_DIM), lambda b, g, pt: (b, 0, 0)),
        scratch_shapes=[pltpu.VMEM((rows, 1), F32),
                        pltpu.VMEM((rows, 1), F32),
                        pltpu.VMEM((rows, V_DIM), F32),
                        pltpu.VMEM((rows, (gp // 2) * page_rows), F32)],
    )
    return pl.pallas_call(
        kern,
        grid_spec=grid_spec,
        out_shape=jax.ShapeDtypeStruct((nb, rows, V_DIM), F32),
        compiler_params=_cparams("arbitrary", "arbitrary"),
        name="sample_attn",
    )(page_table.reshape(-1), q_rows, k_new, v_new, lq1, lk1, lq2, lk2, subln,
      *([cache_k] * gp), *([cache_v] * gp))


def _ssm_disc_kernel(lr_ref, li_ref, ls_ref, are_ref, aim_ref, fre_ref, fim_ref, *, n_pow):
    lr = lr_ref[...]
    li = li_ref[...]
    dt = jnp.exp(ls_ref[...])
    mag = jnp.exp(lr * dt)
    a_re = mag * jnp.cos(li * dt)
    a_im = mag * jnp.sin(li * dt)
    den = lr * lr + li * li
    nr = a_re - 1.0
    ni = a_im
    fre_ref[...] = (nr * lr + ni * li) / den
    fim_ref[...] = (ni * lr - nr * li) / den
    pr, pi = a_re, a_im
    for kk in range(n_pow):
        are_ref[kk] = pr
        aim_ref[kk] = pi
        pr, pi = pr * a_re - pi * a_im, pr * a_im + pi * a_re


def _ssm_bb_kernel(fre_ref, fim_ref, bre_ref, bim_ref, ore_ref, oim_ref):
    fr = fre_ref[...]
    fi = fim_ref[...]
    br = bre_ref[...]
    bi = bim_ref[...]
    ore_ref[...] = fr * br - fi * bi
    oim_ref[...] = fr * bi + fi * br


def _ssm_prepare(lam_re, lam_im, log_step, b_re, b_im, c_re, c_im, n_pow):
    gshape = jax.ShapeDtypeStruct((N_GROUPS, STATE_DIM), F32)
    pshape = jax.ShapeDtypeStruct((n_pow, N_GROUPS, STATE_DIM), F32)
    a_re, a_im, f_re, f_im = pl.pallas_call(
        functools.partial(_ssm_disc_kernel, n_pow=n_pow),
        out_shape=(pshape, pshape, gshape, gshape),
        name="ssm_discretise",
    )(lam_re, lam_im, log_step.reshape(N_GROUPS, 1))
    bshape = jax.ShapeDtypeStruct((N_STATE, SSM_GROUP), F32)
    bb_re, bb_im = pl.pallas_call(
        _ssm_bb_kernel, out_shape=(bshape, bshape), name="ssm_input_matrix",
    )(f_re.reshape(N_STATE, 1), f_im.reshape(N_STATE, 1),
      b_re.reshape(N_STATE, SSM_GROUP), b_im.reshape(N_STATE, SSM_GROUP))
    gc = N_GROUPS // SSM_CHUNKS
    eye = jnp.eye(gc, dtype=F32)

    def in_mat(bb):
        b4 = bb.reshape(SSM_CHUNKS, gc, STATE_DIM, SSM_GROUP)
        m = jnp.einsum('cgpi,gh->cgihp', b4, eye)
        return m.reshape(SSM_CHUNKS, gc * SSM_GROUP, gc * STATE_DIM)

    def out_mat(cc):
        c4 = cc.reshape(SSM_CHUNKS, gc, SSM_GROUP, STATE_DIM)
        m = jnp.einsum('cgip,gh->cgphi', c4, eye)
        return m.reshape(SSM_CHUNKS, gc * STATE_DIM, gc * SSM_GROUP)

    b_mat = jnp.concatenate([in_mat(bb_re), in_mat(bb_im)], axis=-1).astype(BF16)
    c_mat = jnp.concatenate([out_mat(c_re), out_mat(-c_im)], axis=-2).astype(BF16)
    return (a_re.reshape(n_pow, N_STATE), a_im.reshape(n_pow, N_STATE), b_mat, c_mat)


def _shift_rows(x, s, row):
    return jnp.where(row >= s, pltpu.roll(x, s, 0), 0.0)


def _ssm_scan_kernel(u_ref, are_ref, aim_ref, b_ref, c_ref, d_ref, wg_ref,
                     y_ref, hre_ref, him_ref, cre_ref, cim_ref, *, tm):
    @pl.when(pl.program_id(0) == 0)
    def _():
        cre_ref[...] = jnp.zeros(cre_ref.shape, F32)
        cim_ref[...] = jnp.zeros(cim_ref.shape, F32)

    u = u_ref[...]
    ub = u.astype(BF16)
    seg = tm // SCAN_ROWS
    pr_i = lax.broadcasted_iota(jnp.int32, (tm, tm), 0)
    pc_i = lax.broadcasted_iota(jnp.int32, (tm, tm), 1)
    perm = (pc_i == seg * (pr_i % SCAN_ROWS) + pr_i // SCAN_ROWS).astype(BF16)
    perm_t = (pr_i == seg * (pc_i % SCAN_ROWS) + pc_i // SCAN_ROWS).astype(BF16)
    ubp = _dot(perm, ub).astype(BF16)
    row = lax.broadcasted_iota(jnp.int32, (SCAN_ROWS, SSM_CHUNK_ST), 0)
    ys = []
    for c in range(SSM_CHUNKS):
        lanes = slice(c * SSM_CHUNK_ST, (c + 1) * SSM_CHUNK_ST)
        bu = _dot(ubp[:, c * SSM_CHUNK_IN:(c + 1) * SSM_CHUNK_IN], b_ref[c])
        a_r = are_ref[0:1, lanes]
        a_i = aim_ref[0:1, lanes]
        loc_r, loc_i = [], []
        x_r = x_i = None
        for gi in range(seg):
            rows = slice(gi * SCAN_ROWS, (gi + 1) * SCAN_ROWS)
            b_r = bu[rows, :SSM_CHUNK_ST]
            b_i = bu[rows, SSM_CHUNK_ST:]
            if gi == 0:
                x_r, x_i = b_r, b_i
            else:
                x_r, x_i = b_r + (a_r * x_r - a_i * x_i), b_i + (a_r * x_i + a_i * x_r)
            loc_r.append(x_r)
            loc_i.append(x_i)
        e_r, e_i = x_r, x_i
        p_r = are_ref[seg - 1:seg, lanes]
        p_i = aim_ref[seg - 1:seg, lanes]
        h_r = cre_ref[0:1, lanes]
        h_i = cim_ref[0:1, lanes]
        first = row == 0
        e_r = e_r + jnp.where(first, p_r * h_r - p_i * h_i, 0.0)
        e_i = e_i + jnp.where(first, p_r * h_i + p_i * h_r, 0.0)
        s = 1
        while s < SCAN_ROWS:
            sr = _shift_rows(e_r, s, row)
            si = _shift_rows(e_i, s, row)
            e_r, e_i = e_r + (p_r * sr - p_i * si), e_i + (p_r * si + p_i * sr)
            p_r, p_i = p_r * p_r - p_i * p_i, 2.0 * p_r * p_i
            s *= 2
        cre_ref[0:1, lanes] = e_r[SCAN_ROWS - 1:SCAN_ROWS]
        cim_ref[0:1, lanes] = e_i[SCAN_ROWS - 1:SCAN_ROWS]
        in_r = jnp.where(first, h_r, pltpu.roll(e_r, 1, 0))
        in_i = jnp.where(first, h_i, pltpu.roll(e_i, 1, 0))
        out_r, out_i = [], []
        for gi in range(seg):
            q_r = are_ref[gi:gi + 1, lanes]
            q_i = aim_ref[gi:gi + 1, lanes]
            out_r.append(loc_r[gi] + (q_r * in_r - q_i * in_i))
            out_i.append(loc_i[gi] + (q_r * in_i + q_i * in_r))
        hcat_p = jnp.concatenate([jnp.concatenate(out_r, axis=0), jnp.concatenate(out_i, axis=0)],
                                 axis=1).astype(BF16)
        hcat = _dot(perm_t, hcat_p).astype(BF16)
        ys.append(_dot(hcat, c_ref[c]))
    y = jnp.concatenate(ys, axis=1) + d_ref[...] * u
    y = _gelu(y)
    y = y * _sigmoid(_dot(y.astype(BF16), wg_ref[...]))
    y_ref[...] = y.astype(BF16)
    hre_ref[...] = cre_ref[0:1, :]
    him_ref[...] = cim_ref[0:1, :]


def _ssm_prompt(u, a_re, a_im, b_mat, c_mat, d, wg, tm):
    t = u.shape[0]
    n_pow = a_re.shape[0]
    full2 = lambda i: (0, 0)
    full3 = lambda i: (0, 0, 0)
    kern = functools.partial(_ssm_scan_kernel, tm=tm)
    return pl.pallas_call(
        kern,
        grid=(t // tm,),
        in_specs=[pl.BlockSpec((tm, SSM_WIDTH), lambda i: (i, 0)),
                  pl.BlockSpec((n_pow, N_STATE), full2),
                  pl.BlockSpec((n_pow, N_STATE), full2),
                  pl.BlockSpec(b_mat.shape, full3),
                  pl.BlockSpec(c_mat.shape, full3),
                  pl.BlockSpec((1, SSM_WIDTH), full2),
                  pl.BlockSpec((SSM_WIDTH, SSM_WIDTH), full2)],
        out_specs=(pl.BlockSpec((tm, SSM_WIDTH), lambda i: (i, 0)),
                   pl.BlockSpec((1, N_STATE), full2),
                   pl.BlockSpec((1, N_STATE), full2)),
        out_shape=(jax.ShapeDtypeStruct((t, SSM_WIDTH), BF16),
                   jax.ShapeDtypeStruct((1, N_STATE), F32),
                   jax.ShapeDtypeStruct((1, N_STATE), F32)),
        scratch_shapes=[pltpu.VMEM((8, N_STATE), F32), pltpu.VMEM((8, N_STATE), F32)],
        compiler_params=_cparams("arbitrary"),
        name="ssm_prompt",
    )(u, a_re, a_im, b_mat, c_mat, d, wg)


def _ssm_step_kernel(u_ref, h0re_ref, h0im_ref, are_ref, aim_ref, b_ref, c_ref, d_ref, wg_ref,
                     y_ref, hre_ref, him_ref, *, n_steps):
    a_r = are_ref[0:1, :]
    a_i = aim_ref[0:1, :]
    h_r = h0re_ref[...]
    h_i = h0im_ref[...]
    for t in range(n_steps):
        u = u_ref[t]
        ub = u.astype(BF16)
        brs, bis = [], []
        for c in range(SSM_CHUNKS):
            bu = _dot(ub[:, c * SSM_CHUNK_IN:(c + 1) * SSM_CHUNK_IN], b_ref[c])
            brs.append(bu[:, :SSM_CHUNK_ST])
            bis.append(bu[:, SSM_CHUNK_ST:])
        bu_r = jnp.concatenate(brs, axis=1)
        bu_i = jnp.concatenate(bis, axis=1)
        h_r, h_i = a_r * h_r - a_i * h_i + bu_r, a_r * h_i + a_i * h_r + bu_i
        ys = []
        for c in range(SSM_CHUNKS):
            lanes = slice(c * SSM_CHUNK_ST, (c + 1) * SSM_CHUNK_ST)
            hcat = jnp.concatenate([h_r[:, lanes], h_i[:, lanes]], axis=1).astype(BF16)
            ys.append(_dot(hcat, c_ref[c]))
        y = jnp.concatenate(ys, axis=1) + d_ref[...] * u
        y = _gelu(y)
        y = y * _sigmoid(_dot(y.astype(BF16), wg_ref[...]))
        y_ref[t] = y.astype(BF16)
    hre_ref[...] = h_r
    him_ref[...] = h_i


def _ssm_sample(u_tb, h0_re, h0_im, a_re, a_im, b_mat, c_mat, d, wg):
    n_steps, nb, _ = u_tb.shape
    kern = functools.partial(_ssm_step_kernel, n_steps=n_steps)
    return pl.pallas_call(
        kern,
        out_shape=(jax.ShapeDtypeStruct((n_steps, nb, SSM_WIDTH), BF16),
                   jax.ShapeDtypeStruct((nb, N_STATE), F32),
                   jax.ShapeDtypeStruct((nb, N_STATE), F32)),
        compiler_params=pltpu.CompilerParams(vmem_limit_bytes=VMEM_LIMIT_BYTES),
        name="ssm_sample",
    )(u_tb, h0_re, h0_im, a_re, a_im, b_mat, c_mat, d, wg)


def _kth_largest_rows(c, k):
    cnt = jnp.zeros((1, c.shape[1]), F32)
    tau = jnp.full((1, c.shape[1]), NEG_INF, F32)
    for _ in range(k):
        m = jnp.max(c, axis=0, keepdims=True)
        eq = c == m
        tau = jnp.where(cnt < k, m, tau)
        cnt = cnt + jnp.sum(eq.astype(F32), axis=0, keepdims=True)
        c = jnp.where(eq, NEG_INF, c)
    return tau


def _top_rows(a, k, out_ref):
    for r in range(k):
        m = jnp.max(a, axis=0, keepdims=True)
        out_ref[r:r + 1, :] = m
        a = jnp.where(a == m, NEG_INF, a)


def _mix_kernel(x_ref, o_ref, ys_ref, ga_ref, gs_ref, g1_ref, sc2_ref, sh2_ref, npost_ref, npre_ref,
                wao_ref, wso_ref, wout_ref, wq_ref, skt_ref,
                x1_ref, h2_ref, c1_ref, e1_ref, r2_ref, e2_ref, v1_ref, v2_ref, cnt_ref):
    y_attn = _dot(o_ref[...], wao_ref[...])
    y_ssm = _dot(ys_ref[...], wso_ref[...])
    merged = _sigmoid(ga_ref[...]) * y_attn + _sigmoid(gs_ref[...]) * y_ssm
    mix = _dot(merged.astype(BF16), wout_ref[...])
    x1 = x_ref[...] + g1_ref[...] * _rms(mix, npost_ref[...])
    x1_ref[...] = x1
    h2 = _rms(x1, npre_ref[...]) * (1.0 + sc2_ref[...]) + sh2_ref[...]
    h2b = h2.astype(BF16)
    h2_ref[...] = h2b
    qp = _dot(h2b, wq_ref[...]).astype(BF16)
    s_all = _dot_nt(skt_ref[...], qp)
    for hd in range(PEER_HEADS):
        r0 = hd * 2 * N_KEYS
        k0 = hd * N_KEYS
        s1 = s_all[r0:r0 + N_KEYS]
        s2 = s_all[r0 + N_KEYS:r0 + 2 * N_KEYS]
        _top_rows(s1, PEER_TOPK, v1_ref)
        _top_rows(s2, PEER_TOPK, v2_ref)
        v1lo = v1_ref[0:8, :]
        v2lo = v2_ref[0:8, :]
        r8 = lax.broadcasted_iota(jnp.int32, v1lo.shape, 0)
        m1 = v1_ref[0:1, :]
        m2 = v2_ref[0:1, :]
        pieces = [
            v1lo + m2,
            v1_ref[8:16, :] + m2,
            v1lo + v2_ref[1:2, :],
            jnp.where((r8 >= 2) & (r8 <= 4), v1lo + v2_ref[2:3, :], NEG_INF),
            jnp.where((r8 >= 2) & (r8 <= 3), v1lo + v2_ref[3:4, :], NEG_INF),
            jnp.where(r8 == 2, v1lo + v2_ref[4:5, :], NEG_INF),
            jnp.where(r8 >= 2, m1 + v2lo, NEG_INF),
            m1 + v2_ref[8:16, :],
            jnp.where(r8 >= 2, v1_ref[1:2, :] + v2lo, NEG_INF),
        ]
        cand = jnp.concatenate(pieces, axis=0)
        tau = _kth_largest_rows(cand, PEER_TOPK)
        z = jnp.sum(jnp.where(cand >= tau, jnp.exp(cand - (m1 + m2)), 0.0), axis=0, keepdims=True)
        sel = [(p >= tau).astype(F32) for p in pieces]
        n0 = jnp.sum(sel[6] + sel[7], axis=0, keepdims=True)
        n1 = jnp.sum(sel[8], axis=0, keepdims=True)
        cnt_ref[0:8, :] = (sel[0] + sel[2] + sel[3] + sel[4] + sel[5]
                           + jnp.where(r8 == 0, n0, 0.0) + jnp.where(r8 == 1, n1, 0.0))
        cnt_ref[8:16, :] = sel[1]
        c1 = jnp.zeros(s1.shape, F32)
        r2 = jnp.full(s2.shape, float(N_KEYS - 1), F32)
        for r in range(PEER_TOPK):
            c1 = jnp.where(s1 == v1_ref[r:r + 1, :], cnt_ref[r:r + 1, :], c1)
            r2 = jnp.where(s2 == v2_ref[r:r + 1, :], float(r), r2)
        c1_ref[k0:k0 + N_KEYS, :] = c1
        e1_ref[k0:k0 + N_KEYS, :] = jnp.exp(s1 - m1)
        r2_ref[k0:k0 + N_KEYS, :] = r2.astype(BF16)
        e2_ref[k0:k0 + N_KEYS, :] = (jnp.exp(s2 - m2) / z).astype(BF16)


def _mix(x, ob, ysb, ga, gs, g1, sc2, sh2, npost, npre, wao, wso, wout, wq, skt, tm):
    n = x.shape[0]
    row = lambda i: (i, 0)
    col = lambda i: (0, i)
    full = lambda i: (0, 0)
    nrow = 2 * N_KEYS * PEER_HEADS
    krow = N_KEYS * PEER_HEADS
    return pl.pallas_call(
        _mix_kernel,
        grid=(n // tm,),
        in_specs=[pl.BlockSpec((tm, D_MODEL), row),
                  pl.BlockSpec((tm, ATTN_WIDTH), row),
                  pl.BlockSpec((tm, SSM_WIDTH), row),
                  pl.BlockSpec((tm, D_MODEL), row),
                  pl.BlockSpec((tm, D_MODEL), row),
                  _mod_spec(g1.shape[0], tm),
                  _mod_spec(sc2.shape[0], tm),
                  _mod_spec(sh2.shape[0], tm),
                  pl.BlockSpec((1, D_MODEL), full),
                  pl.BlockSpec((1, D_MODEL), full),
                  pl.BlockSpec((ATTN_WIDTH, D_MODEL), full),
                  pl.BlockSpec((SSM_WIDTH, D_MODEL), full),
                  pl.BlockSpec((D_MODEL, D_MODEL), full),
                  pl.BlockSpec((D_MODEL, PEER_HEADS * PEER_KEY_DIM), full),
                  pl.BlockSpec((nrow, PEER_HEADS * PEER_KEY_DIM), full)],
        out_specs=(pl.BlockSpec((tm, D_MODEL), row),
                   pl.BlockSpec((tm, D_MODEL), row),
                   pl.BlockSpec((krow, tm), col),
                   pl.BlockSpec((krow, tm), col),
                   pl.BlockSpec((krow, tm), col),
                   pl.BlockSpec((krow, tm), col)),
        out_shape=(jax.ShapeDtypeStruct((n, D_MODEL), F32),
                   jax.ShapeDtypeStruct((n, D_MODEL), BF16),
                   jax.ShapeDtypeStruct((krow, n), F32),
                   jax.ShapeDtypeStruct((krow, n), F32),
                   jax.ShapeDtypeStruct((krow, n), BF16),
                   jax.ShapeDtypeStruct((krow, n), BF16)),
        scratch_shapes=[pltpu.VMEM((PEER_TOPK, tm), F32)] * 3,
        compiler_params=_cparams("arbitrary"),
        name="mix_route",
    )(x, ob, ysb, ga, gs, g1, sc2, sh2, npost, npre, wao, wso, wout, wq, skt)


BF16_ROWS = 16
PEER_CHUNKS_PER_STEP = 2


def _tile_bf16(row):
    return jnp.broadcast_to(row, (BF16_ROWS, row.shape[1])).astype(BF16)


def _peer_kernel(h2_ref, u0_ref, *rest, ec, n_sub):
    u_refs = rest[:n_sub]
    (vt_ref, c1_ref, e1_ref, r2_ref, e2_ref, x1_ref, g2_ref, nw_ref,
     y_ref, acc_ref, act0_ref, act1_ref) = rest[n_sub:]
    acts = (act0_ref, act1_ref)
    g = pl.program_id(1)
    h2 = h2_ref[...]
    blocks = ec // N_KEYS

    @pl.when(g == 0)
    def _():
        acc_ref[...] = jnp.zeros(acc_ref.shape, F32)
        act0_ref[...] = _dot_nt(u0_ref[...], h2)

    def consume(act_ref, chunk, col0):
        ws = []
        for ii in range(blocks):
            key1 = chunk * blocks + ii
            a = _gelu(act_ref[ii * N_KEYS:(ii + 1) * N_KEYS, :].astype(BF16))
            tiles = N_KEYS // BF16_ROWS
            totals = [None] * tiles
            for hd in range(PEER_HEADS):
                cnt = _tile_bf16(c1_ref[pl.ds(hd * N_KEYS + key1, 1), :])
                e1 = _tile_bf16(e1_ref[pl.ds(hd * N_KEYS + key1, 1), :])
                for t in range(tiles):
                    k0 = hd * N_KEYS + t * BF16_ROWS
                    r2 = r2_ref[k0:k0 + BF16_ROWS, :]
                    e2 = e2_ref[k0:k0 + BF16_ROWS, :]
                    term = jnp.where(r2 < cnt, e2, jnp.zeros_like(e2)) * e1
                    totals[t] = term if totals[t] is None else totals[t] + term
            for t in range(tiles):
                ws.append(totals[t] * a[t * BF16_ROWS:(t + 1) * BF16_ROWS])
        w = jnp.concatenate(ws, axis=0)
        acc_ref[...] += _dot(vt_ref[:, col0:col0 + ec], w)

    for s in range(n_sub):
        acts[(s + 1) % 2][...] = _dot_nt(u_refs[s][...], h2)
        consume(acts[s % 2], n_sub * g + s, s * ec)

    @pl.when(g == pl.num_programs(1) - 1)
    def _():
        f = acc_ref[...].T
        y_ref[...] = x1_ref[...] + g2_ref[...] * _rms(f, nw_ref[...])


def _peer(h2b, u_bf, vt_bf, c1, e1, r2, e2, x1, g2, nw, tn, ec):
    n = h2b.shape[0]
    krow = c1.shape[0]
    n_chunks = N_EXPERTS // ec
    n_sub = PEER_CHUNKS_PER_STEP
    kern = functools.partial(_peer_kernel, ec=ec, n_sub=n_sub)
    g2_spec = (pl.BlockSpec((1, D_MODEL), lambda t, c: (0, 0)) if g2.shape[0] == 1
               else pl.BlockSpec((tn, D_MODEL), lambda t, c: (t, 0)))

    def lookahead_spec(s):
        return pl.BlockSpec((ec, D_MODEL), lambda t, c: (jnp.minimum(n_sub * c + s + 1, n_chunks - 1), 0))

    return pl.pallas_call(
        kern,
        grid=(n // tn, n_chunks // n_sub),
        in_specs=[pl.BlockSpec((tn, D_MODEL), lambda t, c: (t, 0)),
                  pl.BlockSpec((ec, D_MODEL), lambda t, c: (0, 0))]
                 + [lookahead_spec(s) for s in range(n_sub)]
                 + [pl.BlockSpec((D_MODEL, n_sub * ec), lambda t, c: (0, c)),
                  pl.BlockSpec((krow, tn), lambda t, c: (0, t)),
                  pl.BlockSpec((krow, tn), lambda t, c: (0, t)),
                  pl.BlockSpec((krow, tn), lambda t, c: (0, t)),
                  pl.BlockSpec((krow, tn), lambda t, c: (0, t)),
                  pl.BlockSpec((tn, D_MODEL), lambda t, c: (t, 0)),
                  g2_spec,
                  pl.BlockSpec((1, D_MODEL), lambda t, c: (0, 0))],
        out_specs=pl.BlockSpec((tn, D_MODEL), lambda t, c: (t, 0)),
        out_shape=jax.ShapeDtypeStruct((n, D_MODEL), F32),
        scratch_shapes=[pltpu.VMEM((D_MODEL, tn), F32),
                        pltpu.VMEM((ec, tn), F32), pltpu.VMEM((ec, tn), F32)],
        compiler_params=_cparams("arbitrary", "arbitrary"),
        name="peer_experts",
    )(h2b, u_bf, *([u_bf] * n_sub), vt_bf, c1, e1, r2, e2, x1, g2, nw)


def _rope_tables(pos):
    half = HEAD_DIM // 2
    inv = ROPE_THETA ** (-jnp.arange(half, dtype=F32) / half)
    ang = pos.astype(F32)[:, None] * inv[None, :]
    cos = jnp.cos(ang)
    sin = jnp.sin(ang)
    cosf = jnp.concatenate([cos, cos, cos, cos], axis=1)
    sinf = jnp.concatenate([-sin, sin, -sin, sin], axis=1)
    return cosf, sinf


def _tile(n, pref):
    t = min(n, pref)
    assert n % t == 0, (n, pref)
    return t


def kernel(x_prompt, x_sample, cache_k, cache_v, state_ssm_re, state_ssm_im, page_table, c_prompt, c_sample, w_ada, b_ada, norm_pre_mix, norm_post_mix, norm_pre_ffn, norm_post_ffn, w_in, lambda_q1, lambda_k1, lambda_q2, lambda_k2, subln, w_attn_out, ssm_lambda_re, ssm_lambda_im, ssm_log_step, ssm_b_re, ssm_b_im, ssm_c_re, ssm_c_im, ssm_d, w_glu, w_ssm_out, w_out, peer_w_query, peer_sub_keys, peer_u, peer_v):
    depth = w_ada.shape[0]
    bp, tp, _ = x_prompt.shape
    nb, tn_new, _ = x_sample.shape
    assert bp == 1 and 2 * tn_new == 8
    n_pages = page_table.shape[1]
    past_len = n_pages * PAGE_SIZE
    n_phys = cache_k.shape[1]
    ns = nb * tn_new

    pos_p = jnp.arange(tp, dtype=jnp.int32)
    pos_s = jnp.tile(past_len + jnp.arange(tn_new, dtype=jnp.int32), nb)
    cos_p, sin_p = _rope_tables(pos_p)
    cos_s, sin_s = _rope_tables(pos_s)

    tm_p = _tile(tp, 256)
    tq = _tile(tp, 512)
    tscan = _tile(tp, 128)
    assert tscan % SCAN_ROWS == 0
    pages_per_step = next(p for p in (16, 8, 4, 2) if n_pages % p == 0)

    yp = x_prompt.reshape(tp, D_MODEL)
    ys = x_sample.reshape(ns, D_MODEL)
    c_all = jnp.concatenate([c_prompt, c_sample], axis=0)
    pad = (-c_all.shape[0]) % 8
    c_all = jnp.pad(c_all, ((0, pad), (0, 0)))

    outs = [[] for _ in range(8)]
    for l in range(depth):
        lam_init = 0.8 - 0.6 * math.exp(-0.3 * l)
        mod = _ada(c_all, w_ada[l], b_ada[l])
        mod_p = [mod[0:1, j * D_MODEL:(j + 1) * D_MODEL] for j in range(N_MOD)]
        mod_s = [jnp.repeat(mod[1:1 + nb, j * D_MODEL:(j + 1) * D_MODEL], tn_new, axis=0)
                 for j in range(N_MOD)]

        w_ext = w_in[l].astype(BF16)
        a_re, a_im, b_mat, c_mat = _ssm_prepare(ssm_lambda_re[l], ssm_lambda_im[l], ssm_log_step[l],
                                                ssm_b_re[l], ssm_b_im[l], ssm_c_re[l], ssm_c_im[l],
                                                tscan // SCAN_ROWS)
        d_row = ssm_d[l].reshape(1, SSM_WIDTH)
        wg = w_glu[l].astype(BF16)
        wao = w_attn_out[l].astype(BF16)
        wso = w_ssm_out[l].astype(BF16)
        wout = w_out[l].astype(BF16)
        wq = peer_w_query[l].astype(BF16)
        sk = peer_sub_keys[l]
        eye = jnp.eye(PEER_HEADS * 2, dtype=F32)
        skt = jnp.einsum('bkd,bc->bkcd', sk.reshape(PEER_HEADS * 2, N_KEYS, PEER_KEY_DIM // 2), eye)
        skt = skt.reshape(PEER_HEADS * 2 * N_KEYS, PEER_HEADS * PEER_KEY_DIM).astype(BF16)
        u_bf = peer_u[l].astype(BF16)
        vt_bf = peer_v[l].T.astype(BF16)
        lq1 = lambda_q1[l].reshape(1, HEAD_DIM)
        lk1 = lambda_k1[l].reshape(1, HEAD_DIM)
        lq2 = lambda_q2[l].reshape(1, HEAD_DIM)
        lk2 = lambda_k2[l].reshape(1, HEAD_DIM)
        sub = subln[l].reshape(1, V_DIM)
        npre_mix = norm_pre_mix[l].reshape(1, D_MODEL)
        npost_mix = norm_post_mix[l].reshape(1, D_MODEL)
        npre_ffn = norm_pre_ffn[l].reshape(1, D_MODEL)
        npost_ffn = norm_post_ffn[l].reshape(1, D_MODEL)

        sh1, sc1, g1, sh2, sc2, g2 = mod_p
        q2, k32, kb, v32, vb, u, ga, gs = _inproj(yp, npre_mix, sc1, sh1, cos_p, sin_p, w_ext, tm_p)
        ob = _prompt_attention(q2, kb, vb, lq1, lk1, lq2, lk2, sub, lam_init, tq)
        ysb, hre, him = _ssm_prompt(u, a_re, a_im, b_mat, c_mat, d_row, wg, tscan)
        x1, h2b, c1, e1, r2, e2 = _mix(yp, ob, ysb, ga, gs, g1, sc2, sh2, npost_mix, npre_ffn,
                                      wao, wso, wout, wq, skt, tm_p)
        yp = _peer(h2b, u_bf, vt_bf, c1, e1, r2, e2, x1, g2, npost_ffn, _tile(tp, 512), 512)
        outs[0].append(k32.reshape(1, tp, N_HEADS, 2 * HEAD_DIM))
        outs[1].append(v32.reshape(1, tp, N_HEADS, V_DIM))
        outs[2].append(hre.reshape(1, N_GROUPS, STATE_DIM))
        outs[3].append(him.reshape(1, N_GROUPS, STATE_DIM))

        sh1, sc1, g1, sh2, sc2, g2 = mod_s
        q2, k32, _, v32, _, u, ga, gs = _inproj(ys, npre_mix, sc1, sh1, cos_s, sin_s, w_ext, ns)
        q_rows = q2.reshape(2, nb, tn_new, N_HEADS, V_DIM).transpose(1, 3, 0, 2, 4)
        q_rows = q_rows.reshape(nb, N_HEADS * 2 * tn_new, V_DIM)
        k_new = jnp.pad(k32.reshape(nb, tn_new * N_HEADS, V_DIM),
                        ((0, 0), (0, (NEW_ROWS - tn_new) * N_HEADS), (0, 0)))
        v_new = jnp.pad(v32.reshape(nb, tn_new * N_HEADS, V_DIM),
                        ((0, 0), (0, (NEW_ROWS - tn_new) * N_HEADS), (0, 0)))
        ck = cache_k.reshape(depth * n_phys, PAGE_SIZE * N_HEADS, V_DIM)
        cv = cache_v.reshape(depth * n_phys, PAGE_SIZE * N_HEADS, V_DIM)
        o_s = _sample_attention(page_table + l * n_phys, q_rows, k_new, v_new, ck, cv,
                                lq1, lk1, lq2, lk2, sub, lam_init, tn_new, pages_per_step)
        ob = o_s.reshape(nb, N_HEADS, 2, tn_new, V_DIM)[:, :, 0].transpose(0, 2, 1, 3)
        ob = ob.reshape(ns, ATTN_WIDTH).astype(BF16)
        u_tb = u.reshape(nb, tn_new, SSM_WIDTH).transpose(1, 0, 2)
        y_tb, hre_s, him_s = _ssm_sample(u_tb, state_ssm_re[l].reshape(nb, N_STATE),
                                         state_ssm_im[l].reshape(nb, N_STATE),
                                         a_re, a_im, b_mat, c_mat, d_row, wg)
        ysb = y_tb.transpose(1, 0, 2).reshape(ns, SSM_WIDTH)
        x1, h2b, c1, e1, r2, e2 = _mix(ys, ob, ysb, ga, gs, g1, sc2, sh2, npost_mix, npre_ffn,
                                      wao, wso, wout, wq, skt, ns)
        ys = _peer(h2b, u_bf, vt_bf, c1, e1, r2, e2, x1, g2, npost_ffn, ns, 512)
        outs[4].append(k32.reshape(nb, tn_new, N_HEADS, 2 * HEAD_DIM))
        outs[5].append(v32.reshape(nb, tn_new, N_HEADS, V_DIM))
        outs[6].append(hre_s.reshape(nb, N_GROUPS, STATE_DIM))
        outs[7].append(him_s.reshape(nb, N_GROUPS, STATE_DIM))

    st = [jnp.stack(o) for o in outs]
    return (yp.reshape(bp, tp, D_MODEL), ys.reshape(nb, tn_new, D_MODEL),
            st[0], st[1], st[2], st[3], st[4], st[5], st[6], st[7])
```

```python
import functools
import math

import jax
import jax.numpy as jnp
from jax import lax
from jax.experimental import pallas as pl
from jax.experimental.pallas import tpu as pltpu

F32 = jnp.float32
BF16 = jnp.bfloat16

D_MODEL = 1024
N_HEADS = 8
HEAD_DIM = 64
V_DIM = 2 * HEAD_DIM
QK_WIDTH = N_HEADS * 2 * HEAD_DIM
ATTN_WIDTH = N_HEADS * V_DIM
ROPE_THETA = 10000.0
PAGE_SIZE = 128
SSM_WIDTH = 512
SSM_GROUP = 16
N_GROUPS = SSM_WIDTH // SSM_GROUP
STATE_DIM = 64
N_STATE = N_GROUPS * STATE_DIM
SSM_CHUNKS = 4
SSM_CHUNK_IN = SSM_WIDTH // SSM_CHUNKS
SSM_CHUNK_ST = N_STATE // SSM_CHUNKS
SCAN_ROWS = 8
N_KEYS = 128
N_EXPERTS = N_KEYS * N_KEYS
PEER_HEADS = 8
PEER_KEY_DIM = 128
PEER_TOPK = 16
N_MOD = 6
RMS_EPS = 1e-6

VMEM_LIMIT_BYTES = 56 * 1024 * 1024

NEG_INF = float("-inf")
Q_SCALE = (HEAD_DIM ** -0.5) * math.log2(math.e)
NEW_ROWS = 16
ATTN_UNROLL = 8


def _cparams(*sem):
    return pltpu.CompilerParams(dimension_semantics=sem, vmem_limit_bytes=VMEM_LIMIT_BYTES)


def _dot(a, b):
    return jnp.dot(a, b, preferred_element_type=F32)


def _dot_nt(a, b):
    return lax.dot_general(a, b, (((1,), (1,)), ((), ())), preferred_element_type=F32)


def _rms(x, w):
    return x * lax.rsqrt(jnp.mean(x * x, axis=-1, keepdims=True) + RMS_EPS) * w


def _gelu(x):
    return 0.5 * x * (1.0 + jnp.tanh(0.7978845608028654 * (x + 0.044715 * (x * x * x))))


def _sigmoid(x):
    return 1.0 / (1.0 + jnp.exp(-x))


def _mod_spec(rows_per_mod, tm):
    if rows_per_mod == 1:
        return pl.BlockSpec((1, D_MODEL), lambda i: (0, 0))
    return pl.BlockSpec((tm, D_MODEL), lambda i: (i, 0))


def _ada_kernel(c_ref, w_ref, b_ref, o_ref):
    c = c_ref[...]
    s = c * _sigmoid(c)
    o_ref[...] = _dot(s.astype(BF16), w_ref[...].astype(BF16)) + b_ref[...]


def _ada(c_all, w_ada, b_ada):
    rows = c_all.shape[0]
    tn = 1536
    return pl.pallas_call(
        _ada_kernel,
        grid=(N_MOD * D_MODEL // tn,),
        in_specs=[pl.BlockSpec((rows, D_MODEL), lambda j: (0, 0)),
                  pl.BlockSpec((D_MODEL, tn), lambda j: (0, j)),
                  pl.BlockSpec((1, tn), lambda j: (0, j))],
        out_specs=pl.BlockSpec((rows, tn), lambda j: (0, j)),
        out_shape=jax.ShapeDtypeStruct((rows, N_MOD * D_MODEL), F32),
        compiler_params=_cparams("arbitrary"),
        name="ada_mod",
    )(c_all, w_ada, b_ada.reshape(1, -1))


def _inproj_kernel(x_ref, nw_ref, sc_ref, sh_ref, cos_ref, sin_ref, w_ref,
                   q2_ref, k32_ref, kb_ref, v32_ref, vb_ref, u_ref, ga_ref, gs_ref):
    x = x_ref[...]
    h = _rms(x, nw_ref[...]) * (1.0 + sc_ref[...]) + sh_ref[...]
    hb = h.astype(BF16)
    cos = cos_ref[...]
    sin = sin_ref[...]
    lane = lax.broadcasted_iota(jnp.int32, cos.shape, 1)
    first = lane < HEAD_DIM
    low = lane % HEAD_DIM < HEAD_DIM // 2
    W = QK_WIDTH

    def rotate(x):
        partner = jnp.where(low, pltpu.roll(x, V_DIM - HEAD_DIM // 2, 1), pltpu.roll(x, HEAD_DIM // 2, 1))
        return x * cos + partner * sin

    for hd in range(N_HEADS):
        c0 = hd * V_DIM
        qr = rotate(_dot(hb, w_ref[:, c0:c0 + V_DIM])) * Q_SCALE
        q2_ref[0, :, c0:c0 + V_DIM] = jnp.where(first, qr, 0.0).astype(BF16)
        q2_ref[1, :, c0:c0 + V_DIM] = jnp.where(first, 0.0, qr).astype(BF16)
        kr = rotate(_dot(hb, w_ref[:, W + c0:W + c0 + V_DIM]))
        k32_ref[:, c0:c0 + V_DIM] = kr
        kb_ref[:, c0:c0 + V_DIM] = kr.astype(BF16)
    o = 2 * W
    v = _dot(hb, w_ref[:, o:o + ATTN_WIDTH])
    v32_ref[...] = v
    vb_ref[...] = v.astype(BF16)
    o += ATTN_WIDTH
    u_ref[...] = _dot(hb, w_ref[:, o:o + SSM_WIDTH])
    o += SSM_WIDTH
    ga_ref[...] = _dot(hb, w_ref[:, o:o + D_MODEL])
    o += D_MODEL
    gs_ref[...] = _dot(hb, w_ref[:, o:o + D_MODEL])


def _inproj(x, nw, sc, sh, cosf, sinf, w_ext, tm):
    n = x.shape[0]
    wcols = w_ext.shape[1]
    row = lambda i: (i, 0)
    full = lambda i: (0, 0)
    outs = (
        jax.ShapeDtypeStruct((2, n, QK_WIDTH), BF16),
        jax.ShapeDtypeStruct((n, QK_WIDTH), F32),
        jax.ShapeDtypeStruct((n, QK_WIDTH), BF16),
        jax.ShapeDtypeStruct((n, ATTN_WIDTH), F32),
        jax.ShapeDtypeStruct((n, ATTN_WIDTH), BF16),
        jax.ShapeDtypeStruct((n, SSM_WIDTH), F32),
        jax.ShapeDtypeStruct((n, D_MODEL), F32),
        jax.ShapeDtypeStruct((n, D_MODEL), F32),
    )
    out_specs = (
        pl.BlockSpec((2, tm, QK_WIDTH), lambda i: (0, i, 0)),
        pl.BlockSpec((tm, QK_WIDTH), row),
        pl.BlockSpec((tm, QK_WIDTH), row),
        pl.BlockSpec((tm, ATTN_WIDTH), row),
        pl.BlockSpec((tm, ATTN_WIDTH), row),
        pl.BlockSpec((tm, SSM_WIDTH), row),
        pl.BlockSpec((tm, D_MODEL), row),
        pl.BlockSpec((tm, D_MODEL), row),
    )
    return pl.pallas_call(
        _inproj_kernel,
        grid=(n // tm,),
        in_specs=[pl.BlockSpec((tm, D_MODEL), row),
                  pl.BlockSpec((1, D_MODEL), full),
                  _mod_spec(sc.shape[0], tm),
                  _mod_spec(sh.shape[0], tm),
                  pl.BlockSpec((tm, V_DIM), row),
                  pl.BlockSpec((tm, V_DIM), row),
                  pl.BlockSpec((D_MODEL, wcols), full)],
        out_specs=out_specs,
        out_shape=outs,
        compiler_params=_cparams("arbitrary"),
        name="in_proj",
    )(x, nw, sc, sh, cosf, sinf, w_ext)


def _lambda_full(lq1_ref, lk1_ref, lq2_ref, lk2_ref, lam_init):
    s1 = jnp.sum(lq1_ref[...] * lk1_ref[...], axis=-1, keepdims=True)
    s2 = jnp.sum(lq2_ref[...] * lk2_ref[...], axis=-1, keepdims=True)
    return jnp.exp(s1) - jnp.exp(s2) + lam_init


def _subln(o, w, lam_init):
    return _rms(o, w) * (1.0 - lam_init)


def _pattn_kernel(q_ref, k_ref, v_ref, lq1_ref, lk1_ref, lq2_ref, lk2_ref, sub_ref, o_ref,
                  s_ref, m_ref, acc_ref, *, tq, lam_init):
    i = pl.program_id(1)
    m_ref[...] = jnp.full(m_ref.shape, NEG_INF, F32)
    acc_ref[...] = jnp.zeros(acc_ref.shape, F32)
    ones = jnp.ones((tq, V_DIM), BF16)

    def produce(comp, chunk):
        start = pl.multiple_of(chunk * tq, tq)
        s_ref[comp] = _dot_nt(q_ref[comp], k_ref[pl.ds(start, tq), :])

    def consume(comp, chunk, masked):
        start = pl.multiple_of(chunk * tq, tq)
        v1 = jnp.concatenate([v_ref[pl.ds(start, tq), :], ones], axis=1)
        s = s_ref[comp]
        if masked:
            r = lax.broadcasted_iota(jnp.int32, (tq, tq), 0)
            c = lax.broadcasted_iota(jnp.int32, (tq, tq), 1)
            s = jnp.where(c <= r, s, NEG_INF)
        m_old = m_ref[comp]
        m_new = jnp.maximum(m_old, jnp.max(s, axis=-1, keepdims=True))
        p = jnp.exp2(s - jnp.tile(m_new, (1, tq // V_DIM)))
        alpha = jnp.exp2(m_old - m_new)
        acc_ref[comp] = jnp.tile(alpha, (1, 2)) * acc_ref[comp] + _dot(p.astype(BF16), v1)
        m_ref[comp] = m_new

    def pair(chunk, last_next):
        produce(1, chunk)
        consume(0, chunk, False)
        produce(0, last_next)
        consume(1, chunk, False)

    def run(first, count):
        for w in range(count):
            pair(first + w, first + w + 1)

    def body(t, carry):
        run(ATTN_UNROLL * t, ATTN_UNROLL)
        return carry

    produce(0, 0)
    lax.fori_loop(0, i // ATTN_UNROLL, body, 0)
    width = ATTN_UNROLL // 2
    while width >= 1:
        first = (i // (2 * width)) * (2 * width)

        @pl.when(i % (2 * width) >= width)
        def _(first=first, width=width):
            run(first, width)

        width //= 2

    produce(1, i)
    consume(0, i, True)
    consume(1, i, True)

    lam = _lambda_full(lq1_ref, lk1_ref, lq2_ref, lk2_ref, lam_init)
    o1 = acc_ref[0, :, :V_DIM] / acc_ref[0, :, V_DIM:]
    o2 = acc_ref[1, :, :V_DIM] / acc_ref[1, :, V_DIM:]
    o_ref[...] = _subln(o1 - lam * o2, sub_ref[...], lam_init).astype(BF16)


def _prompt_attention(q2, kb, vb, lq1, lk1, lq2, lk2, subln, lam_init, tq):
    t = kb.shape[0]
    vec = lambda h, i: (0, 0)
    kern = functools.partial(_pattn_kernel, tq=tq, lam_init=lam_init)
    return pl.pallas_call(
        kern,
        grid=(N_HEADS, t // tq),
        in_specs=[pl.BlockSpec((2, tq, V_DIM), lambda h, i: (0, i, h)),
                  pl.BlockSpec((t, V_DIM), lambda h, i: (0, h)),
                  pl.BlockSpec((t, V_DIM), lambda h, i: (0, h)),
                  pl.BlockSpec((1, HEAD_DIM), vec),
                  pl.BlockSpec((1, HEAD_DIM), vec),
                  pl.BlockSpec((1, HEAD_DIM), vec),
                  pl.BlockSpec((1, HEAD_DIM), vec),
                  pl.BlockSpec((1, V_DIM), vec)],
        out_specs=pl.BlockSpec((tq, V_DIM), lambda h, i: (i, h)),
        out_shape=jax.ShapeDtypeStruct((t, ATTN_WIDTH), BF16),
        scratch_shapes=[pltpu.VMEM((2, tq, tq), F32),
                        pltpu.VMEM((2, tq, V_DIM), F32),
                        pltpu.VMEM((2, tq, 2 * V_DIM), F32)],
        compiler_params=_cparams("arbitrary", "arbitrary"),
        name="prompt_attn",
    )(q2, kb, vb, lq1, lk1, lq2, lk2, subln)


def _sattn_kernel(pt_ref, q_ref, kn_ref, vn_ref, lq1_ref, lk1_ref, lq2_ref, lk2_ref, sub_ref, *rest,
                  pages_per_step, n_new, lam_init):
    g_pages = pages_per_step
    k_refs = rest[:g_pages]
    v_refs = rest[g_pages:2 * g_pages]
    o_ref = rest[2 * g_pages]
    m_ref, l_ref, acc_ref, bias_ref = rest[2 * g_pages + 1:]
    b = pl.program_id(0)
    g = pl.program_id(1)
    q = q_ref[0]
    rows_per_head = 2 * n_new

    @pl.when((b == 0) & (g == 0))
    def _():
        r = lax.broadcasted_iota(jnp.int32, bias_ref.shape, 0) // rows_per_head
        c = lax.broadcasted_iota(jnp.int32, bias_ref.shape, 1) % N_HEADS
        bias_ref[...] = jnp.where(r == c, 0.0, NEG_INF)

    @pl.when(g == 0)
    def _():
        m_ref[...] = jnp.full(m_ref.shape, NEG_INF, F32)
        l_ref[...] = jnp.zeros(l_ref.shape, F32)
        acc_ref[...] = jnp.zeros(acc_ref.shape, F32)

    def update(s, v):
        m_old = m_ref[...]
        m_new = jnp.maximum(m_old, jnp.max(s, axis=-1, keepdims=True))
        p = jnp.exp2(s - m_new)
        alpha = jnp.exp2(m_old - m_new)
        l_ref[...] = alpha * l_ref[...] + jnp.sum(p, axis=-1, keepdims=True)
        acc_ref[...] = alpha * acc_ref[...] + _dot(p.astype(BF16), v)
        m_ref[...] = m_new

    half = g_pages // 2
    scores = []
    for grp in range(2):
        k = jnp.concatenate([r[0].astype(BF16) for r in k_refs[grp * half:(grp + 1) * half]], axis=0)
        scores.append(_dot_nt(q, k))
    for grp in range(2):
        v = jnp.concatenate([r[0].astype(BF16) for r in v_refs[grp * half:(grp + 1) * half]], axis=0)
        update(scores[grp] + bias_ref[...], v)

    @pl.when(g == pl.num_programs(1) - 1)
    def _():
        kn = kn_ref[0].astype(BF16)
        vn = vn_ref[0].astype(BF16)
        s = _dot_nt(q, kn)
        row = lax.broadcasted_iota(jnp.int32, s.shape, 0)
        col = lax.broadcasted_iota(jnp.int32, s.shape, 1)
        keep = (row // rows_per_head == col % N_HEADS) & (col // N_HEADS <= row % n_new)
        update(jnp.where(keep, s, NEG_INF), vn)
        lam = _lambda_full(lq1_ref, lk1_ref, lq2_ref, lk2_ref, lam_init)
        o = acc_ref[...] / l_ref[...]
        d = o - lam * pltpu.roll(o, o.shape[0] - n_new, 0)
        o_ref[0] = _subln(d, sub_ref[...], lam_init)


def _sample_attention(page_table, q_rows, k_new, v_new, cache_k, cache_v, lq1, lk1, lq2, lk2, subln,
                      lam_init, n_new, pages_per_step):
    nb, n_pages = page_table.shape
    gp = pages_per_step
    assert gp % 2 == 0
    rows = q_rows.shape[1]
    page_rows = PAGE_SIZE * N_HEADS
    new_rows = k_new.shape[1]
    vec = lambda b, g, pt: (0, 0)

    def page_spec(r):
        return pl.BlockSpec((1, page_rows, V_DIM),
                            lambda b, g, pt: (pt[b * n_pages + g * gp + r], 0, 0))

    kern = functools.partial(_sattn_kernel, pages_per_step=gp, n_new=n_new, lam_init=lam_init)
    grid_spec = pltpu.PrefetchScalarGridSpec(
        num_scalar_prefetch=1,
        grid=(nb, n_pages // gp),
        in_specs=[pl.BlockSpec((1, rows, V_DIM), lambda b, g, pt: (b, 0, 0)),
                  pl.BlockSpec((1, new_rows, V_DIM), lambda b, g, pt: (b, 0, 0)),
                  pl.BlockSpec((1, new_rows, V_DIM), lambda b, g, pt: (b, 0, 0)),
                  pl.BlockSpec((1, HEAD_DIM), vec),
                  pl.BlockSpec((1, HEAD_DIM), vec),
                  pl.BlockSpec((1, HEAD_DIM), vec),
                  pl.BlockSpec((1, HEAD_DIM), vec),
                  pl.BlockSpec((1, V_DIM), vec)]
                 + [page_spec(r) for r in range(gp)] + [page_spec(r) for r in range(gp)],
        out_specs=pl.BlockSpec((1, rows, V_DIM), lambda b, g, pt: (b, 0, 0)),
        scratch_shapes=[pltpu.VMEM((rows, 1), F32),
                        pltpu.VMEM((rows, 1), F32),
                        pltpu.VMEM((rows, V_DIM), F32),
                        pltpu.VMEM((rows, (gp // 2) * page_rows), F32)],
    )
    return pl.pallas_call(
        kern,
        grid_spec=grid_spec,
        out_shape=jax.ShapeDtypeStruct((nb, rows, V_DIM), F32),
        compiler_params=_cparams("arbitrary", "arbitrary"),
        name="sample_attn",
    )(page_table.reshape(-1), q_rows, k_new, v_new, lq1, lk1, lq2, lk2, subln,
      *([cache_k] * gp), *([cache_v] * gp))


def _ssm_disc_kernel(lr_ref, li_ref, ls_ref, are_ref, aim_ref, fre_ref, fim_ref, *, n_pow):
    lr = lr_ref[...]
    li = li_ref[...]
    dt = jnp.exp(ls_ref[...])
    mag = jnp.exp(lr * dt)
    a_re = mag * jnp.cos(li * dt)
    a_im = mag * jnp.sin(li * dt)
    den = lr * lr + li * li
    nr = a_re - 1.0
    ni = a_im
    fre_ref[...] = (nr * lr + ni * li) / den
    fim_ref[...] = (ni * lr - nr * li) / den
    pr, pi = a_re, a_im
    for kk in range(n_pow):
        are_ref[kk] = pr
        aim_ref[kk] = pi
        pr, pi = pr * a_re - pi * a_im, pr * a_im + pi * a_re


def _ssm_bb_kernel(fre_ref, fim_ref, bre_ref, bim_ref, ore_ref, oim_ref):
    fr = fre_ref[...]
    fi = fim_ref[...]
    br = bre_ref[...]
    bi = bim_ref[...]
    ore_ref[...] = fr * br - fi * bi
    oim_ref[...] = fr * bi + fi * br


def _ssm_prepare(lam_re, lam_im, log_step, b_re, b_im, c_re, c_im, n_pow):
    gshape = jax.ShapeDtypeStruct((N_GROUPS, STATE_DIM), F32)
    pshape = jax.ShapeDtypeStruct((n_pow, N_GROUPS, STATE_DIM), F32)
    a_re, a_im, f_re, f_im = pl.pallas_call(
        functools.partial(_ssm_disc_kernel, n_pow=n_pow),
        out_shape=(pshape, pshape, gshape, gshape),
        name="ssm_discretise",
    )(lam_re, lam_im, log_step.reshape(N_GROUPS, 1))
    bshape = jax.ShapeDtypeStruct((N_STATE, SSM_GROUP), F32)
    bb_re, bb_im = pl.pallas_call(
        _ssm_bb_kernel, out_shape=(bshape, bshape), name="ssm_input_matrix",
    )(f_re.reshape(N_STATE, 1), f_im.reshape(N_STATE, 1),
      b_re.reshape(N_STATE, SSM_GROUP), b_im.reshape(N_STATE, SSM_GROUP))
    gc = N_GROUPS // SSM_CHUNKS
    eye = jnp.eye(gc, dtype=F32)

    def in_mat(bb):
        b4 = bb.reshape(SSM_CHUNKS, gc, STATE_DIM, SSM_GROUP)
        m = jnp.einsum('cgpi,gh->cgihp', b4, eye)
        return m.reshape(SSM_CHUNKS, gc * SSM_GROUP, gc * STATE_DIM)

    def out_mat(cc):
        c4 = cc.reshape(SSM_CHUNKS, gc, SSM_GROUP, STATE_DIM)
        m = jnp.einsum('cgip,gh->cgphi', c4, eye)
        return m.reshape(SSM_CHUNKS, gc * STATE_DIM, gc * SSM_GROUP)

    b_mat = jnp.concatenate([in_mat(bb_re), in_mat(bb_im)], axis=-1).astype(BF16)
    c_mat = jnp.concatenate([out_mat(c_re), out_mat(-c_im)], axis=-2).astype(BF16)
    return (a_re.reshape(n_pow, N_STATE), a_im.reshape(n_pow, N_STATE), b_mat, c_mat)


def _shift_rows(x, s, row):
    return jnp.where(row >= s, pltpu.roll(x, s, 0), 0.0)


def _ssm_scan_kernel(u_ref, are_ref, aim_ref, b_ref, c_ref, d_ref, wg_ref,
                     y_ref, hre_ref, him_ref, cre_ref, cim_ref, *, tm):
    @pl.when(pl.program_id(0) == 0)
    def _():
        cre_ref[...] = jnp.zeros(cre_ref.shape, F32)
        cim_ref[...] = jnp.zeros(cim_ref.shape, F32)

    u = u_ref[...]
    ub = u.astype(BF16)
    seg = tm // SCAN_ROWS
    pr_i = lax.broadcasted_iota(jnp.int32, (tm, tm), 0)
    pc_i = lax.broadcasted_iota(jnp.int32, (tm, tm), 1)
    perm = (pc_i == seg * (pr_i % SCAN_ROWS) + pr_i // SCAN_ROWS).astype(BF16)
    perm_t = (pr_i == seg * (pc_i % SCAN_ROWS) + pc_i // SCAN_ROWS).astype(BF16)
    ubp = _dot(perm, ub).astype(BF16)
    row = lax.broadcasted_iota(jnp.int32, (SCAN_ROWS, SSM_CHUNK_ST), 0)
    ys = []
    for c in range(SSM_CHUNKS):
        lanes = slice(c * SSM_CHUNK_ST, (c + 1) * SSM_CHUNK_ST)
        bu = _dot(ubp[:, c * SSM_CHUNK_IN:(c + 1) * SSM_CHUNK_IN], b_ref[c])
        a_r = are_ref[0:1, lanes]
        a_i = aim_ref[0:1, lanes]
        loc_r, loc_i = [], []
        x_r = x_i = None
        for gi in range(seg):
            rows = slice(gi * SCAN_ROWS, (gi + 1) * SCAN_ROWS)
            b_r = bu[rows, :SSM_CHUNK_ST]
            b_i = bu[rows, SSM_CHUNK_ST:]
            if gi == 0:
                x_r, x_i = b_r, b_i
            else:
                x_r, x_i = b_r + (a_r * x_r - a_i * x_i), b_i + (a_r * x_i + a_i * x_r)
            loc_r.append(x_r)
            loc_i.append(x_i)
        e_r, e_i = x_r, x_i
        p_r = are_ref[seg - 1:seg, lanes]
        p_i = aim_ref[seg - 1:seg, lanes]
        h_r = cre_ref[0:1, lanes]
        h_i = cim_ref[0:1, lanes]
        first = row == 0
        e_r = e_r + jnp.where(first, p_r * h_r - p_i * h_i, 0.0)
        e_i = e_i + jnp.where(first, p_r * h_i + p_i * h_r, 0.0)
        s = 1
        while s < SCAN_ROWS:
            sr = _shift_rows(e_r, s, row)
            si = _shift_rows(e_i, s, row)
            e_r, e_i = e_r + (p_r * sr - p_i * si), e_i + (p_r * si + p_i * sr)
            p_r, p_i = p_r * p_r - p_i * p_i, 2.0 * p_r * p_i
            s *= 2
        cre_ref[0:1, lanes] = e_r[SCAN_ROWS - 1:SCAN_ROWS]
        cim_ref[0:1, lanes] = e_i[SCAN_ROWS - 1:SCAN_ROWS]
        in_r = jnp.where(first, h_r, pltpu.roll(e_r, 1, 0))
        in_i = jnp.where(first, h_i, pltpu.roll(e_i, 1, 0))
        out_r, out_i = [], []
        for gi in range(seg):
            q_r = are_ref[gi:gi + 1, lanes]
            q_i = aim_ref[gi:gi + 1, lanes]
            out_r.append(loc_r[gi] + (q_r * in_r - q_i * in_i))
            out_i.append(loc_i[gi] + (q_r * in_i + q_i * in_r))
        hcat_p = jnp.concatenate([jnp.concatenate(out_r, axis=0), jnp.concatenate(out_i, axis=0)],
                                 axis=1).astype(BF16)
        hcat = _dot(perm_t, hcat_p).astype(BF16)
        ys.append(_dot(hcat, c_ref[c]))
    y = jnp.concatenate(ys, axis=1) + d_ref[...] * u
    y = _gelu(y)
    y = y * _sigmoid(_dot(y.astype(BF16), wg_ref[...]))
    y_ref[...] = y.astype(BF16)
    hre_ref[...] = cre_ref[0:1, :]
    him_ref[...] = cim_ref[0:1, :]


def _ssm_prompt(u, a_re, a_im, b_mat, c_mat, d, wg, tm):
    t = u.shape[0]
    n_pow = a_re.shape[0]
    full2 = lambda i: (0, 0)
    full3 = lambda i: (0, 0, 0)
    kern = functools.partial(_ssm_scan_kernel, tm=tm)
    return pl.pallas_call(
        kern,
        grid=(t // tm,),
        in_specs=[pl.BlockSpec((tm, SSM_WIDTH), lambda i: (i, 0)),
                  pl.BlockSpec((n_pow, N_STATE), full2),
                  pl.BlockSpec((n_pow, N_STATE), full2),
                  pl.BlockSpec(b_mat.shape, full3),
                  pl.BlockSpec(c_mat.shape, full3),
                  pl.BlockSpec((1, SSM_WIDTH), full2),
                  pl.BlockSpec((SSM_WIDTH, SSM_WIDTH), full2)],
        out_specs=(pl.BlockSpec((tm, SSM_WIDTH), lambda i: (i, 0)),
                   pl.BlockSpec((1, N_STATE), full2),
                   pl.BlockSpec((1, N_STATE), full2)),
        out_shape=(jax.ShapeDtypeStruct((t, SSM_WIDTH), BF16),
                   jax.ShapeDtypeStruct((1, N_STATE), F32),
                   jax.ShapeDtypeStruct((1, N_STATE), F32)),
        scratch_shapes=[pltpu.VMEM((8, N_STATE), F32), pltpu.VMEM((8, N_STATE), F32)],
        compiler_params=_cparams("arbitrary"),
        name="ssm_prompt",
    )(u, a_re, a_im, b_mat, c_mat, d, wg)


def _ssm_step_kernel(u_ref, h0re_ref, h0im_ref, are_ref, aim_ref, b_ref, c_ref, d_ref, wg_ref,
                     y_ref, hre_ref, him_ref, *, n_steps):
    a_r = are_ref[0:1, :]
    a_i = aim_ref[0:1, :]
    h_r = h0re_ref[...]
    h_i = h0im_ref[...]
    for t in range(n_steps):
        u = u_ref[t]
        ub = u.astype(BF16)
        brs, bis = [], []
        for c in range(SSM_CHUNKS):
            bu = _dot(ub[:, c * SSM_CHUNK_IN:(c + 1) * SSM_CHUNK_IN], b_ref[c])
            brs.append(bu[:, :SSM_CHUNK_ST])
            bis.append(bu[:, SSM_CHUNK_ST:])
        bu_r = jnp.concatenate(brs, axis=1)
        bu_i = jnp.concatenate(bis, axis=1)
        h_r, h_i = a_r * h_r - a_i * h_i + bu_r, a_r * h_i + a_i * h_r + bu_i
        ys = []
        for c in range(SSM_CHUNKS):
            lanes = slice(c * SSM_CHUNK_ST, (c + 1) * SSM_CHUNK_ST)
            hcat = jnp.concatenate([h_r[:, lanes], h_i[:, lanes]], axis=1).astype(BF16)
            ys.append(_dot(hcat, c_ref[c]))
        y = jnp.concatenate(ys, axis=1) + d_ref[...] * u
        y = _gelu(y)
        y = y * _sigmoid(_dot(y.astype(BF16), wg_ref[...]))
        y_ref[t] = y.astype(BF16)
    hre_ref[...] = h_r
    him_ref[...] = h_i


def _ssm_sample(u_tb, h0_re, h0_im, a_re, a_im, b_mat, c_mat, d, wg):
    n_steps, nb, _ = u_tb.shape
    kern = functools.partial(_ssm_step_kernel, n_steps=n_steps)
    return pl.pallas_call(
        kern,
        out_shape=(jax.ShapeDtypeStruct((n_steps, nb, SSM_WIDTH), BF16),
                   jax.ShapeDtypeStruct((nb, N_STATE), F32),
                   jax.ShapeDtypeStruct((nb, N_STATE), F32)),
        compiler_params=pltpu.CompilerParams(vmem_limit_bytes=VMEM_LIMIT_BYTES),
        name="ssm_sample",
    )(u_tb, h0_re, h0_im, a_re, a_im, b_mat, c_mat, d, wg)


def _kth_largest_rows(c, k):
    cnt = jnp.zeros((1, c.shape[1]), F32)
    tau = jnp.full((1, c.shape[1]), NEG_INF, F32)
    for _ in range(k):
        m = jnp.max(c, axis=0, keepdims=True)
        eq = c == m
        tau = jnp.where(cnt < k, m, tau)
        cnt = cnt + jnp.sum(eq.astype(F32), axis=0, keepdims=True)
        c = jnp.where(eq, NEG_INF, c)
    return tau


def _top_rows(a, k, out_ref):
    for r in range(k):
        m = jnp.max(a, axis=0, keepdims=True)
        out_ref[r:r + 1, :] = m
        a = jnp.where(a == m, NEG_INF, a)


def _mix_kernel(x_ref, o_ref, ys_ref, ga_ref, gs_ref, g1_ref, sc2_ref, sh2_ref, npost_ref, npre_ref,
                wao_ref, wso_ref, wout_ref, wq_ref, skt_ref,
                x1_ref, h2_ref, c1_ref, e1_ref, r2_ref, e2_ref, v1_ref, v2_ref, cnt_ref):
    y_attn = _dot(o_ref[...], wao_ref[...])
    y_ssm = _dot(ys_ref[...], wso_ref[...])
    merged = _sigmoid(ga_ref[...]) * y_attn + _sigmoid(gs_ref[...]) * y_ssm
    mix = _dot(merged.astype(BF16), wout_ref[...])
    x1 = x_ref[...] + g1_ref[...] * _rms(mix, npost_ref[...])
    x1_ref[...] = x1
    h2 = _rms(x1, npre_ref[...]) * (1.0 + sc2_ref[...]) + sh2_ref[...]
    h2b = h2.astype(BF16)
    h2_ref[...] = h2b
    qp = _dot(h2b, wq_ref[...]).astype(BF16)
    s_all = _dot_nt(skt_ref[...], qp)
    for hd in range(PEER_HEADS):
        r0 = hd * 2 * N_KEYS
        k0 = hd * N_KEYS
        s1 = s_all[r0:r0 + N_KEYS]
        s2 = s_all[r0 + N_KEYS:r0 + 2 * N_KEYS]
        _top_rows(s1, PEER_TOPK, v1_ref)
        _top_rows(s2, PEER_TOPK, v2_ref)
        v1lo = v1_ref[0:8, :]
        v2lo = v2_ref[0:8, :]
        r8 = lax.broadcasted_iota(jnp.int32, v1lo.shape, 0)
        m1 = v1_ref[0:1, :]
        m2 = v2_ref[0:1, :]
        pieces = [
            v1lo + m2,
            v1_ref[8:16, :] + m2,
            v1lo + v2_ref[1:2, :],
            jnp.where((r8 >= 2) & (r8 <= 4), v1lo + v2_ref[2:3, :], NEG_INF),
            jnp.where((r8 >= 2) & (r8 <= 3), v1lo + v2_ref[3:4, :], NEG_INF),
            jnp.where(r8 == 2, v1lo + v2_ref[4:5, :], NEG_INF),
            jnp.where(r8 >= 2, m1 + v2lo, NEG_INF),
            m1 + v2_ref[8:16, :],
            jnp.where(r8 >= 2, v1_ref[1:2, :] + v2lo, NEG_INF),
        ]
        cand = jnp.concatenate(pieces, axis=0)
        tau = _kth_largest_rows(cand, PEER_TOPK)
        z = jnp.sum(jnp.where(cand >= tau, jnp.exp(cand - (m1 + m2)), 0.0), axis=0, keepdims=True)
        sel = [(p >= tau).astype(F32) for p in pieces]
        n0 = jnp.sum(sel[6] + sel[7], axis=0, keepdims=True)
        n1 = jnp.sum(sel[8], axis=0, keepdims=True)
        cnt_ref[0:8, :] = (sel[0] + sel[2] + sel[3] + sel[4] + sel[5]
                           + jnp.where(r8 == 0, n0, 0.0) + jnp.where(r8 == 1, n1, 0.0))
        cnt_ref[8:16, :] = sel[1]
        c1 = jnp.zeros(s1.shape, F32)
        r2 = jnp.full(s2.shape, float(N_KEYS - 1), F32)
        for r in range(PEER_TOPK):
            c1 = jnp.where(s1 == v1_ref[r:r + 1, :], cnt_ref[r:r + 1, :], c1)
            r2 = jnp.where(s2 == v2_ref[r:r + 1, :], float(r), r2)
        c1_ref[k0:k0 + N_KEYS, :] = c1
        e1_ref[k0:k0 + N_KEYS, :] = jnp.exp(s1 - m1)
        r2_ref[k0:k0 + N_KEYS, :] = r2.astype(BF16)
        e2_ref[k0:k0 + N_KEYS, :] = (jnp.exp(s2 - m2) / z).astype(BF16)


def _mix(x, ob, ysb, ga, gs, g1, sc2, sh2, npost, npre, wao, wso, wout, wq, skt, tm):
    n = x.shape[0]
    row = lambda i: (i, 0)
    col = lambda i: (0, i)
    full = lambda i: (0, 0)
    nrow = 2 * N_KEYS * PEER_HEADS
    krow = N_KEYS * PEER_HEADS
    return pl.pallas_call(
        _mix_kernel,
        grid=(n // tm,),
        in_specs=[pl.BlockSpec((tm, D_MODEL), row),
                  pl.BlockSpec((tm, ATTN_WIDTH), row),
                  pl.BlockSpec((tm, SSM_WIDTH), row),
                  pl.BlockSpec((tm, D_MODEL), row),
                  pl.BlockSpec((tm, D_MODEL), row),
                  _mod_spec(g1.shape[0], tm),
                  _mod_spec(sc2.shape[0], tm),
                  _mod_spec(sh2.shape[0], tm),
                  pl.BlockSpec((1, D_MODEL), full),
                  pl.BlockSpec((1, D_MODEL), full),
                  pl.BlockSpec((ATTN_WIDTH, D_MODEL), full),
                  pl.BlockSpec((SSM_WIDTH, D_MODEL), full),
                  pl.BlockSpec((D_MODEL, D_MODEL), full),
                  pl.BlockSpec((D_MODEL, PEER_HEADS * PEER_KEY_DIM), full),
                  pl.BlockSpec((nrow, PEER_HEADS * PEER_KEY_DIM), full)],
        out_specs=(pl.BlockSpec((tm, D_MODEL), row),
                   pl.BlockSpec((tm, D_MODEL), row),
                   pl.BlockSpec((krow, tm), col),
                   pl.BlockSpec((krow, tm), col),
                   pl.BlockSpec((krow, tm), col),
                   pl.BlockSpec((krow, tm), col)),
        out_shape=(jax.ShapeDtypeStruct((n, D_MODEL), F32),
                   jax.ShapeDtypeStruct((n, D_MODEL), BF16),
                   jax.ShapeDtypeStruct((krow, n), F32),
                   jax.ShapeDtypeStruct((krow, n), F32),
                   jax.ShapeDtypeStruct((krow, n), BF16),
                   jax.ShapeDtypeStruct((krow, n), BF16)),
        scratch_shapes=[pltpu.VMEM((PEER_TOPK, tm), F32)] * 3,
        compiler_params=_cparams("arbitrary"),
        name="mix_route",
    )(x, ob, ysb, ga, gs, g1, sc2, sh2, npost, npre, wao, wso, wout, wq, skt)


BF16_ROWS = 16
PEER_CHUNKS_PER_STEP = 2


def _tile_bf16(row):
    return jnp.broadcast_to(row, (BF16_ROWS, row.shape[1])).astype(BF16)


def _peer_kernel(h2_ref, u0_ref, *rest, ec, n_sub):
    u_refs = rest[:n_sub]
    (vt_ref, c1_ref, e1_ref, r2_ref, e2_ref, x1_ref, g2_ref, nw_ref,
     y_ref, acc_ref, act0_ref, act1_ref) = rest[n_sub:]
    acts = (act0_ref, act1_ref)
    g = pl.program_id(1)
    h2 = h2_ref[...]
    blocks = ec // N_KEYS

    @pl.when(g == 0)
    def _():
        acc_ref[...] = jnp.zeros(acc_ref.shape, F32)
        act0_ref[...] = _dot_nt(u0_ref[...], h2)

    def consume(act_ref, chunk, col0):
        ws = []
        for ii in range(blocks):
            key1 = chunk * blocks + ii
            a = _gelu(act_ref[ii * N_KEYS:(ii + 1) * N_KEYS, :].astype(BF16))
            tiles = N_KEYS // BF16_ROWS
            totals = [None] * tiles
            for hd in range(PEER_HEADS):
                cnt = _tile_bf16(c1_ref[pl.ds(hd * N_KEYS + key1, 1), :])
                e1 = _tile_bf16(e1_ref[pl.ds(hd * N_KEYS + key1, 1), :])
                for t in range(tiles):
                    k0 = hd * N_KEYS + t * BF16_ROWS
                    r2 = r2_ref[k0:k0 + BF16_ROWS, :]
                    e2 = e2_ref[k0:k0 + BF16_ROWS, :]
                    term = jnp.where(r2 < cnt, e2, jnp.zeros_like(e2)) * e1
                    totals[t] = term if totals[t] is None else totals[t] + term
            for t in range(tiles):
                ws.append(totals[t] * a[t * BF16_ROWS:(t + 1) * BF16_ROWS])
        w = jnp.concatenate(ws, axis=0)
        acc_ref[...] += _dot(vt_ref[:, col0:col0 + ec], w)

    for s in range(n_sub):
        acts[(s + 1) % 2][...] = _dot_nt(u_refs[s][...], h2)
        consume(acts[s % 2], n_sub * g + s, s * ec)

    @pl.when(g == pl.num_programs(1) - 1)
    def _():
        f = acc_ref[...].T
        y_ref[...] = x1_ref[...] + g2_ref[...] * _rms(f, nw_ref[...])


def _peer(h2b, u_bf, vt_bf, c1, e1, r2, e2, x1, g2, nw, tn, ec):
    n = h2b.shape[0]
    krow = c1.shape[0]
    n_chunks = N_EXPERTS // ec
    n_sub = PEER_CHUNKS_PER_STEP
    kern = functools.partial(_peer_kernel, ec=ec, n_sub=n_sub)
    g2_spec = (pl.BlockSpec((1, D_MODEL), lambda t, c: (0, 0)) if g2.shape[0] == 1
               else pl.BlockSpec((tn, D_MODEL), lambda t, c: (t, 0)))

    def lookahead_spec(s):
        return pl.BlockSpec((ec, D_MODEL), lambda t, c: (jnp.minimum(n_sub * c + s + 1, n_chunks - 1), 0))

    return pl.pallas_call(
        kern,
        grid=(n // tn, n_chunks // n_sub),
        in_specs=[pl.BlockSpec((tn, D_MODEL), lambda t, c: (t, 0)),
                  pl.BlockSpec((ec, D_MODEL), lambda t, c: (0, 0))]
                 + [lookahead_spec(s) for s in range(n_sub)]
                 + [pl.BlockSpec((D_MODEL, n_sub * ec), lambda t, c: (0, c)),
                  pl.BlockSpec((krow, tn), lambda t, c: (0, t)),
                  pl.BlockSpec((krow, tn), lambda t, c: (0, t)),
                  pl.BlockSpec((krow, tn), lambda t, c: (0, t)),
                  pl.BlockSpec((krow, tn), lambda t, c: (0, t)),
                  pl.BlockSpec((tn, D_MODEL), lambda t, c: (t, 0)),
                  g2_spec,
                  pl.BlockSpec((1, D_MODEL), lambda t, c: (0, 0))],
        out_specs=pl.BlockSpec((tn, D_MODEL), lambda t, c: (t, 0)),
        out_shape=jax.ShapeDtypeStruct((n, D_MODEL), F32),
        scratch_shapes=[pltpu.VMEM((D_MODEL, tn), F32),
                        pltpu.VMEM((ec, tn), F32), pltpu.VMEM((ec, tn), F32)],
        compiler_params=_cparams("arbitrary", "arbitrary"),
        name="peer_experts",
    )(h2b, u_bf, *([u_bf] * n_sub), vt_bf, c1, e1, r2, e2, x1, g2, nw)


def _rope_tables(pos):
    half = HEAD_DIM // 2
    inv = ROPE_THETA ** (-jnp.arange(half, dtype=F32) / half)
    ang = pos.astype(F32)[:, None] * inv[None, :]
    cos = jnp.cos(ang)
    sin = jnp.sin(ang)
    cosf = jnp.concatenate([cos, cos, cos, cos], axis=1)
    sinf = jnp.concatenate([-sin, sin, -sin, sin], axis=1)
    return cosf, sinf


def _tile(n, pref):
    t = min(n, pref)
    assert n % t == 0, (n, pref)
    return t


def kernel(x_prompt, x_sample, cache_k, cache_v, state_ssm_re, state_ssm_im, page_table, c_prompt, c_sample, w_ada, b_ada, norm_pre_mix, norm_post_mix, norm_pre_ffn, norm_post_ffn, w_in, lambda_q1, lambda_k1, lambda_q2, lambda_k2, subln, w_attn_out, ssm_lambda_re, ssm_lambda_im, ssm_log_step, ssm_b_re, ssm_b_im, ssm_c_re, ssm_c_im, ssm_d, w_glu, w_ssm_out, w_out, peer_w_query, peer_sub_keys, peer_u, peer_v):
    depth = w_ada.shape[0]
    bp, tp, _ = x_prompt.shape
    nb, tn_new, _ = x_sample.shape
    assert bp == 1 and 2 * tn_new == 8
    n_pages = page_table.shape[1]
    past_len = n_pages * PAGE_SIZE
    n_phys = cache_k.shape[1]
    ns = nb * tn_new

    pos_p = jnp.arange(tp, dtype=jnp.int32)
    pos_s = jnp.tile(past_len + jnp.arange(tn_new, dtype=jnp.int32), nb)
    cos_p, sin_p = _rope_tables(pos_p)
    cos_s, sin_s = _rope_tables(pos_s)

    tm_p = _tile(tp, 256)
    tq = _tile(tp, 512)
    tscan = _tile(tp, 128)
    assert tscan % SCAN_ROWS == 0
    pages_per_step = next(p for p in (16, 8, 4, 2) if n_pages % p == 0)

    yp = x_prompt.reshape(tp, D_MODEL)
    ys = x_sample.reshape(ns, D_MODEL)
    c_all = jnp.concatenate([c_prompt, c_sample], axis=0)
    pad = (-c_all.shape[0]) % 8
    c_all = jnp.pad(c_all, ((0, pad), (0, 0)))

    outs = [[] for _ in range(8)]
    for l in range(depth):
        lam_init = 0.8 - 0.6 * math.exp(-0.3 * l)
        mod = _ada(c_all, w_ada[l], b_ada[l])
        mod_p = [mod[0:1, j * D_MODEL:(j + 1) * D_MODEL] for j in range(N_MOD)]
        mod_s = [jnp.repeat(mod[1:1 + nb, j * D_MODEL:(j + 1) * D_MODEL], tn_new, axis=0)
                 for j in range(N_MOD)]

        w_ext = w_in[l].astype(BF16)
        a_re, a_im, b_mat, c_mat = _ssm_prepare(ssm_lambda_re[l], ssm_lambda_im[l], ssm_log_step[l],
                                                ssm_b_re[l], ssm_b_im[l], ssm_c_re[l], ssm_c_im[l],
                                                tscan // SCAN_ROWS)
        d_row = ssm_d[l].reshape(1, SSM_WIDTH)
        wg = w_glu[l].astype(BF16)
        wao = w_attn_out[l].astype(BF16)
        wso = w_ssm_out[l].astype(BF16)
        wout = w_out[l].astype(BF16)
        wq = peer_w_query[l].astype(BF16)
        sk = peer_sub_keys[l]
        eye = jnp.eye(PEER_HEADS * 2, dtype=F32)
        skt = jnp.einsum('bkd,bc->bkcd', sk.reshape(PEER_HEADS * 2, N_KEYS, PEER_KEY_DIM // 2), eye)
        skt = skt.reshape(PEER_HEADS * 2 * N_KEYS, PEER_HEADS * PEER_KEY_DIM).astype(BF16)
        u_bf = peer_u[l].astype(BF16)
        vt_bf = peer_v[l].T.astype(BF16)
        lq1 = lambda_q1[l].reshape(1, HEAD_DIM)
        lk1 = lambda_k1[l].reshape(1, HEAD_DIM)
        lq2 = lambda_q2[l].reshape(1, HEAD_DIM)
        lk2 = lambda_k2[l].reshape(1, HEAD_DIM)
        sub = subln[l].reshape(1, V_DIM)
        npre_mix = norm_pre_mix[l].reshape(1, D_MODEL)
        npost_mix = norm_post_mix[l].reshape(1, D_MODEL)
        npre_ffn = norm_pre_ffn[l].reshape(1, D_MODEL)
        npost_ffn = norm_post_ffn[l].reshape(1, D_MODEL)

        sh1, sc1, g1, sh2, sc2, g2 = mod_p
        q2, k32, kb, v32, vb, u, ga, gs = _inproj(yp, npre_mix, sc1, sh1, cos_p, sin_p, w_ext, tm_p)
        ob = _prompt_attention(q2, kb, vb, lq1, lk1, lq2, lk2, sub, lam_init, tq)
        ysb, hre, him = _ssm_prompt(u, a_re, a_im, b_mat, c_mat, d_row, wg, tscan)
        x1, h2b, c1, e1, r2, e2 = _mix(yp, ob, ysb, ga, gs, g1, sc2, sh2, npost_mix, npre_ffn,
                                      wao, wso, wout, wq, skt, tm_p)
        yp = _peer(h2b, u_bf, vt_bf, c1, e1, r2, e2, x1, g2, npost_ffn, _tile(tp, 512), 512)
        outs[0].append(k32.reshape(1, tp, N_HEADS, 2 * HEAD_DIM))
        outs[1].append(v32.reshape(1, tp, N_HEADS, V_DIM))
        outs[2].append(hre.reshape(1, N_GROUPS, STATE_DIM))
        outs[3].append(him.reshape(1, N_GROUPS, STATE_DIM))

        sh1, sc1, g1, sh2, sc2, g2 = mod_s
        q2, k32, _, v32, _, u, ga, gs = _inproj(ys, npre_mix, sc1, sh1, cos_s, sin_s, w_ext, ns)
        q_rows = q2.reshape(2, nb, tn_new, N_HEADS, V_DIM).transpose(1, 3, 0, 2, 4)
        q_rows = q_rows.reshape(nb, N_HEADS * 2 * tn_new, V_DIM)
        k_new = jnp.pad(k32.reshape(nb, tn_new * N_HEADS, V_DIM),
                        ((0, 0), (0, (NEW_ROWS - tn_new) * N_HEADS), (0, 0)))
        v_new = jnp.pad(v32.reshape(nb, tn_new * N_HEADS, V_DIM),
                        ((0, 0), (0, (NEW_ROWS - tn_new) * N_HEADS), (0, 0)))
        ck = cache_k.reshape(depth * n_phys, PAGE_SIZE * N_HEADS, V_DIM)
        cv = cache_v.reshape(depth * n_phys, PAGE_SIZE * N_HEADS, V_DIM)
        o_s = _sample_attention(page_table + l * n_phys, q_rows, k_new, v_new, ck, cv,
                                lq1, lk1, lq2, lk2, sub, lam_init, tn_new, pages_per_step)
        ob = o_s.reshape(nb, N_HEADS, 2, tn_new, V_DIM)[:, :, 0].transpose(0, 2, 1, 3)
        ob = ob.reshape(ns, ATTN_WIDTH).astype(BF16)
        u_tb = u.reshape(nb, tn_new, SSM_WIDTH).transpose(1, 0, 2)
        y_tb, hre_s, him_s = _ssm_sample(u_tb, state_ssm_re[l].reshape(nb, N_STATE),
                                         state_ssm_im[l].reshape(nb, N_STATE),
                                         a_re, a_im, b_mat, c_mat, d_row, wg)
        ysb = y_tb.transpose(1, 0, 2).reshape(ns, SSM_WIDTH)
        x1, h2b, c1, e1, r2, e2 = _mix(ys, ob, ysb, ga, gs, g1, sc2, sh2, npost_mix, npre_ffn,
                                      wao, wso, wout, wq, skt, ns)
        ys = _peer(h2b, u_bf, vt_bf, c1, e1, r2, e2, x1, g2, npost_ffn, ns, 512)
        outs[4].append(k32.reshape(nb, tn_new, N_HEADS, 2 * HEAD_DIM))
        outs[5].append(v32.reshape(nb, tn_new, N_HEADS, V_DIM))
        outs[6].append(hre_s.reshape(nb, N_GROUPS, STATE_DIM))
        outs[7].append(him_s.reshape(nb, N_GROUPS, STATE_DIM))

    st = [jnp.stack(o) for o in outs]
    return (yp.reshape(bp, tp, D_MODEL), ys.reshape(nb, tn_new, D_MODEL),
            st[0], st[1], st[2], st[3], st[4], st[5], st[6], st[7])
```

```python
import functools
import math

import jax
import jax.numpy as jnp
from jax import lax
from jax.experimental import pallas as pl
from jax.experimental.pallas import tpu as pltpu

F32 = jnp.float32
BF16 = jnp.bfloat16

D_MODEL = 1024
N_HEADS = 8
HEAD_DIM = 64
V_DIM = 2 * HEAD_DIM
QK_WIDTH = N_HEADS * 2 * HEAD_DIM
ATTN_WIDTH = N_HEADS * V_DIM
ROPE_THETA = 10000.0
PAGE_SIZE = 128
SSM_WIDTH = 512
SSM_GROUP = 16
N_GROUPS = SSM_WIDTH // SSM_GROUP
STATE_DIM = 64
N_STATE = N_GROUPS * STATE_DIM
SSM_CHUNKS = 4
SSM_CHUNK_IN = SSM_WIDTH // SSM_CHUNKS
SSM_CHUNK_ST = N_STATE // SSM_CHUNKS
SCAN_ROWS = 8
N_KEYS = 128
N_EXPERTS = N_KEYS * N_KEYS
PEER_HEADS = 8
PEER_KEY_DIM = 128
PEER_TOPK = 16
N_MOD = 6
RMS_EPS = 1e-6

VMEM_LIMIT_BYTES = 56 * 1024 * 1024

NEG_INF = float("-inf")
Q_SCALE = (HEAD_DIM ** -0.5) * math.log2(math.e)
NEW_ROWS = 16
ATTN_UNROLL = 8


def _cparams(*sem):
    return pltpu.CompilerParams(dimension_semantics=sem, vmem_limit_bytes=VMEM_LIMIT_BYTES)


def _dot(a, b):
    return jnp.dot(a, b, preferred_element_type=F32)


def _dot_nt(a, b):
    return lax.dot_general(a, b, (((1,), (1,)), ((), ())), preferred_element_type=F32)


def _rms(x, w):
    return x * lax.rsqrt(jnp.mean(x * x, axis=-1, keepdims=True) + RMS_EPS) * w


def _gelu(x):
    return 0.5 * x * (1.0 + jnp.tanh(0.7978845608028654 * (x + 0.044715 * (x * x * x))))


def _sigmoid(x):
    return 1.0 / (1.0 + jnp.exp(-x))


def _mod_spec(rows_per_mod, tm):
    if rows_per_mod == 1:
        return pl.BlockSpec((1, D_MODEL), lambda i: (0, 0))
    return pl.BlockSpec((tm, D_MODEL), lambda i: (i, 0))


def _ada_kernel(c_ref, w_ref, b_ref, o_ref):
    c = c_ref[...]
    s = c * _sigmoid(c)
    o_ref[...] = _dot(s.astype(BF16), w_ref[...].astype(BF16)) + b_ref[...]


def _ada(c_all, w_ada, b_ada):
    rows = c_all.shape[0]
    tn = 1536
    return pl.pallas_call(
        _ada_kernel,
        grid=(N_MOD * D_MODEL // tn,),
        in_specs=[pl.BlockSpec((rows, D_MODEL), lambda j: (0, 0)),
                  pl.BlockSpec((D_MODEL, tn), lambda j: (0, j)),
                  pl.BlockSpec((1, tn), lambda j: (0, j))],
        out_specs=pl.BlockSpec((rows, tn), lambda j: (0, j)),
        out_shape=jax.ShapeDtypeStruct((rows, N_MOD * D_MODEL), F32),
        compiler_params=_cparams("arbitrary"),
        name="ada_mod",
    )(c_all, w_ada, b_ada.reshape(1, -1))


def _inproj_kernel(x_ref, nw_ref, sc_ref, sh_ref, cos_ref, sin_ref, w_ref,
                   q2_ref, k32_ref, kb_ref, v32_ref, vb_ref, u_ref, ga_ref, gs_ref):
    x = x_ref[...]
    h = _rms(x, nw_ref[...]) * (1.0 + sc_ref[...]) + sh_ref[...]
    hb = h.astype(BF16)
    cos = cos_ref[...]
    sin = sin_ref[...]
    lane = lax.broadcasted_iota(jnp.int32, cos.shape, 1)
    first = lane < HEAD_DIM
    low = lane % HEAD_DIM < HEAD_DIM // 2
    W = QK_WIDTH

    def rotate(x):
        partner = jnp.where(low, pltpu.roll(x, V_DIM - HEAD_DIM // 2, 1), pltpu.roll(x, HEAD_DIM // 2, 1))
        return x * cos + partner * sin

    for hd in range(N_HEADS):
        c0 = hd * V_DIM
        qr = rotate(_dot(hb, w_ref[:, c0:c0 + V_DIM])) * Q_SCALE
        q2_ref[0, :, c0:c0 + V_DIM] = jnp.where(first, qr, 0.0).astype(BF16)
        q2_ref[1, :, c0:c0 + V_DIM] = jnp.where(first, 0.0, qr).astype(BF16)
        kr = rotate(_dot(hb, w_ref[:, W + c0:W + c0 + V_DIM]))
        k32_ref[:, c0:c0 + V_DIM] = kr
        kb_ref[:, c0:c0 + V_DIM] = kr.astype(BF16)
    o = 2 * W
    v = _dot(hb, w_ref[:, o:o + ATTN_WIDTH])
    v32_ref[...] = v
    vb_ref[...] = v.astype(BF16)
    o += ATTN_WIDTH
    u_ref[...] = _dot(hb, w_ref[:, o:o + SSM_WIDTH])
    o += SSM_WIDTH
    ga_ref[...] = _dot(hb, w_ref[:, o:o + D_MODEL])
    o += D_MODEL
    gs_ref[...] = _dot(hb, w_ref[:, o:o + D_MODEL])


def _inproj(x, nw, sc, sh, cosf, sinf, w_ext, tm):
    n = x.shape[0]
    wcols = w_ext.shape[1]
    row = lambda i: (i, 0)
    full = lambda i: (0, 0)
    outs = (
        jax.ShapeDtypeStruct((2, n, QK_WIDTH), BF16),
        jax.ShapeDtypeStruct((n, QK_WIDTH), F32),
        jax.ShapeDtypeStruct((n, QK_WIDTH), BF16),
        jax.ShapeDtypeStruct((n, ATTN_WIDTH), F32),
        jax.ShapeDtypeStruct((n, ATTN_WIDTH), BF16),
        jax.ShapeDtypeStruct((n, SSM_WIDTH), F32),
        jax.ShapeDtypeStruct((n, D_MODEL), F32),
        jax.ShapeDtypeStruct((n, D_MODEL), F32),
    )
    out_specs = (
        pl.BlockSpec((2, tm, QK_WIDTH), lambda i: (0, i, 0)),
        pl.BlockSpec((tm, QK_WIDTH), row),
        pl.BlockSpec((tm, QK_WIDTH), row),
        pl.BlockSpec((tm, ATTN_WIDTH), row),
        pl.BlockSpec((tm, ATTN_WIDTH), row),
        pl.BlockSpec((tm, SSM_WIDTH), row),
        pl.BlockSpec((tm, D_MODEL), row),
        pl.BlockSpec((tm, D_MODEL), row),
    )
    return pl.pallas_call(
        _inproj_kernel,
        grid=(n // tm,),
        in_specs=[pl.BlockSpec((tm, D_MODEL), row),
                  pl.BlockSpec((1, D_MODEL), full),
                  _mod_spec(sc.shape[0], tm),
                  _mod_spec(sh.shape[0], tm),
                  pl.BlockSpec((tm, V_DIM), row),
                  pl.BlockSpec((tm, V_DIM), row),
                  pl.BlockSpec((D_MODEL, wcols), full)],
        out_specs=out_specs,
        out_shape=outs,
        compiler_params=_cparams("arbitrary"),
        name="in_proj",
    )(x, nw, sc, sh, cosf, sinf, w_ext)


def _lambda_full(lq1_ref, lk1_ref, lq2_ref, lk2_ref, lam_init):
    s1 = jnp.sum(lq1_ref[...] * lk1_ref[...], axis=-1, keepdims=True)
    s2 = jnp.sum(lq2_ref[...] * lk2_ref[...], axis=-1, keepdims=True)
    return jnp.exp(s1) - jnp.exp(s2) + lam_init


def _subln(o, w, lam_init):
    return _rms(o, w) * (1.0 - lam_init)


def _pattn_kernel(q_ref, qn_ref, k_ref, v_ref, lq1_ref, lk1_ref, lq2_ref, lk2_ref, sub_ref, o_ref,
                  s_ref, m_ref, acc_ref, *, tq, lam_init):
    i = pl.program_id(1)
    m_ref[...] = jnp.full(m_ref.shape, NEG_INF, F32)
    acc_ref[...] = jnp.zeros(acc_ref.shape, F32)
    ones = jnp.ones((tq, V_DIM), BF16)

    def produce(comp, chunk):
        start = pl.multiple_of(chunk * tq, tq)
        s_ref[comp] = _dot_nt(q_ref[comp], k_ref[pl.ds(start, tq), :])

    def consume(comp, chunk, masked):
        start = pl.multiple_of(chunk * tq, tq)
        v1 = jnp.concatenate([v_ref[pl.ds(start, tq), :], ones], axis=1)
        s = s_ref[comp]
        if masked:
            r = lax.broadcasted_iota(jnp.int32, (tq, tq), 0)
            c = lax.broadcasted_iota(jnp.int32, (tq, tq), 1)
            s = jnp.where(c <= r, s, NEG_INF)
        m_old = m_ref[comp]
        m_new = jnp.maximum(m_old, jnp.max(s, axis=-1, keepdims=True))
        p = jnp.exp2(s - jnp.tile(m_new, (1, tq // V_DIM)))
        alpha = jnp.exp2(m_old - m_new)
        acc_ref[comp] = jnp.tile(alpha, (1, 2)) * acc_ref[comp] + _dot(p.astype(BF16), v1)
        m_ref[comp] = m_new

    def pair(chunk, last_next):
        produce(1, chunk)
        consume(0, chunk, False)
        produce(0, last_next)
        consume(1, chunk, False)

    def run(first, count):
        for w in range(count):
            pair(first + w, first + w + 1)

    def body(t, carry):
        run(ATTN_UNROLL * t, ATTN_UNROLL)
        return carry

    @pl.when(i == 0)
    def _():
        produce(0, 0)

    lax.fori_loop(0, i // ATTN_UNROLL, body, 0)
    width = ATTN_UNROLL // 2
    while width >= 1:
        first = (i // (2 * width)) * (2 * width)

        @pl.when(i % (2 * width) >= width)
        def _(first=first, width=width):
            run(first, width)

        width //= 2

    produce(1, i)
    consume(0, i, True)
    s_ref[0] = _dot_nt(qn_ref[0], k_ref[pl.ds(0, tq), :])
    consume(1, i, True)

    lam = _lambda_full(lq1_ref, lk1_ref, lq2_ref, lk2_ref, lam_init)
    o1 = acc_ref[0, :, :V_DIM] / acc_ref[0, :, V_DIM:]
    o2 = acc_ref[1, :, :V_DIM] / acc_ref[1, :, V_DIM:]
    o_ref[...] = _subln(o1 - lam * o2, sub_ref[...], lam_init).astype(BF16)


def _prompt_attention(q2, kb, vb, lq1, lk1, lq2, lk2, subln, lam_init, tq):
    t = kb.shape[0]
    nq = t // tq
    vec = lambda h, i: (0, 0)
    kern = functools.partial(_pattn_kernel, tq=tq, lam_init=lam_init)
    return pl.pallas_call(
        kern,
        grid=(N_HEADS, nq),
        in_specs=[pl.BlockSpec((2, tq, V_DIM), lambda h, i: (0, i, h)),
                  pl.BlockSpec((2, tq, V_DIM), lambda h, i: (0, jnp.minimum(i + 1, nq - 1), h)),
                  pl.BlockSpec((t, V_DIM), lambda h, i: (0, h)),
                  pl.BlockSpec((t, V_DIM), lambda h, i: (0, h)),
                  pl.BlockSpec((1, HEAD_DIM), vec),
                  pl.BlockSpec((1, HEAD_DIM), vec),
                  pl.BlockSpec((1, HEAD_DIM), vec),
                  pl.BlockSpec((1, HEAD_DIM), vec),
                  pl.BlockSpec((1, V_DIM), vec)],
        out_specs=pl.BlockSpec((tq, V_DIM), lambda h, i: (i, h)),
        out_shape=jax.ShapeDtypeStruct((t, ATTN_WIDTH), BF16),
        scratch_shapes=[pltpu.VMEM((2, tq, tq), F32),
                        pltpu.VMEM((2, tq, V_DIM), F32),
                        pltpu.VMEM((2, tq, 2 * V_DIM), F32)],
        compiler_params=_cparams("arbitrary", "arbitrary"),
        name="prompt_attn",
    )(q2, q2, kb, vb, lq1, lk1, lq2, lk2, subln)


def _sattn_kernel(pt_ref, q_ref, kn_ref, vn_ref, lq1_ref, lk1_ref, lq2_ref, lk2_ref, sub_ref, *rest,
                  pages_per_step, n_new, lam_init):
    g_pages = pages_per_step
    k_refs = rest[:g_pages]
    v_refs = rest[g_pages:2 * g_pages]
    o_ref = rest[2 * g_pages]
    m_ref, l_ref, acc_ref, bias_ref = rest[2 * g_pages + 1:]
    b = pl.program_id(0)
    g = pl.program_id(1)
    q = q_ref[0]
    rows_per_head = 2 * n_new

    @pl.when((b == 0) & (g == 0))
    def _():
        r = lax.broadcasted_iota(jnp.int32, bias_ref.shape, 0) // rows_per_head
        c = lax.broadcasted_iota(jnp.int32, bias_ref.shape, 1) % N_HEADS
        bias_ref[...] = jnp.where(r == c, 0.0, NEG_INF)

    @pl.when(g == 0)
    def _():
        m_ref[...] = jnp.full(m_ref.shape, NEG_INF, F32)
        l_ref[...] = jnp.zeros(l_ref.shape, F32)
        acc_ref[...] = jnp.zeros(acc_ref.shape, F32)

    def update(s, v):
        m_old = m_ref[...]
        m_new = jnp.maximum(m_old, jnp.max(s, axis=-1, keepdims=True))
        p = jnp.exp2(s - m_new)
        alpha = jnp.exp2(m_old - m_new)
        l_ref[...] = alpha * l_ref[...] + jnp.sum(p, axis=-1, keepdims=True)
        acc_ref[...] = alpha * acc_ref[...] + _dot(p.astype(BF16), v)
        m_ref[...] = m_new

    half = g_pages // 2
    scores = []
    for grp in range(2):
        k = jnp.concatenate([r[0].astype(BF16) for r in k_refs[grp * half:(grp + 1) * half]], axis=0)
        scores.append(_dot_nt(q, k))
    for grp in range(2):
        v = jnp.concatenate([r[0].astype(BF16) for r in v_refs[grp * half:(grp + 1) * half]], axis=0)
        update(scores[grp] + bias_ref[...], v)

    @pl.when(g == pl.num_programs(1) - 1)
    def _():
        kn = kn_ref[0].astype(BF16)
        vn = vn_ref[0].astype(BF16)
        s = _dot_nt(q, kn)
        row = lax.broadcasted_iota(jnp.int32, s.shape, 0)
        col = lax.broadcasted_iota(jnp.int32, s.shape, 1)
        keep = (row // rows_per_head == col % N_HEADS) & (col // N_HEADS <= row % n_new)
        update(jnp.where(keep, s, NEG_INF), vn)
        lam = _lambda_full(lq1_ref, lk1_ref, lq2_ref, lk2_ref, lam_init)
        o = acc_ref[...] / l_ref[...]
        d = o - lam * pltpu.roll(o, o.shape[0] - n_new, 0)
        o_ref[0] = _subln(d, sub_ref[...], lam_init)


def _sample_attention(page_table, q_rows, k_new, v_new, cache_k, cache_v, lq1, lk1, lq2, lk2, subln,
                      lam_init, n_new, pages_per_step):
    nb, n_pages = page_table.shape
    gp = pages_per_step
    assert gp % 2 == 0
    rows = q_rows.shape[1]
    page_rows = PAGE_SIZE * N_HEADS
    new_rows = k_new.shape[1]
    vec = lambda b, g, pt: (0, 0)

    def page_spec(r):
        return pl.BlockSpec((1, page_rows, V_DIM),
                            lambda b, g, pt: (pt[b * n_pages + g * gp + r], 0, 0))

    kern = functools.partial(_sattn_kernel, pages_per_step=gp, n_new=n_new, lam_init=lam_init)
    grid_spec = pltpu.PrefetchScalarGridSpec(
        num_scalar_prefetch=1,
        grid=(nb, n_pages // gp),
        in_specs=[pl.BlockSpec((1, rows, V_DIM), lambda b, g, pt: (b, 0, 0)),
                  pl.BlockSpec((1, new_rows, V_DIM), lambda b, g, pt: (b, 0, 0)),
                  pl.BlockSpec((1, new_rows, V_DIM), lambda b, g, pt: (b, 0, 0)),
                  pl.BlockSpec((1, HEAD_DIM), vec),
                  pl.BlockSpec((1, HEAD_DIM), vec),
                  pl.BlockSpec((1, HEAD_DIM), vec),
                  pl.BlockSpec((1, HEAD_DIM), vec),
                  pl.BlockSpec((1, V_DIM), vec)]
                 + [page_spec(r) for r in range(gp)] + [page_spec(r) for r in range(gp)],
        out_specs=pl.BlockSpec((1, rows, V_DIM), lambda b, g, pt: (b, 0, 0)),
        scratch_shapes=[pltpu.VMEM((rows, 1), F32),
                        pltpu.VMEM((rows, 1), F32),
                        pltpu.VMEM((rows, V_DIM), F32),
                        pltpu.VMEM((rows, (gp // 2) * page_rows), F32)],
    )
    return pl.pallas_call(
        kern,
        grid_spec=grid_spec,
        out_shape=jax.ShapeDtypeStruct((nb, rows, V_DIM), F32),
        compiler_params=_cparams("arbitrary", "arbitrary"),
        name="sample_attn",
    )(page_table.reshape(-1), q_rows, k_new, v_new, lq1, lk1, lq2, lk2, subln,
      *([cache_k] * gp), *([cache_v] * gp))


def _ssm_disc_kernel(lr_ref, li_ref, ls_ref, are_ref, aim_ref, fre_ref, fim_ref, *, n_pow):
    lr = lr_ref[...]
    li = li_ref[...]
    dt = jnp.exp(ls_ref[...])
    mag = jnp.exp(lr * dt)
    a_re = mag * jnp.cos(li * dt)
    a_im = mag * jnp.sin(li * dt)
    den = lr * lr + li * li
    nr = a_re - 1.0
    ni = a_im
    fre_ref[...] = (nr * lr + ni * li) / den
    fim_ref[...] = (ni * lr - nr * li) / den
    pr, pi = a_re, a_im
    for kk in range(n_pow):
        are_ref[kk] = pr
        aim_ref[kk] = pi
        pr, pi = pr * a_re - pi * a_im, pr * a_im + pi * a_re


def _ssm_bb_kernel(fre_ref, fim_ref, bre_ref, bim_ref, ore_ref, oim_ref):
    fr = fre_ref[...]
    fi = fim_ref[...]
    br = bre_ref[...]
    bi = bim_ref[...]
    ore_ref[...] = fr * br - fi * bi
    oim_ref[...] = fr * bi + fi * br


def _ssm_prepare(lam_re, lam_im, log_step, b_re, b_im, c_re, c_im, n_pow):
    gshape = jax.ShapeDtypeStruct((N_GROUPS, STATE_DIM), F32)
    pshape = jax.ShapeDtypeStruct((n_pow, N_GROUPS, STATE_DIM), F32)
    a_re, a_im, f_re, f_im = pl.pallas_call(
        functools.partial(_ssm_disc_kernel, n_pow=n_pow),
        out_shape=(pshape, pshape, gshape, gshape),
        name="ssm_discretise",
    )(lam_re, lam_im, log_step.reshape(N_GROUPS, 1))
    bshape = jax.ShapeDtypeStruct((N_STATE, SSM_GROUP), F32)
    bb_re, bb_im = pl.pallas_call(
        _ssm_bb_kernel, out_shape=(bshape, bshape), name="ssm_input_matrix",
    )(f_re.reshape(N_STATE, 1), f_im.reshape(N_STATE, 1),
      b_re.reshape(N_STATE, SSM_GROUP), b_im.reshape(N_STATE, SSM_GROUP))
    gc = N_GROUPS // SSM_CHUNKS
    eye = jnp.eye(gc, dtype=F32)

    def in_mat(bb):
        b4 = bb.reshape(SSM_CHUNKS, gc, STATE_DIM, SSM_GROUP)
        m = jnp.einsum('cgpi,gh->cgihp', b4, eye)
        return m.reshape(SSM_CHUNKS, gc * SSM_GROUP, gc * STATE_DIM)

    def out_mat(cc):
        c4 = cc.reshape(SSM_CHUNKS, gc, SSM_GROUP, STATE_DIM)
        m = jnp.einsum('cgip,gh->cgphi', c4, eye)
        return m.reshape(SSM_CHUNKS, gc * STATE_DIM, gc * SSM_GROUP)

    b_mat = jnp.concatenate([in_mat(bb_re), in_mat(bb_im)], axis=-1).astype(BF16)
    c_mat = jnp.concatenate([out_mat(c_re), out_mat(-c_im)], axis=-2).astype(BF16)
    return (a_re.reshape(n_pow, N_STATE), a_im.reshape(n_pow, N_STATE), b_mat, c_mat)


def _shift_rows(x, s, row):
    return jnp.where(row >= s, pltpu.roll(x, s, 0), 0.0)


def _ssm_scan_kernel(u_ref, are_ref, aim_ref, b_ref, c_ref, d_ref, wg_ref,
                     y_ref, hre_ref, him_ref, cre_ref, cim_ref, *, tm):
    @pl.when(pl.program_id(0) == 0)
    def _():
        cre_ref[...] = jnp.zeros(cre_ref.shape, F32)
        cim_ref[...] = jnp.zeros(cim_ref.shape, F32)

    u = u_ref[...]
    ub = u.astype(BF16)
    seg = tm // SCAN_ROWS
    pr_i = lax.broadcasted_iota(jnp.int32, (tm, tm), 0)
    pc_i = lax.broadcasted_iota(jnp.int32, (tm, tm), 1)
    perm = (pc_i == seg * (pr_i % SCAN_ROWS) + pr_i // SCAN_ROWS).astype(BF16)
    perm_t = (pr_i == seg * (pc_i % SCAN_ROWS) + pc_i // SCAN_ROWS).astype(BF16)
    ubp = _dot(perm, ub).astype(BF16)
    row = lax.broadcasted_iota(jnp.int32, (SCAN_ROWS, SSM_CHUNK_ST), 0)
    ys = []
    for c in range(SSM_CHUNKS):
        lanes = slice(c * SSM_CHUNK_ST, (c + 1) * SSM_CHUNK_ST)
        bu = _dot(ubp[:, c * SSM_CHUNK_IN:(c + 1) * SSM_CHUNK_IN], b_ref[c])
        a_r = are_ref[0:1, lanes]
        a_i = aim_ref[0:1, lanes]
        loc_r, loc_i = [], []
        x_r = x_i = None
        for gi in range(seg):
            rows = slice(gi * SCAN_ROWS, (gi + 1) * SCAN_ROWS)
            b_r = bu[rows, :SSM_CHUNK_ST]
            b_i = bu[rows, SSM_CHUNK_ST:]
            if gi == 0:
                x_r, x_i = b_r, b_i
            else:
                x_r, x_i = b_r + (a_r * x_r - a_i * x_i), b_i + (a_r * x_i + a_i * x_r)
            loc_r.append(x_r)
            loc_i.append(x_i)
        e_r, e_i = x_r, x_i
        p_r = are_ref[seg - 1:seg, lanes]
        p_i = aim_ref[seg - 1:seg, lanes]
        h_r = cre_ref[0:1, lanes]
        h_i = cim_ref[0:1, lanes]
        first = row == 0
        e_r = e_r + jnp.where(first, p_r * h_r - p_i * h_i, 0.0)
        e_i = e_i + jnp.where(first, p_r * h_i + p_i * h_r, 0.0)
        s = 1
        while s < SCAN_ROWS:
            sr = _shift_rows(e_r, s, row)
            si = _shift_rows(e_i, s, row)
            e_r, e_i = e_r + (p_r * sr - p_i * si), e_i + (p_r * si + p_i * sr)
            p_r, p_i = p_r * p_r - p_i * p_i, 2.0 * p_r * p_i
            s *= 2
        cre_ref[0:1, lanes] = e_r[SCAN_ROWS - 1:SCAN_ROWS]
        cim_ref[0:1, lanes] = e_i[SCAN_ROWS - 1:SCAN_ROWS]
        in_r = jnp.where(first, h_r, pltpu.roll(e_r, 1, 0))
        in_i = jnp.where(first, h_i, pltpu.roll(e_i, 1, 0))
        out_r, out_i = [], []
        for gi in range(seg):
            q_r = are_ref[gi:gi + 1, lanes]
            q_i = aim_ref[gi:gi + 1, lanes]
            out_r.append(loc_r[gi] + (q_r * in_r - q_i * in_i))
            out_i.append(loc_i[gi] + (q_r * in_i + q_i * in_r))
        hcat_p = jnp.concatenate([jnp.concatenate(out_r, axis=0), jnp.concatenate(out_i, axis=0)],
                                 axis=1).astype(BF16)
        hcat = _dot(perm_t, hcat_p).astype(BF16)
        ys.append(_dot(hcat, c_ref[c]))
    y = jnp.concatenate(ys, axis=1) + d_ref[...] * u
    y = _gelu(y)
    y = y * _sigmoid(_dot(y.astype(BF16), wg_ref[...]))
    y_ref[...] = y.astype(BF16)
    hre_ref[...] = cre_ref[0:1, :]
    him_ref[...] = cim_ref[0:1, :]


def _ssm_prompt(u, a_re, a_im, b_mat, c_mat, d, wg, tm):
    t = u.shape[0]
    n_pow = a_re.shape[0]
    full2 = lambda i: (0, 0)
    full3 = lambda i: (0, 0, 0)
    kern = functools.partial(_ssm_scan_kernel, tm=tm)
    return pl.pallas_call(
        kern,
        grid=(t // tm,),
        in_specs=[pl.BlockSpec((tm, SSM_WIDTH), lambda i: (i, 0)),
                  pl.BlockSpec((n_pow, N_STATE), full2),
                  pl.BlockSpec((n_pow, N_STATE), full2),
                  pl.BlockSpec(b_mat.shape, full3),
                  pl.BlockSpec(c_mat.shape, full3),
                  pl.BlockSpec((1, SSM_WIDTH), full2),
                  pl.BlockSpec((SSM_WIDTH, SSM_WIDTH), full2)],
        out_specs=(pl.BlockSpec((tm, SSM_WIDTH), lambda i: (i, 0)),
                   pl.BlockSpec((1, N_STATE), full2),
                   pl.BlockSpec((1, N_STATE), full2)),
        out_shape=(jax.ShapeDtypeStruct((t, SSM_WIDTH), BF16),
                   jax.ShapeDtypeStruct((1, N_STATE), F32),
                   jax.ShapeDtypeStruct((1, N_STATE), F32)),
        scratch_shapes=[pltpu.VMEM((8, N_STATE), F32), pltpu.VMEM((8, N_STATE), F32)],
        compiler_params=_cparams("arbitrary"),
        name="ssm_prompt",
    )(u, a_re, a_im, b_mat, c_mat, d, wg)


def _ssm_step_kernel(u_ref, h0re_ref, h0im_ref, are_ref, aim_ref, b_ref, c_ref, d_ref, wg_ref,
                     y_ref, hre_ref, him_ref, *, n_steps):
    a_r = are_ref[0:1, :]
    a_i = aim_ref[0:1, :]
    h_r = h0re_ref[...]
    h_i = h0im_ref[...]
    for t in range(n_steps):
        u = u_ref[t]
        ub = u.astype(BF16)
        brs, bis = [], []
        for c in range(SSM_CHUNKS):
            bu = _dot(ub[:, c * SSM_CHUNK_IN:(c + 1) * SSM_CHUNK_IN], b_ref[c])
            brs.append(bu[:, :SSM_CHUNK_ST])
            bis.append(bu[:, SSM_CHUNK_ST:])
        bu_r = jnp.concatenate(brs, axis=1)
        bu_i = jnp.concatenate(bis, axis=1)
        h_r, h_i = a_r * h_r - a_i * h_i + bu_r, a_r * h_i + a_i * h_r + bu_i
        ys = []
        for c in range(SSM_CHUNKS):
            lanes = slice(c * SSM_CHUNK_ST, (c + 1) * SSM_CHUNK_ST)
            hcat = jnp.concatenate([h_r[:, lanes], h_i[:, lanes]], axis=1).astype(BF16)
            ys.append(_dot(hcat, c_ref[c]))
        y = jnp.concatenate(ys, axis=1) + d_ref[...] * u
        y = _gelu(y)
        y = y * _sigmoid(_dot(y.astype(BF16), wg_ref[...]))
        y_ref[t] = y.astype(BF16)
    hre_ref[...] = h_r
    him_ref[...] = h_i


def _ssm_sample(u_tb, h0_re, h0_im, a_re, a_im, b_mat, c_mat, d, wg):
    n_steps, nb, _ = u_tb.shape
    kern = functools.partial(_ssm_step_kernel, n_steps=n_steps)
    return pl.pallas_call(
        kern,
        out_shape=(jax.ShapeDtypeStruct((n_steps, nb, SSM_WIDTH), BF16),
                   jax.ShapeDtypeStruct((nb, N_STATE), F32),
                   jax.ShapeDtypeStruct((nb, N_STATE), F32)),
        compiler_params=pltpu.CompilerParams(vmem_limit_bytes=VMEM_LIMIT_BYTES),
        name="ssm_sample",
    )(u_tb, h0_re, h0_im, a_re, a_im, b_mat, c_mat, d, wg)


def _kth_largest_rows(c, k):
    cnt = jnp.zeros((1, c.shape[1]), F32)
    tau = jnp.full((1, c.shape[1]), NEG_INF, F32)
    for _ in range(k):
        m = jnp.max(c, axis=0, keepdims=True)
        eq = c == m
        tau = jnp.where(cnt < k, m, tau)
        cnt = cnt + jnp.sum(eq.astype(F32), axis=0, keepdims=True)
        c = jnp.where(eq, NEG_INF, c)
    return tau


def _top_rows(a, k, out_ref):
    for r in range(k):
        m = jnp.max(a, axis=0, keepdims=True)
        out_ref[r:r + 1, :] = m
        a = jnp.where(a == m, NEG_INF, a)


def _mix_kernel(x_ref, o_ref, ys_ref, ga_ref, gs_ref, g1_ref, sc2_ref, sh2_ref, npost_ref, npre_ref,
                wao_ref, wso_ref, wout_ref, wq_ref, skt_ref,
                x1_ref, h2_ref, c1_ref, e1_ref, r2_ref, e2_ref, v1_ref, v2_ref, cnt_ref):
    y_attn = _dot(o_ref[...], wao_ref[...])
    y_ssm = _dot(ys_ref[...], wso_ref[...])
    merged = _sigmoid(ga_ref[...]) * y_attn + _sigmoid(gs_ref[...]) * y_ssm
    mix = _dot(merged.astype(BF16), wout_ref[...])
    x1 = x_ref[...] + g1_ref[...] * _rms(mix, npost_ref[...])
    x1_ref[...] = x1
    h2 = _rms(x1, npre_ref[...]) * (1.0 + sc2_ref[...]) + sh2_ref[...]
    h2b = h2.astype(BF16)
    h2_ref[...] = h2b
    qp = _dot(h2b, wq_ref[...]).astype(BF16)
    s_all = _dot_nt(skt_ref[...], qp)
    for hd in range(PEER_HEADS):
        r0 = hd * 2 * N_KEYS
        k0 = hd * N_KEYS
        s1 = s_all[r0:r0 + N_KEYS]
        s2 = s_all[r0 + N_KEYS:r0 + 2 * N_KEYS]
        _top_rows(s1, PEER_TOPK, v1_ref)
        _top_rows(s2, PEER_TOPK, v2_ref)
        v1lo = v1_ref[0:8, :]
        v2lo = v2_ref[0:8, :]
        r8 = lax.broadcasted_iota(jnp.int32, v1lo.shape, 0)
        m1 = v1_ref[0:1, :]
        m2 = v2_ref[0:1, :]
        pieces = [
            v1lo + m2,
            v1_ref[8:16, :] + m2,
            v1lo + v2_ref[1:2, :],
            jnp.where((r8 >= 2) & (r8 <= 4), v1lo + v2_ref[2:3, :], NEG_INF),
            jnp.where((r8 >= 2) & (r8 <= 3), v1lo + v2_ref[3:4, :], NEG_INF),
            jnp.where(r8 == 2, v1lo + v2_ref[4:5, :], NEG_INF),
            jnp.where(r8 >= 2, m1 + v2lo, NEG_INF),
            m1 + v2_ref[8:16, :],
            jnp.where(r8 >= 2, v1_ref[1:2, :] + v2lo, NEG_INF),
        ]
        cand = jnp.concatenate(pieces, axis=0)
        tau = _kth_largest_rows(cand, PEER_TOPK)
        z = jnp.sum(jnp.where(cand >= tau, jnp.exp(cand - (m1 + m2)), 0.0), axis=0, keepdims=True)
        sel = [(p >= tau).astype(F32) for p in pieces]
        n0 = jnp.sum(sel[6] + sel[7], axis=0, keepdims=True)
        n1 = jnp.sum(sel[8], axis=0, keepdims=True)
        cnt_ref[0:8, :] = (sel[0] + sel[2] + sel[3] + sel[4] + sel[5]
                           + jnp.where(r8 == 0, n0, 0.0) + jnp.where(r8 == 1, n1, 0.0))
        cnt_ref[8:16, :] = sel[1]
        c1 = jnp.zeros(s1.shape, F32)
        r2 = jnp.full(s2.shape, float(N_KEYS - 1), F32)
        for r in range(PEER_TOPK):
            c1 = jnp.where(s1 == v1_ref[r:r + 1, :], cnt_ref[r:r + 1, :], c1)
            r2 = jnp.where(s2 == v2_ref[r:r + 1, :], float(r), r2)
        c1_ref[k0:k0 + N_KEYS, :] = c1
        e1_ref[k0:k0 + N_KEYS, :] = jnp.exp(s1 - m1)
        r2_ref[k0:k0 + N_KEYS, :] = r2.astype(BF16)
        e2_ref[k0:k0 + N_KEYS, :] = (jnp.exp(s2 - m2) / z).astype(BF16)


def _mix(x, ob, ysb, ga, gs, g1, sc2, sh2, npost, npre, wao, wso, wout, wq, skt, tm):
    n = x.shape[0]
    row = lambda i: (i, 0)
    col = lambda i: (0, i)
    full = lambda i: (0, 0)
    nrow = 2 * N_KEYS * PEER_HEADS
    krow = N_KEYS * PEER_HEADS
    return pl.pallas_call(
        _mix_kernel,
        grid=(n // tm,),
        in_specs=[pl.BlockSpec((tm, D_MODEL), row),
                  pl.BlockSpec((tm, ATTN_WIDTH), row),
                  pl.BlockSpec((tm, SSM_WIDTH), row),
                  pl.BlockSpec((tm, D_MODEL), row),
                  pl.BlockSpec((tm, D_MODEL), row),
                  _mod_spec(g1.shape[0], tm),
                  _mod_spec(sc2.shape[0], tm),
                  _mod_spec(sh2.shape[0], tm),
                  pl.BlockSpec((1, D_MODEL), full),
                  pl.BlockSpec((1, D_MODEL), full),
                  pl.BlockSpec((ATTN_WIDTH, D_MODEL), full),
                  pl.BlockSpec((SSM_WIDTH, D_MODEL), full),
                  pl.BlockSpec((D_MODEL, D_MODEL), full),
                  pl.BlockSpec((D_MODEL, PEER_HEADS * PEER_KEY_DIM), full),
                  pl.BlockSpec((nrow, PEER_HEADS * PEER_KEY_DIM), full)],
        out_specs=(pl.BlockSpec((tm, D_MODEL), row),
                   pl.BlockSpec((tm, D_MODEL), row),
                   pl.BlockSpec((krow, tm), col),
                   pl.BlockSpec((krow, tm), col),
                   pl.BlockSpec((krow, tm), col),
                   pl.BlockSpec((krow, tm), col)),
        out_shape=(jax.ShapeDtypeStruct((n, D_MODEL), F32),
                   jax.ShapeDtypeStruct((n, D_MODEL), BF16),
                   jax.ShapeDtypeStruct((krow, n), F32),
                   jax.ShapeDtypeStruct((krow, n), F32),
                   jax.ShapeDtypeStruct((krow, n), BF16),
                   jax.ShapeDtypeStruct((krow, n), BF16)),
        scratch_shapes=[pltpu.VMEM((PEER_TOPK, tm), F32)] * 3,
        compiler_params=_cparams("arbitrary"),
        name="mix_route",
    )(x, ob, ysb, ga, gs, g1, sc2, sh2, npost, npre, wao, wso, wout, wq, skt)


BF16_ROWS = 16
PEER_CHUNKS_PER_STEP = 2


def _tile_bf16(row):
    return jnp.broadcast_to(row, (BF16_ROWS, row.shape[1])).astype(BF16)


def _peer_kernel(h2_ref, u0_ref, *rest, ec, n_sub):
    u_refs = rest[:n_sub]
    (vt_ref, c1_ref, e1_ref, r2_ref, e2_ref, x1_ref, g2_ref, nw_ref,
     y_ref, acc_ref, act0_ref, act1_ref) = rest[n_sub:]
    acts = (act0_ref, act1_ref)
    g = pl.program_id(1)
    h2 = h2_ref[...]
    blocks = ec // N_KEYS

    @pl.when(g == 0)
    def _():
        acc_ref[...] = jnp.zeros(acc_ref.shape, F32)
        act0_ref[...] = _dot_nt(u0_ref[...], h2)

    def consume(act_ref, chunk, col0):
        ws = []
        for ii in range(blocks):
            key1 = chunk * blocks + ii
            a = _gelu(act_ref[ii * N_KEYS:(ii + 1) * N_KEYS, :].astype(BF16))
            tiles = N_KEYS // BF16_ROWS
            totals = [None] * tiles
            for hd in range(PEER_HEADS):
                cnt = _tile_bf16(c1_ref[pl.ds(hd * N_KEYS + key1, 1), :])
                e1 = _tile_bf16(e1_ref[pl.ds(hd * N_KEYS + key1, 1), :])
                for t in range(tiles):
                    k0 = hd * N_KEYS + t * BF16_ROWS
                    r2 = r2_ref[k0:k0 + BF16_ROWS, :]
                    e2 = e2_ref[k0:k0 + BF16_ROWS, :]
                    term = jnp.where(r2 < cnt, e2, jnp.zeros_like(e2)) * e1
                    totals[t] = term if totals[t] is None else totals[t] + term
            for t in range(tiles):
                ws.append(totals[t] * a[t * BF16_ROWS:(t + 1) * BF16_ROWS])
        w = jnp.concatenate(ws, axis=0)
        acc_ref[...] += _dot(vt_ref[:, col0:col0 + ec], w)

    for s in range(n_sub):
        acts[(s + 1) % 2][...] = _dot_nt(u_refs[s][...], h2)
        consume(acts[s % 2], n_sub * g + s, s * ec)

    @pl.when(g == pl.num_programs(1) - 1)
    def _():
        f = acc_ref[...].T
        y_ref[...] = x1_ref[...] + g2_ref[...] * _rms(f, nw_ref[...])


def _peer(h2b, u_bf, vt_bf, c1, e1, r2, e2, x1, g2, nw, tn, ec):
    n = h2b.shape[0]
    krow = c1.shape[0]
    n_chunks = N_EXPERTS // ec
    n_sub = PEER_CHUNKS_PER_STEP
    kern = functools.partial(_peer_kernel, ec=ec, n_sub=n_sub)
    g2_spec = (pl.BlockSpec((1, D_MODEL), lambda t, c: (0, 0)) if g2.shape[0] == 1
               else pl.BlockSpec((tn, D_MODEL), lambda t, c: (t, 0)))

    def lookahead_spec(s):
        return pl.BlockSpec((ec, D_MODEL), lambda t, c: (jnp.minimum(n_sub * c + s + 1, n_chunks - 1), 0))

    return pl.pallas_call(
        kern,
        grid=(n // tn, n_chunks // n_sub),
        in_specs=[pl.BlockSpec((tn, D_MODEL), lambda t, c: (t, 0)),
                  pl.BlockSpec((ec, D_MODEL), lambda t, c: (0, 0))]
                 + [lookahead_spec(s) for s in range(n_sub)]
                 + [pl.BlockSpec((D_MODEL, n_sub * ec), lambda t, c: (0, c)),
                  pl.BlockSpec((krow, tn), lambda t, c: (0, t)),
                  pl.BlockSpec((krow, tn), lambda t, c: (0, t)),
                  pl.BlockSpec((krow, tn), lambda t, c: (0, t)),
                  pl.BlockSpec((krow, tn), lambda t, c: (0, t)),
                  pl.BlockSpec((tn, D_MODEL), lambda t, c: (t, 0)),
                  g2_spec,
                  pl.BlockSpec((1, D_MODEL), lambda t, c: (0, 0))],
        out_specs=pl.BlockSpec((tn, D_MODEL), lambda t, c: (t, 0)),
        out_shape=jax.ShapeDtypeStruct((n, D_MODEL), F32),
        scratch_shapes=[pltpu.VMEM((D_MODEL, tn), F32),
                        pltpu.VMEM((ec, tn), F32), pltpu.VMEM((ec, tn), F32)],
        compiler_params=_cparams("arbitrary", "arbitrary"),
        name="peer_experts",
    )(h2b, u_bf, *([u_bf] * n_sub), vt_bf, c1, e1, r2, e2, x1, g2, nw)


def _rope_tables(pos):
    half = HEAD_DIM // 2
    inv = ROPE_THETA ** (-jnp.arange(half, dtype=F32) / half)
    ang = pos.astype(F32)[:, None] * inv[None, :]
    cos = jnp.cos(ang)
    sin = jnp.sin(ang)
    cosf = jnp.concatenate([cos, cos, cos, cos], axis=1)
    sinf = jnp.concatenate([-sin, sin, -sin, sin], axis=1)
    return cosf, sinf


def _tile(n, pref):
    t = min(n, pref)
    assert n % t == 0, (n, pref)
    return t


def kernel(x_prompt, x_sample, cache_k, cache_v, state_ssm_re, state_ssm_im, page_table, c_prompt, c_sample, w_ada, b_ada, norm_pre_mix, norm_post_mix, norm_pre_ffn, norm_post_ffn, w_in, lambda_q1, lambda_k1, lambda_q2, lambda_k2, subln, w_attn_out, ssm_lambda_re, ssm_lambda_im, ssm_log_step, ssm_b_re, ssm_b_im, ssm_c_re, ssm_c_im, ssm_d, w_glu, w_ssm_out, w_out, peer_w_query, peer_sub_keys, peer_u, peer_v):
    depth = w_ada.shape[0]
    bp, tp, _ = x_prompt.shape
    nb, tn_new, _ = x_sample.shape
    assert bp == 1 and 2 * tn_new == 8
    n_pages = page_table.shape[1]
    past_len = n_pages * PAGE_SIZE
    n_phys = cache_k.shape[1]
    ns = nb * tn_new

    pos_p = jnp.arange(tp, dtype=jnp.int32)
    pos_s = jnp.tile(past_len + jnp.arange(tn_new, dtype=jnp.int32), nb)
    cos_p, sin_p = _rope_tables(pos_p)
    cos_s, sin_s = _rope_tables(pos_s)

    tm_p = _tile(tp, 256)
    tq = _tile(tp, 512)
    tscan = _tile(tp, 128)
    assert tscan % SCAN_ROWS == 0
    pages_per_step = next(p for p in (16, 8, 4, 2) if n_pages % p == 0)

    yp = x_prompt.reshape(tp, D_MODEL)
    ys = x_sample.reshape(ns, D_MODEL)
    c_all = jnp.concatenate([c_prompt, c_sample], axis=0)
    pad = (-c_all.shape[0]) % 8
    c_all = jnp.pad(c_all, ((0, pad), (0, 0)))

    outs = [[] for _ in range(8)]
    for l in range(depth):
        lam_init = 0.8 - 0.6 * math.exp(-0.3 * l)
        mod = _ada(c_all, w_ada[l], b_ada[l])
        mod_p = [mod[0:1, j * D_MODEL:(j + 1) * D_MODEL] for j in range(N_MOD)]
        mod_s = [jnp.repeat(mod[1:1 + nb, j * D_MODEL:(j + 1) * D_MODEL], tn_new, axis=0)
                 for j in range(N_MOD)]

        w_ext = w_in[l].astype(BF16)
        a_re, a_im, b_mat, c_mat = _ssm_prepare(ssm_lambda_re[l], ssm_lambda_im[l], ssm_log_step[l],
                                                ssm_b_re[l], ssm_b_im[l], ssm_c_re[l], ssm_c_im[l],
                                                tscan // SCAN_ROWS)
        d_row = ssm_d[l].reshape(1, SSM_WIDTH)
        wg = w_glu[l].astype(BF16)
        wao = w_attn_out[l].astype(BF16)
        wso = w_ssm_out[l].astype(BF16)
        wout = w_out[l].astype(BF16)
        wq = peer_w_query[l].astype(BF16)
        sk = peer_sub_keys[l]
        eye = jnp.eye(PEER_HEADS * 2, dtype=F32)
        skt = jnp.einsum('bkd,bc->bkcd', sk.reshape(PEER_HEADS * 2, N_KEYS, PEER_KEY_DIM // 2), eye)
        skt = skt.reshape(PEER_HEADS * 2 * N_KEYS, PEER_HEADS * PEER_KEY_DIM).astype(BF16)
        u_bf = peer_u[l].astype(BF16)
        vt_bf = peer_v[l].T.astype(BF16)
        lq1 = lambda_q1[l].reshape(1, HEAD_DIM)
        lk1 = lambda_k1[l].reshape(1, HEAD_DIM)
        lq2 = lambda_q2[l].reshape(1, HEAD_DIM)
        lk2 = lambda_k2[l].reshape(1, HEAD_DIM)
        sub = subln[l].reshape(1, V_DIM)
        npre_mix = norm_pre_mix[l].reshape(1, D_MODEL)
        npost_mix = norm_post_mix[l].reshape(1, D_MODEL)
        npre_ffn = norm_pre_ffn[l].reshape(1, D_MODEL)
        npost_ffn = norm_post_ffn[l].reshape(1, D_MODEL)

        sh1, sc1, g1, sh2, sc2, g2 = mod_p
        q2, k32, kb, v32, vb, u, ga, gs = _inproj(yp, npre_mix, sc1, sh1, cos_p, sin_p, w_ext, tm_p)
        ob = _prompt_attention(q2, kb, vb, lq1, lk1, lq2, lk2, sub, lam_init, tq)
        ysb, hre, him = _ssm_prompt(u, a_re, a_im, b_mat, c_mat, d_row, wg, tscan)
        x1, h2b, c1, e1, r2, e2 = _mix(yp, ob, ysb, ga, gs, g1, sc2, sh2, npost_mix, npre_ffn,
                                      wao, wso, wout, wq, skt, tm_p)
        yp = _peer(h2b, u_bf, vt_bf, c1, e1, r2, e2, x1, g2, npost_ffn, _tile(tp, 512), 512)
        outs[0].append(k32.reshape(1, tp, N_HEADS, 2 * HEAD_DIM))
        outs[1].append(v32.reshape(1, tp, N_HEADS, V_DIM))
        outs[2].append(hre.reshape(1, N_GROUPS, STATE_DIM))
        outs[3].append(him.reshape(1, N_GROUPS, STATE_DIM))

        sh1, sc1, g1, sh2, sc2, g2 = mod_s
        q2, k32, _, v32, _, u, ga, gs = _inproj(ys, npre_mix, sc1, sh1, cos_s, sin_s, w_ext, ns)
        q_rows = q2.reshape(2, nb, tn_new, N_HEADS, V_DIM).transpose(1, 3, 0, 2, 4)
        q_rows = q_rows.reshape(nb, N_HEADS * 2 * tn_new, V_DIM)
        k_new = jnp.pad(k32.reshape(nb, tn_new * N_HEADS, V_DIM),
                        ((0, 0), (0, (NEW_ROWS - tn_new) * N_HEADS), (0, 0)))
        v_new = jnp.pad(v32.reshape(nb, tn_new * N_HEADS, V_DIM),
                        ((0, 0), (0, (NEW_ROWS - tn_new) * N_HEADS), (0, 0)))
        ck = cache_k.reshape(depth * n_phys, PAGE_SIZE * N_HEADS, V_DIM)
        cv = cache_v.reshape(depth * n_phys, PAGE_SIZE * N_HEADS, V_DIM)
        o_s = _sample_attention(page_table + l * n_phys, q_rows, k_new, v_new, ck, cv,
                                lq1, lk1, lq2, lk2, sub, lam_init, tn_new, pages_per_step)
        ob = o_s.reshape(nb, N_HEADS, 2, tn_new, V_DIM)[:, :, 0].transpose(0, 2, 1, 3)
        ob = ob.reshape(ns, ATTN_WIDTH).astype(BF16)
        u_tb = u.reshape(nb, tn_new, SSM_WIDTH).transpose(1, 0, 2)
        y_tb, hre_s, him_s = _ssm_sample(u_tb, state_ssm_re[l].reshape(nb, N_STATE),
                                         state_ssm_im[l].reshape(nb, N_STATE),
                                         a_re, a_im, b_mat, c_mat, d_row, wg)
        ysb = y_tb.transpose(1, 0, 2).reshape(ns, SSM_WIDTH)
        x1, h2b, c1, e1, r2, e2 = _mix(ys, ob, ysb, ga, gs, g1, sc2, sh2, npost_mix, npre_ffn,
                                      wao, wso, wout, wq, skt, ns)
        ys = _peer(h2b, u_bf, vt_bf, c1, e1, r2, e2, x1, g2, npost_ffn, ns, 512)
        outs[4].append(k32.reshape(nb, tn_new, N_HEADS, 2 * HEAD_DIM))
        outs[5].append(v32.reshape(nb, tn_new, N_HEADS, V_DIM))
        outs[6].append(hre_s.reshape(nb, N_GROUPS, STATE_DIM))
        outs[7].append(him_s.reshape(nb, N_GROUPS, STATE_DIM))

    st = [jnp.stack(o) for o in outs]
    return (yp.reshape(bp, tp, D_MODEL), ys.reshape(nb, tn_new, D_MODEL),
            st[0], st[1], st[2], st[3], st[4], st[5], st[6], st[7])
```

```python
import functools
import math

import jax
import jax.numpy as jnp
from jax import lax
from jax.experimental import pallas as pl
from jax.experimental.pallas import tpu as pltpu

F32 = jnp.float32
BF16 = jnp.bfloat16

D_MODEL = 1024
N_HEADS = 8
HEAD_DIM = 64
V_DIM = 2 * HEAD_DIM
QK_WIDTH = N_HEADS * 2 * HEAD_DIM
ATTN_WIDTH = N_HEADS * V_DIM
ROPE_THETA = 10000.0
PAGE_SIZE = 128
SSM_WIDTH = 512
SSM_GROUP = 16
N_GROUPS = SSM_WIDTH // SSM_GROUP
STATE_DIM = 64
N_STATE = N_GROUPS * STATE_DIM
SSM_CHUNKS = 4
SSM_CHUNK_IN = SSM_WIDTH // SSM_CHUNKS
SSM_CHUNK_ST = N_STATE // SSM_CHUNKS
SCAN_ROWS = 8
N_KEYS = 128
N_EXPERTS = N_KEYS * N_KEYS
PEER_HEADS = 8
PEER_KEY_DIM = 128
PEER_TOPK = 16
N_MOD = 6
RMS_EPS = 1e-6

VMEM_LIMIT_BYTES = 56 * 1024 * 1024

NEG_INF = float("-inf")
Q_SCALE = (HEAD_DIM ** -0.5) * math.log2(math.e)
NEW_ROWS = 16
ATTN_UNROLL = 8


def _cparams(*sem):
    return pltpu.CompilerParams(dimension_semantics=sem, vmem_limit_bytes=VMEM_LIMIT_BYTES)


def _dot(a, b):
    return jnp.dot(a, b, preferred_element_type=F32)


def _dot_nt(a, b):
    return lax.dot_general(a, b, (((1,), (1,)), ((), ())), preferred_element_type=F32)


def _rms(x, w):
    return x * lax.rsqrt(jnp.mean(x * x, axis=-1, keepdims=True) + RMS_EPS) * w


def _gelu(x):
    return 0.5 * x * (1.0 + jnp.tanh(0.7978845608028654 * (x + 0.044715 * (x * x * x))))


def _sigmoid(x):
    return 1.0 / (1.0 + jnp.exp(-x))


def _mod_spec(rows_per_mod, tm):
    if rows_per_mod == 1:
        return pl.BlockSpec((1, D_MODEL), lambda i: (0, 0))
    return pl.BlockSpec((tm, D_MODEL), lambda i: (i, 0))


def _ada_kernel(c_ref, w_ref, b_ref, o_ref):
    c = c_ref[...]
    s = c * _sigmoid(c)
    o_ref[...] = _dot(s.astype(BF16), w_ref[...].astype(BF16)) + b_ref[...]


def _ada(c_all, w_ada, b_ada):
    rows = c_all.shape[0]
    tn = 1536
    return pl.pallas_call(
        _ada_kernel,
        grid=(N_MOD * D_MODEL // tn,),
        in_specs=[pl.BlockSpec((rows, D_MODEL), lambda j: (0, 0)),
                  pl.BlockSpec((D_MODEL, tn), lambda j: (0, j)),
                  pl.BlockSpec((1, tn), lambda j: (0, j))],
        out_specs=pl.BlockSpec((rows, tn), lambda j: (0, j)),
        out_shape=jax.ShapeDtypeStruct((rows, N_MOD * D_MODEL), F32),
        compiler_params=_cparams("arbitrary"),
        name="ada_mod",
    )(c_all, w_ada, b_ada.reshape(1, -1))


def _inproj_kernel(x_ref, nw_ref, sc_ref, sh_ref, cos_ref, sin_ref, w_ref,
                   q2_ref, k32_ref, kb_ref, v32_ref, vb_ref, u_ref, ga_ref, gs_ref):
    x = x_ref[...]
    h = _rms(x, nw_ref[...]) * (1.0 + sc_ref[...]) + sh_ref[...]
    hb = h.astype(BF16)
    cos = cos_ref[...]
    sin = sin_ref[...]
    lane = lax.broadcasted_iota(jnp.int32, cos.shape, 1)
    first = lane < HEAD_DIM
    low = lane % HEAD_DIM < HEAD_DIM // 2
    W = QK_WIDTH

    def rotate(x):
        partner = jnp.where(low, pltpu.roll(x, V_DIM - HEAD_DIM // 2, 1), pltpu.roll(x, HEAD_DIM // 2, 1))
        return x * cos + partner * sin

    for hd in range(N_HEADS):
        c0 = hd * V_DIM
        qr = rotate(_dot(hb, w_ref[:, c0:c0 + V_DIM])) * Q_SCALE
        q2_ref[0, :, c0:c0 + V_DIM] = jnp.where(first, qr, 0.0).astype(BF16)
        q2_ref[1, :, c0:c0 + V_DIM] = jnp.where(first, 0.0, qr).astype(BF16)
        kr = rotate(_dot(hb, w_ref[:, W + c0:W + c0 + V_DIM]))
        k32_ref[:, c0:c0 + V_DIM] = kr
        kb_ref[:, c0:c0 + V_DIM] = kr.astype(BF16)
    o = 2 * W
    v = _dot(hb, w_ref[:, o:o + ATTN_WIDTH])
    v32_ref[...] = v
    vb_ref[...] = v.astype(BF16)
    o += ATTN_WIDTH
    u_ref[...] = _dot(hb, w_ref[:, o:o + SSM_WIDTH])
    o += SSM_WIDTH
    ga_ref[...] = _dot(hb, w_ref[:, o:o + D_MODEL])
    o += D_MODEL
    gs_ref[...] = _dot(hb, w_ref[:, o:o + D_MODEL])


def _inproj(x, nw, sc, sh, cosf, sinf, w_ext, tm):
    n = x.shape[0]
    wcols = w_ext.shape[1]
    row = lambda i: (i, 0)
    full = lambda i: (0, 0)
    outs = (
        jax.ShapeDtypeStruct((2, n, QK_WIDTH), BF16),
        jax.ShapeDtypeStruct((n, QK_WIDTH), F32),
        jax.ShapeDtypeStruct((n, QK_WIDTH), BF16),
        jax.ShapeDtypeStruct((n, ATTN_WIDTH), F32),
        jax.ShapeDtypeStruct((n, ATTN_WIDTH), BF16),
        jax.ShapeDtypeStruct((n, SSM_WIDTH), F32),
        jax.ShapeDtypeStruct((n, D_MODEL), F32),
        jax.ShapeDtypeStruct((n, D_MODEL), F32),
    )
    out_specs = (
        pl.BlockSpec((2, tm, QK_WIDTH), lambda i: (0, i, 0)),
        pl.BlockSpec((tm, QK_WIDTH), row),
        pl.BlockSpec((tm, QK_WIDTH), row),
        pl.BlockSpec((tm, ATTN_WIDTH), row),
        pl.BlockSpec((tm, ATTN_WIDTH), row),
        pl.BlockSpec((tm, SSM_WIDTH), row),
        pl.BlockSpec((tm, D_MODEL), row),
        pl.BlockSpec((tm, D_MODEL), row),
    )
    return pl.pallas_call(
        _inproj_kernel,
        grid=(n // tm,),
        in_specs=[pl.BlockSpec((tm, D_MODEL), row),
                  pl.BlockSpec((1, D_MODEL), full),
                  _mod_spec(sc.shape[0], tm),
                  _mod_spec(sh.shape[0], tm),
                  pl.BlockSpec((tm, V_DIM), row),
                  pl.BlockSpec((tm, V_DIM), row),
                  pl.BlockSpec((D_MODEL, wcols), full)],
        out_specs=out_specs,
        out_shape=outs,
        compiler_params=_cparams("arbitrary"),
        name="in_proj",
    )(x, nw, sc, sh, cosf, sinf, w_ext)


def _lambda_full(lq1_ref, lk1_ref, lq2_ref, lk2_ref, lam_init):
    s1 = jnp.sum(lq1_ref[...] * lk1_ref[...], axis=-1, keepdims=True)
    s2 = jnp.sum(lq2_ref[...] * lk2_ref[...], axis=-1, keepdims=True)
    return jnp.exp(s1) - jnp.exp(s2) + lam_init


def _subln(o, w, lam_init):
    return _rms(o, w) * (1.0 - lam_init)


def _pattn_kernel(q_ref, qn_ref, k_ref, v_ref, lq1_ref, lk1_ref, lq2_ref, lk2_ref, sub_ref, o_ref,
                  s_ref, m_ref, acc_ref, *, tq, lam_init):
    i = pl.program_id(1)
    m_ref[...] = jnp.full(m_ref.shape, NEG_INF, F32)
    acc_ref[...] = jnp.zeros(acc_ref.shape, F32)
    ones = jnp.ones((tq, V_DIM), BF16)

    def produce(comp, chunk):
        start = pl.multiple_of(chunk * tq, tq)
        s_ref[comp] = _dot_nt(q_ref[comp], k_ref[pl.ds(start, tq), :])

    def consume(comp, chunk, masked):
        start = pl.multiple_of(chunk * tq, tq)
        v1 = jnp.concatenate([v_ref[pl.ds(start, tq), :], ones], axis=1)
        s = s_ref[comp]
        if masked:
            r = lax.broadcasted_iota(jnp.int32, (tq, tq), 0)
            c = lax.broadcasted_iota(jnp.int32, (tq, tq), 1)
            s = jnp.where(c <= r, s, NEG_INF)
        m_old = m_ref[comp]
        m_new = jnp.maximum(m_old, jnp.max(s, axis=-1, keepdims=True))
        p = jnp.exp2(s - jnp.tile(m_new, (1, tq // V_DIM)))
        alpha = jnp.exp2(m_old - m_new)
        acc_ref[comp] = jnp.tile(alpha, (1, 2)) * acc_ref[comp] + _dot(p.astype(BF16), v1)
        m_ref[comp] = m_new

    def pair(chunk, last_next):
        produce(1, chunk)
        consume(0, chunk, False)
        produce(0, last_next)
        consume(1, chunk, False)

    def run(first, count):
        for w in range(count):
            pair(first + w, first + w + 1)

    def body(t, carry):
        run(ATTN_UNROLL * t, ATTN_UNROLL)
        return carry

    @pl.when(i == 0)
    def _():
        produce(0, 0)

    lax.fori_loop(0, i // ATTN_UNROLL, body, 0)
    width = ATTN_UNROLL // 2
    while width >= 1:
        first = (i // (2 * width)) * (2 * width)

        @pl.when(i % (2 * width) >= width)
        def _(first=first, width=width):
            run(first, width)

        width //= 2

    produce(1, i)
    consume(0, i, True)
    s_ref[0] = _dot_nt(qn_ref[0], k_ref[pl.ds(0, tq), :])
    consume(1, i, True)

    lam = _lambda_full(lq1_ref, lk1_ref, lq2_ref, lk2_ref, lam_init)
    o1 = acc_ref[0, :, :V_DIM] / acc_ref[0, :, V_DIM:]
    o2 = acc_ref[1, :, :V_DIM] / acc_ref[1, :, V_DIM:]
    o_ref[...] = _subln(o1 - lam * o2, sub_ref[...], lam_init).astype(BF16)


def _prompt_attention(q2, kb, vb, lq1, lk1, lq2, lk2, subln, lam_init, tq):
    t = kb.shape[0]
    nq = t // tq
    vec = lambda h, i: (0, 0)
    kern = functools.partial(_pattn_kernel, tq=tq, lam_init=lam_init)
    return pl.pallas_call(
        kern,
        grid=(N_HEADS, nq),
        in_specs=[pl.BlockSpec((2, tq, V_DIM), lambda h, i: (0, i, h)),
                  pl.BlockSpec((2, tq, V_DIM), lambda h, i: (0, jnp.minimum(i + 1, nq - 1), h)),
                  pl.BlockSpec((t, V_DIM), lambda h, i: (0, h)),
                  pl.BlockSpec((t, V_DIM), lambda h, i: (0, h)),
                  pl.BlockSpec((1, HEAD_DIM), vec),
                  pl.BlockSpec((1, HEAD_DIM), vec),
                  pl.BlockSpec((1, HEAD_DIM), vec),
                  pl.BlockSpec((1, HEAD_DIM), vec),
                  pl.BlockSpec((1, V_DIM), vec)],
        out_specs=pl.BlockSpec((tq, V_DIM), lambda h, i: (i, h)),
        out_shape=jax.ShapeDtypeStruct((t, ATTN_WIDTH), BF16),
        scratch_shapes=[pltpu.VMEM((2, tq, tq), F32),
                        pltpu.VMEM((2, tq, V_DIM), F32),
                        pltpu.VMEM((2, tq, 2 * V_DIM), F32)],
        compiler_params=_cparams("arbitrary", "arbitrary"),
        name="prompt_attn",
    )(q2, q2, kb, vb, lq1, lk1, lq2, lk2, subln)


def _sattn_kernel(pt_ref, q_ref, kn_ref, vn_ref, lq1_ref, lk1_ref, lq2_ref, lk2_ref, sub_ref, *rest,
                  pages_per_step, n_new, lam_init):
    g_pages = pages_per_step
    k_refs = rest[:g_pages]
    v_refs = rest[g_pages:2 * g_pages]
    o_ref = rest[2 * g_pages]
    m_ref, l_ref, acc_ref, bias_ref = rest[2 * g_pages + 1:]
    b = pl.program_id(0)
    g = pl.program_id(1)
    q = q_ref[0]
    rows_per_head = 2 * n_new

    @pl.when((b == 0) & (g == 0))
    def _():
        r = lax.broadcasted_iota(jnp.int32, bias_ref.shape, 0) // rows_per_head
        c = lax.broadcasted_iota(jnp.int32, bias_ref.shape, 1) % N_HEADS
        bias_ref[...] = jnp.where(r == c, 0.0, NEG_INF)

    @pl.when(g == 0)
    def _():
        m_ref[...] = jnp.full(m_ref.shape, NEG_INF, F32)
        l_ref[...] = jnp.zeros(l_ref.shape, F32)
        acc_ref[...] = jnp.zeros(acc_ref.shape, F32)

    def update(s, v):
        m_old = m_ref[...]
        m_new = jnp.maximum(m_old, jnp.max(s, axis=-1, keepdims=True))
        p = jnp.exp2(s - m_new)
        alpha = jnp.exp2(m_old - m_new)
        l_ref[...] = alpha * l_ref[...] + jnp.sum(p, axis=-1, keepdims=True)
        acc_ref[...] = alpha * acc_ref[...] + _dot(p.astype(BF16), v)
        m_ref[...] = m_new

    half = g_pages // 2
    scores = []
    for grp in range(2):
        k = jnp.concatenate([r[0].astype(BF16) for r in k_refs[grp * half:(grp + 1) * half]], axis=0)
        scores.append(_dot_nt(q, k))
    for grp in range(2):
        v = jnp.concatenate([r[0].astype(BF16) for r in v_refs[grp * half:(grp + 1) * half]], axis=0)
        update(scores[grp] + bias_ref[...], v)

    @pl.when(g == pl.num_programs(1) - 1)
    def _():
        kn = kn_ref[0].astype(BF16)
        vn = vn_ref[0].astype(BF16)
        s = _dot_nt(q, kn)
        row = lax.broadcasted_iota(jnp.int32, s.shape, 0)
        col = lax.broadcasted_iota(jnp.int32, s.shape, 1)
        keep = (row // rows_per_head == col % N_HEADS) & (col // N_HEADS <= row % n_new)
        update(jnp.where(keep, s, NEG_INF), vn)
        lam = _lambda_full(lq1_ref, lk1_ref, lq2_ref, lk2_ref, lam_init)
        o = acc_ref[...] / l_ref[...]
        d = o - lam * pltpu.roll(o, o.shape[0] - n_new, 0)
        o_ref[0] = _subln(d, sub_ref[...], lam_init)


def _sample_attention(page_table, q_rows, k_new, v_new, cache_k, cache_v, lq1, lk1, lq2, lk2, subln,
                      lam_init, n_new, pages_per_step):
    nb, n_pages = page_table.shape
    gp = pages_per_step
    assert gp % 2 == 0
    rows = q_rows.shape[1]
    page_rows = PAGE_SIZE * N_HEADS
    new_rows = k_new.shape[1]
    vec = lambda b, g, pt: (0, 0)

    def page_spec(r):
        return pl.BlockSpec((1, page_rows, V_DIM),
                            lambda b, g, pt: (pt[b * n_pages + g * gp + r], 0, 0))

    kern = functools.partial(_sattn_kernel, pages_per_step=gp, n_new=n_new, lam_init=lam_init)
    grid_spec = pltpu.PrefetchScalarGridSpec(
        num_scalar_prefetch=1,
        grid=(nb, n_pages // gp),
        in_specs=[pl.BlockSpec((1, rows, V_DIM), lambda b, g, pt: (b, 0, 0)),
                  pl.BlockSpec((1, new_rows, V_DIM), lambda b, g, pt: (b, 0, 0)),
                  pl.BlockSpec((1, new_rows, V_DIM), lambda b, g, pt: (b, 0, 0)),
                  pl.BlockSpec((1, HEAD_DIM), vec),
                  pl.BlockSpec((1, HEAD_DIM), vec),
                  pl.BlockSpec((1, HEAD_DIM), vec),
                  pl.BlockSpec((1, HEAD_DIM), vec),
                  pl.BlockSpec((1, V_DIM), vec)]
                 + [page_spec(r) for r in range(gp)] + [page_spec(r) for r in range(gp)],
        out_specs=pl.BlockSpec((1, rows, V_DIM), lambda b, g, pt: (b, 0, 0)),
        scratch_shapes=[pltpu.VMEM((rows, 1), F32),
                        pltpu.VMEM((rows, 1), F32),
                        pltpu.VMEM((rows, V_DIM), F32),
                        pltpu.VMEM((rows, (gp // 2) * page_rows), F32)],
    )
    return pl.pallas_call(
        kern,
        grid_spec=grid_spec,
        out_shape=jax.ShapeDtypeStruct((nb, rows, V_DIM), F32),
        compiler_params=_cparams("arbitrary", "arbitrary"),
        name="sample_attn",
    )(page_table.reshape(-1), q_rows, k_new, v_new, lq1, lk1, lq2, lk2, subln,
      *([cache_k] * gp), *([cache_v] * gp))


def _ssm_disc_kernel(lr_ref, li_ref, ls_ref, are_ref, aim_ref, fre_ref, fim_ref, *, n_pow):
    lr = lr_ref[...]
    li = li_ref[...]
    dt = jnp.exp(ls_ref[...])
    mag = jnp.exp(lr * dt)
    a_re = mag * jnp.cos(li * dt)
    a_im = mag * jnp.sin(li * dt)
    den = lr * lr + li * li
    nr = a_re - 1.0
    ni = a_im
    fre_ref[...] = (nr * lr + ni * li) / den
    fim_ref[...] = (ni * lr - nr * li) / den
    pr, pi = a_re, a_im
    for kk in range(n_pow):
        are_ref[kk] = pr
        aim_ref[kk] = pi
        pr, pi = pr * a_re - pi * a_im, pr * a_im + pi * a_re


def _ssm_bb_kernel(fre_ref, fim_ref, bre_ref, bim_ref, ore_ref, oim_ref):
    fr = fre_ref[...]
    fi = fim_ref[...]
    br = bre_ref[...]
    bi = bim_ref[...]
    ore_ref[...] = fr * br - fi * bi
    oim_ref[...] = fr * bi + fi * br


def _ssm_prepare(lam_re, lam_im, log_step, b_re, b_im, c_re, c_im, n_pow):
    gshape = jax.ShapeDtypeStruct((N_GROUPS, STATE_DIM), F32)
    pshape = jax.ShapeDtypeStruct((n_pow, N_GROUPS, STATE_DIM), F32)
    a_re, a_im, f_re, f_im = pl.pallas_call(
        functools.partial(_ssm_disc_kernel, n_pow=n_pow),
        out_shape=(pshape, pshape, gshape, gshape),
        name="ssm_discretise",
    )(lam_re, lam_im, log_step.reshape(N_GROUPS, 1))
    bshape = jax.ShapeDtypeStruct((N_STATE, SSM_GROUP), F32)
    bb_re, bb_im = pl.pallas_call(
        _ssm_bb_kernel, out_shape=(bshape, bshape), name="ssm_input_matrix",
    )(f_re.reshape(N_STATE, 1), f_im.reshape(N_STATE, 1),
      b_re.reshape(N_STATE, SSM_GROUP), b_im.reshape(N_STATE, SSM_GROUP))
    gc = N_GROUPS // SSM_CHUNKS
    eye = jnp.eye(gc, dtype=F32)

    def in_mat(bb):
        b4 = bb.reshape(SSM_CHUNKS, gc, STATE_DIM, SSM_GROUP)
        m = jnp.einsum('cgpi,gh->cgihp', b4, eye)
        return m.reshape(SSM_CHUNKS, gc * SSM_GROUP, gc * STATE_DIM)

    def out_mat(cc):
        c4 = cc.reshape(SSM_CHUNKS, gc, SSM_GROUP, STATE_DIM)
        m = jnp.einsum('cgip,gh->cgphi', c4, eye)
        return m.reshape(SSM_CHUNKS, gc * STATE_DIM, gc * SSM_GROUP)

    b_mat = jnp.concatenate([in_mat(bb_re), in_mat(bb_im)], axis=-1).astype(BF16)
    c_mat = jnp.concatenate([out_mat(c_re), out_mat(-c_im)], axis=-2).astype(BF16)
    return (a_re.reshape(n_pow, N_STATE), a_im.reshape(n_pow, N_STATE), b_mat, c_mat)


def _shift_rows(x, s, row):
    return jnp.where(row >= s, pltpu.roll(x, s, 0), 0.0)


def _ssm_scan_kernel(u_ref, are_ref, aim_ref, b_ref, c_ref, d_ref, wg_ref,
                     y_ref, hre_ref, him_ref, cre_ref, cim_ref, *, tm, n_sub):
    @pl.when(pl.program_id(0) == 0)
    def _():
        cre_ref[...] = jnp.zeros(cre_ref.shape, F32)
        cim_ref[...] = jnp.zeros(cim_ref.shape, F32)

    for sub in range(n_sub):
        rows = pl.ds(sub * tm, tm)
        _ssm_tile(u_ref.at[rows], are_ref, aim_ref, b_ref, c_ref, d_ref, wg_ref, y_ref.at[rows],
                  cre_ref, cim_ref, tm=tm)
    hre_ref[...] = cre_ref[0:1, :]
    him_ref[...] = cim_ref[0:1, :]


def _ssm_tile(u_ref, are_ref, aim_ref, b_ref, c_ref, d_ref, wg_ref, y_ref, cre_ref, cim_ref, *, tm):
    u = u_ref[...]
    ub = u.astype(BF16)
    seg = tm // SCAN_ROWS
    pr_i = lax.broadcasted_iota(jnp.int32, (tm, tm), 0)
    pc_i = lax.broadcasted_iota(jnp.int32, (tm, tm), 1)
    perm = (pc_i == seg * (pr_i % SCAN_ROWS) + pr_i // SCAN_ROWS).astype(BF16)
    perm_t = (pr_i == seg * (pc_i % SCAN_ROWS) + pc_i // SCAN_ROWS).astype(BF16)
    ubp = _dot(perm, ub).astype(BF16)
    row = lax.broadcasted_iota(jnp.int32, (SCAN_ROWS, SSM_CHUNK_ST), 0)
    ys = []
    for c in range(SSM_CHUNKS):
        lanes = slice(c * SSM_CHUNK_ST, (c + 1) * SSM_CHUNK_ST)
        bu = _dot(ubp[:, c * SSM_CHUNK_IN:(c + 1) * SSM_CHUNK_IN], b_ref[c])
        a_r = are_ref[0:1, lanes]
        a_i = aim_ref[0:1, lanes]
        loc_r, loc_i = [], []
        x_r = x_i = None
        for gi in range(seg):
            rows = slice(gi * SCAN_ROWS, (gi + 1) * SCAN_ROWS)
            b_r = bu[rows, :SSM_CHUNK_ST]
            b_i = bu[rows, SSM_CHUNK_ST:]
            if gi == 0:
                x_r, x_i = b_r, b_i
            else:
                x_r, x_i = b_r + (a_r * x_r - a_i * x_i), b_i + (a_r * x_i + a_i * x_r)
            loc_r.append(x_r)
            loc_i.append(x_i)
        e_r, e_i = x_r, x_i
        p_r = are_ref[seg - 1:seg, lanes]
        p_i = aim_ref[seg - 1:seg, lanes]
        h_r = cre_ref[0:1, lanes]
        h_i = cim_ref[0:1, lanes]
        first = row == 0
        e_r = e_r + jnp.where(first, p_r * h_r - p_i * h_i, 0.0)
        e_i = e_i + jnp.where(first, p_r * h_i + p_i * h_r, 0.0)
        s = 1
        while s < SCAN_ROWS:
            sr = _shift_rows(e_r, s, row)
            si = _shift_rows(e_i, s, row)
            e_r, e_i = e_r + (p_r * sr - p_i * si), e_i + (p_r * si + p_i * sr)
            p_r, p_i = p_r * p_r - p_i * p_i, 2.0 * p_r * p_i
            s *= 2
        cre_ref[0:1, lanes] = e_r[SCAN_ROWS - 1:SCAN_ROWS]
        cim_ref[0:1, lanes] = e_i[SCAN_ROWS - 1:SCAN_ROWS]
        in_r = jnp.where(first, h_r, pltpu.roll(e_r, 1, 0))
        in_i = jnp.where(first, h_i, pltpu.roll(e_i, 1, 0))
        out_r, out_i = [], []
        for gi in range(seg):
            q_r = are_ref[gi:gi + 1, lanes]
            q_i = aim_ref[gi:gi + 1, lanes]
            out_r.append(loc_r[gi] + (q_r * in_r - q_i * in_i))
            out_i.append(loc_i[gi] + (q_r * in_i + q_i * in_r))
        hcat_p = jnp.concatenate([jnp.concatenate(out_r, axis=0), jnp.concatenate(out_i, axis=0)],
                                 axis=1).astype(BF16)
        hcat = _dot(perm_t, hcat_p).astype(BF16)
        ys.append(_dot(hcat, c_ref[c]))
    y = jnp.concatenate(ys, axis=1) + d_ref[...] * u
    y = _gelu(y)
    y = y * _sigmoid(_dot(y.astype(BF16), wg_ref[...]))
    y_ref[...] = y.astype(BF16)


def _ssm_prompt(u, a_re, a_im, b_mat, c_mat, d, wg, tscan, n_sub):
    t = u.shape[0]
    n_pow = a_re.shape[0]
    full2 = lambda i: (0, 0)
    full3 = lambda i: (0, 0, 0)
    kern = functools.partial(_ssm_scan_kernel, tm=tscan, n_sub=n_sub)
    tm = tscan * n_sub
    return pl.pallas_call(
        kern,
        grid=(t // tm,),
        in_specs=[pl.BlockSpec((tm, SSM_WIDTH), lambda i: (i, 0)),
                  pl.BlockSpec((n_pow, N_STATE), full2),
                  pl.BlockSpec((n_pow, N_STATE), full2),
                  pl.BlockSpec(b_mat.shape, full3),
                  pl.BlockSpec(c_mat.shape, full3),
                  pl.BlockSpec((1, SSM_WIDTH), full2),
                  pl.BlockSpec((SSM_WIDTH, SSM_WIDTH), full2)],
        out_specs=(pl.BlockSpec((tm, SSM_WIDTH), lambda i: (i, 0)),
                   pl.BlockSpec((1, N_STATE), full2),
                   pl.BlockSpec((1, N_STATE), full2)),
        out_shape=(jax.ShapeDtypeStruct((t, SSM_WIDTH), BF16),
                   jax.ShapeDtypeStruct((1, N_STATE), F32),
                   jax.ShapeDtypeStruct((1, N_STATE), F32)),
        scratch_shapes=[pltpu.VMEM((8, N_STATE), F32), pltpu.VMEM((8, N_STATE), F32)],
        compiler_params=_cparams("arbitrary"),
        name="ssm_prompt",
    )(u, a_re, a_im, b_mat, c_mat, d, wg)


def _ssm_step_kernel(u_ref, h0re_ref, h0im_ref, are_ref, aim_ref, b_ref, c_ref, d_ref, wg_ref,
                     y_ref, hre_ref, him_ref, *, n_steps):
    a_r = are_ref[0:1, :]
    a_i = aim_ref[0:1, :]
    h_r = h0re_ref[...]
    h_i = h0im_ref[...]
    for t in range(n_steps):
        u = u_ref[t]
        ub = u.astype(BF16)
        brs, bis = [], []
        for c in range(SSM_CHUNKS):
            bu = _dot(ub[:, c * SSM_CHUNK_IN:(c + 1) * SSM_CHUNK_IN], b_ref[c])
            brs.append(bu[:, :SSM_CHUNK_ST])
            bis.append(bu[:, SSM_CHUNK_ST:])
        bu_r = jnp.concatenate(brs, axis=1)
        bu_i = jnp.concatenate(bis, axis=1)
        h_r, h_i = a_r * h_r - a_i * h_i + bu_r, a_r * h_i + a_i * h_r + bu_i
        ys = []
        for c in range(SSM_CHUNKS):
            lanes = slice(c * SSM_CHUNK_ST, (c + 1) * SSM_CHUNK_ST)
            hcat = jnp.concatenate([h_r[:, lanes], h_i[:, lanes]], axis=1).astype(BF16)
            ys.append(_dot(hcat, c_ref[c]))
        y = jnp.concatenate(ys, axis=1) + d_ref[...] * u
        y = _gelu(y)
        y = y * _sigmoid(_dot(y.astype(BF16), wg_ref[...]))
        y_ref[t] = y.astype(BF16)
    hre_ref[...] = h_r
    him_ref[...] = h_i


def _ssm_sample(u_tb, h0_re, h0_im, a_re, a_im, b_mat, c_mat, d, wg):
    n_steps, nb, _ = u_tb.shape
    kern = functools.partial(_ssm_step_kernel, n_steps=n_steps)
    return pl.pallas_call(
        kern,
        out_shape=(jax.ShapeDtypeStruct((n_steps, nb, SSM_WIDTH), BF16),
                   jax.ShapeDtypeStruct((nb, N_STATE), F32),
                   jax.ShapeDtypeStruct((nb, N_STATE), F32)),
        compiler_params=pltpu.CompilerParams(vmem_limit_bytes=VMEM_LIMIT_BYTES),
        name="ssm_sample",
    )(u_tb, h0_re, h0_im, a_re, a_im, b_mat, c_mat, d, wg)


def _kth_largest_rows(c, k):
    cnt = jnp.zeros((1, c.shape[1]), F32)
    tau = jnp.full((1, c.shape[1]), NEG_INF, F32)
    for _ in range(k):
        m = jnp.max(c, axis=0, keepdims=True)
        eq = c == m
        tau = jnp.where(cnt < k, m, tau)
        cnt = cnt + jnp.sum(eq.astype(F32), axis=0, keepdims=True)
        c = jnp.where(eq, NEG_INF, c)
    return tau


def _top_rows(a, k, out_ref, want_rank=False):
    rank = jnp.full(a.shape, float(a.shape[0] - 1), F32) if want_rank else None
    for r in range(k):
        m = jnp.max(a, axis=0, keepdims=True)
        out_ref[r:r + 1, :] = m
        eq = a == m
        if want_rank:
            rank = jnp.where(eq, float(r), rank)
        a = jnp.where(eq, NEG_INF, a)
    return rank


def _mix_kernel(x_ref, o_ref, ys_ref, ga_ref, gs_ref, g1_ref, sc2_ref, sh2_ref, npost_ref, npre_ref,
                wao_ref, wso_ref, wout_ref, wq_ref, skt_ref,
                x1_ref, h2_ref, c1_ref, e1_ref, r2_ref, e2_ref, v1_ref, v2_ref, cnt_ref):
    y_attn = _dot(o_ref[...], wao_ref[...])
    y_ssm = _dot(ys_ref[...], wso_ref[...])
    merged = _sigmoid(ga_ref[...]) * y_attn + _sigmoid(gs_ref[...]) * y_ssm
    mix = _dot(merged.astype(BF16), wout_ref[...])
    x1 = x_ref[...] + g1_ref[...] * _rms(mix, npost_ref[...])
    x1_ref[...] = x1
    h2 = _rms(x1, npre_ref[...]) * (1.0 + sc2_ref[...]) + sh2_ref[...]
    h2b = h2.astype(BF16)
    h2_ref[...] = h2b
    qp = _dot(h2b, wq_ref[...]).astype(BF16)
    s_all = _dot_nt(skt_ref[...], qp)
    for hd in range(PEER_HEADS):
        r0 = hd * 2 * N_KEYS
        k0 = hd * N_KEYS
        s1 = s_all[r0:r0 + N_KEYS]
        s2 = s_all[r0 + N_KEYS:r0 + 2 * N_KEYS]
        _top_rows(s1, PEER_TOPK, v1_ref)
        r2 = _top_rows(s2, PEER_TOPK, v2_ref, want_rank=True)
        v1lo = v1_ref[0:8, :]
        v2lo = v2_ref[0:8, :]
        r8 = lax.broadcasted_iota(jnp.int32, v1lo.shape, 0)
        m1 = v1_ref[0:1, :]
        m2 = v2_ref[0:1, :]
        pieces = [
            v1lo + m2,
            v1_ref[8:16, :] + m2,
            v1lo + v2_ref[1:2, :],
            jnp.where((r8 >= 2) & (r8 <= 4), v1lo + v2_ref[2:3, :], NEG_INF),
            jnp.where((r8 >= 2) & (r8 <= 3), v1lo + v2_ref[3:4, :], NEG_INF),
            jnp.where(r8 == 2, v1lo + v2_ref[4:5, :], NEG_INF),
            jnp.where(r8 >= 2, m1 + v2lo, NEG_INF),
            m1 + v2_ref[8:16, :],
            jnp.where(r8 >= 2, v1_ref[1:2, :] + v2lo, NEG_INF),
        ]
        cand = jnp.concatenate(pieces, axis=0)
        tau = _kth_largest_rows(cand, PEER_TOPK)
        z = jnp.sum(jnp.where(cand >= tau, jnp.exp(cand - (m1 + m2)), 0.0), axis=0, keepdims=True)
        sel = [(p >= tau).astype(F32) for p in pieces]
        n0 = jnp.sum(sel[6] + sel[7], axis=0, keepdims=True)
        n1 = jnp.sum(sel[8], axis=0, keepdims=True)
        cnt_ref[0:8, :] = (sel[0] + sel[2] + sel[3] + sel[4] + sel[5]
                           + jnp.where(r8 == 0, n0, 0.0) + jnp.where(r8 == 1, n1, 0.0))
        cnt_ref[8:16, :] = sel[1]
        c1 = jnp.zeros(s1.shape, F32)
        for r in range(PEER_TOPK):
            c1 = jnp.where(s1 == v1_ref[r:r + 1, :], cnt_ref[r:r + 1, :], c1)
        c1_ref[k0:k0 + N_KEYS, :] = c1
        e1_ref[k0:k0 + N_KEYS, :] = jnp.exp(s1 - m1)
        r2_ref[k0:k0 + N_KEYS, :] = r2.astype(BF16)
        e2_ref[k0:k0 + N_KEYS, :] = (jnp.exp(s2 - m2) / z).astype(BF16)


def _mix(x, ob, ysb, ga, gs, g1, sc2, sh2, npost, npre, wao, wso, wout, wq, skt, tm):
    n = x.shape[0]
    row = lambda i: (i, 0)
    col = lambda i: (0, i)
    full = lambda i: (0, 0)
    nrow = 2 * N_KEYS * PEER_HEADS
    krow = N_KEYS * PEER_HEADS
    return pl.pallas_call(
        _mix_kernel,
        grid=(n // tm,),
        in_specs=[pl.BlockSpec((tm, D_MODEL), row),
                  pl.BlockSpec((tm, ATTN_WIDTH), row),
                  pl.BlockSpec((tm, SSM_WIDTH), row),
                  pl.BlockSpec((tm, D_MODEL), row),
                  pl.BlockSpec((tm, D_MODEL), row),
                  _mod_spec(g1.shape[0], tm),
                  _mod_spec(sc2.shape[0], tm),
                  _mod_spec(sh2.shape[0], tm),
                  pl.BlockSpec((1, D_MODEL), full),
                  pl.BlockSpec((1, D_MODEL), full),
                  pl.BlockSpec((ATTN_WIDTH, D_MODEL), full),
                  pl.BlockSpec((SSM_WIDTH, D_MODEL), full),
                  pl.BlockSpec((D_MODEL, D_MODEL), full),
                  pl.BlockSpec((D_MODEL, PEER_HEADS * PEER_KEY_DIM), full),
                  pl.BlockSpec((nrow, PEER_HEADS * PEER_KEY_DIM), full)],
        out_specs=(pl.BlockSpec((tm, D_MODEL), row),
                   pl.BlockSpec((tm, D_MODEL), row),
                   pl.BlockSpec((krow, tm), col),
                   pl.BlockSpec((krow, tm), col),
                   pl.BlockSpec((krow, tm), col),
                   pl.BlockSpec((krow, tm), col)),
        out_shape=(jax.ShapeDtypeStruct((n, D_MODEL), F32),
                   jax.ShapeDtypeStruct((n, D_MODEL), BF16),
                   jax.ShapeDtypeStruct((krow, n), F32),
                   jax.ShapeDtypeStruct((krow, n), F32),
                   jax.ShapeDtypeStruct((krow, n), BF16),
                   jax.ShapeDtypeStruct((krow, n), BF16)),
        scratch_shapes=[pltpu.VMEM((PEER_TOPK, tm), F32)] * 3,
        compiler_params=_cparams("arbitrary"),
        name="mix_route",
    )(x, ob, ysb, ga, gs, g1, sc2, sh2, npost, npre, wao, wso, wout, wq, skt)


BF16_ROWS = 16
PEER_CHUNKS_PER_STEP = 2


def _tile_bf16(row):
    return jnp.broadcast_to(row, (BF16_ROWS, row.shape[1])).astype(BF16)


def _peer_kernel(h2_ref, u0_ref, *rest, ec, n_sub):
    u_refs = rest[:n_sub]
    (vt_ref, c1_ref, e1_ref, r2_ref, e2_ref, x1_ref, g2_ref, nw_ref,
     y_ref, acc_ref, act0_ref, act1_ref) = rest[n_sub:]
    acts = (act0_ref, act1_ref)
    g = pl.program_id(1)
    h2 = h2_ref[...]
    blocks = ec // N_KEYS

    @pl.when(g == 0)
    def _():
        acc_ref[...] = jnp.zeros(acc_ref.shape, F32)
        act0_ref[...] = _dot_nt(u0_ref[...], h2)

    def consume(act_ref, chunk, col0):
        ws = []
        for ii in range(blocks):
            key1 = chunk * blocks + ii
            a = _gelu(act_ref[ii * N_KEYS:(ii + 1) * N_KEYS, :].astype(BF16))
            tiles = N_KEYS // BF16_ROWS
            totals = [None] * tiles
            for hd in range(PEER_HEADS):
                cnt = _tile_bf16(c1_ref[pl.ds(hd * N_KEYS + key1, 1), :])
                e1 = _tile_bf16(e1_ref[pl.ds(hd * N_KEYS + key1, 1), :])
                for t in range(tiles):
                    k0 = hd * N_KEYS + t * BF16_ROWS
                    r2 = r2_ref[k0:k0 + BF16_ROWS, :]
                    e2 = e2_ref[k0:k0 + BF16_ROWS, :]
                    term = jnp.where(r2 < cnt, e2, jnp.zeros_like(e2)) * e1
                    totals[t] = term if totals[t] is None else totals[t] + term
            for t in range(tiles):
                ws.append(totals[t] * a[t * BF16_ROWS:(t + 1) * BF16_ROWS])
        w = jnp.concatenate(ws, axis=0)
        acc_ref[...] += _dot(vt_ref[:, col0:col0 + ec], w)

    for s in range(n_sub):
        acts[(s + 1) % 2][...] = _dot_nt(u_refs[s][...], h2)
        consume(acts[s % 2], n_sub * g + s, s * ec)

    @pl.when(g == pl.num_programs(1) - 1)
    def _():
        f = acc_ref[...].T
        y_ref[...] = x1_ref[...] + g2_ref[...] * _rms(f, nw_ref[...])


def _peer(h2b, u_bf, vt_bf, c1, e1, r2, e2, x1, g2, nw, tn, ec):
    n = h2b.shape[0]
    krow = c1.shape[0]
    n_chunks = N_EXPERTS // ec
    n_sub = PEER_CHUNKS_PER_STEP
    kern = functools.partial(_peer_kernel, ec=ec, n_sub=n_sub)
    g2_spec = (pl.BlockSpec((1, D_MODEL), lambda t, c: (0, 0)) if g2.shape[0] == 1
               else pl.BlockSpec((tn, D_MODEL), lambda t, c: (t, 0)))

    def lookahead_spec(s):
        return pl.BlockSpec((ec, D_MODEL), lambda t, c: (jnp.minimum(n_sub * c + s + 1, n_chunks - 1), 0))

    return pl.pallas_call(
        kern,
        grid=(n // tn, n_chunks // n_sub),
        in_specs=[pl.BlockSpec((tn, D_MODEL), lambda t, c: (t, 0)),
                  pl.BlockSpec((ec, D_MODEL), lambda t, c: (0, 0))]
                 + [lookahead_spec(s) for s in range(n_sub)]
                 + [pl.BlockSpec((D_MODEL, n_sub * ec), lambda t, c: (0, c)),
                  pl.BlockSpec((krow, tn), lambda t, c: (0, t)),
                  pl.BlockSpec((krow, tn), lambda t, c: (0, t)),
                  pl.BlockSpec((krow, tn), lambda t, c: (0, t)),
                  pl.BlockSpec((krow, tn), lambda t, c: (0, t)),
                  pl.BlockSpec((tn, D_MODEL), lambda t, c: (t, 0)),
                  g2_spec,
                  pl.BlockSpec((1, D_MODEL), lambda t, c: (0, 0))],
        out_specs=pl.BlockSpec((tn, D_MODEL), lambda t, c: (t, 0)),
        out_shape=jax.ShapeDtypeStruct((n, D_MODEL), F32),
        scratch_shapes=[pltpu.VMEM((D_MODEL, tn), F32),
                        pltpu.VMEM((ec, tn), F32), pltpu.VMEM((ec, tn), F32)],
        compiler_params=_cparams("arbitrary", "arbitrary"),
        name="peer_experts",
    )(h2b, u_bf, *([u_bf] * n_sub), vt_bf, c1, e1, r2, e2, x1, g2, nw)


def _rope_tables(pos):
    half = HEAD_DIM // 2
    inv = ROPE_THETA ** (-jnp.arange(half, dtype=F32) / half)
    ang = pos.astype(F32)[:, None] * inv[None, :]
    cos = jnp.cos(ang)
    sin = jnp.sin(ang)
    cosf = jnp.concatenate([cos, cos, cos, cos], axis=1)
    sinf = jnp.concatenate([-sin, sin, -sin, sin], axis=1)
    return cosf, sinf


def _tile(n, pref):
    t = min(n, pref)
    assert n % t == 0, (n, pref)
    return t


def kernel(x_prompt, x_sample, cache_k, cache_v, state_ssm_re, state_ssm_im, page_table, c_prompt, c_sample, w_ada, b_ada, norm_pre_mix, norm_post_mix, norm_pre_ffn, norm_post_ffn, w_in, lambda_q1, lambda_k1, lambda_q2, lambda_k2, subln, w_attn_out, ssm_lambda_re, ssm_lambda_im, ssm_log_step, ssm_b_re, ssm_b_im, ssm_c_re, ssm_c_im, ssm_d, w_glu, w_ssm_out, w_out, peer_w_query, peer_sub_keys, peer_u, peer_v):
    depth = w_ada.shape[0]
    bp, tp, _ = x_prompt.shape
    nb, tn_new, _ = x_sample.shape
    assert bp == 1 and 2 * tn_new == 8
    n_pages = page_table.shape[1]
    past_len = n_pages * PAGE_SIZE
    n_phys = cache_k.shape[1]
    ns = nb * tn_new

    pos_p = jnp.arange(tp, dtype=jnp.int32)
    pos_s = jnp.tile(past_len + jnp.arange(tn_new, dtype=jnp.int32), nb)
    cos_p, sin_p = _rope_tables(pos_p)
    cos_s, sin_s = _rope_tables(pos_s)

    tm_p = _tile(tp, 256)
    tq = _tile(tp, 512)
    tscan = _tile(tp, 128)
    assert tscan % SCAN_ROWS == 0
    scan_tiles = 2 if tp % (2 * tscan) == 0 else 1
    pages_per_step = next(p for p in (16, 8, 4, 2) if n_pages % p == 0)

    yp = x_prompt.reshape(tp, D_MODEL)
    ys = x_sample.reshape(ns, D_MODEL)
    c_all = jnp.concatenate([c_prompt, c_sample], axis=0)
    pad = (-c_all.shape[0]) % 8
    c_all = jnp.pad(c_all, ((0, pad), (0, 0)))

    outs = [[] for _ in range(8)]
    for l in range(depth):
        lam_init = 0.8 - 0.6 * math.exp(-0.3 * l)
        mod = _ada(c_all, w_ada[l], b_ada[l])
        mod_p = [mod[0:1, j * D_MODEL:(j + 1) * D_MODEL] for j in range(N_MOD)]
        mod_s = [jnp.repeat(mod[1:1 + nb, j * D_MODEL:(j + 1) * D_MODEL], tn_new, axis=0)
                 for j in range(N_MOD)]

        w_ext = w_in[l].astype(BF16)
        a_re, a_im, b_mat, c_mat = _ssm_prepare(ssm_lambda_re[l], ssm_lambda_im[l], ssm_log_step[l],
                                                ssm_b_re[l], ssm_b_im[l], ssm_c_re[l], ssm_c_im[l],
                                                tscan // SCAN_ROWS)
        d_row = ssm_d[l].reshape(1, SSM_WIDTH)
        wg = w_glu[l].astype(BF16)
        wao = w_attn_out[l].astype(BF16)
        wso = w_ssm_out[l].astype(BF16)
        wout = w_out[l].astype(BF16)
        wq = peer_w_query[l].astype(BF16)
        sk = peer_sub_keys[l]
        eye = jnp.eye(PEER_HEADS * 2, dtype=F32)
        skt = jnp.einsum('bkd,bc->bkcd', sk.reshape(PEER_HEADS * 2, N_KEYS, PEER_KEY_DIM // 2), eye)
        skt = skt.reshape(PEER_HEADS * 2 * N_KEYS, PEER_HEADS * PEER_KEY_DIM).astype(BF16)
        u_bf = peer_u[l].astype(BF16)
        vt_bf = peer_v[l].T.astype(BF16)
        lq1 = lambda_q1[l].reshape(1, HEAD_DIM)
        lk1 = lambda_k1[l].reshape(1, HEAD_DIM)
        lq2 = lambda_q2[l].reshape(1, HEAD_DIM)
        lk2 = lambda_k2[l].reshape(1, HEAD_DIM)
        sub = subln[l].reshape(1, V_DIM)
        npre_mix = norm_pre_mix[l].reshape(1, D_MODEL)
        npost_mix = norm_post_mix[l].reshape(1, D_MODEL)
        npre_ffn = norm_pre_ffn[l].reshape(1, D_MODEL)
        npost_ffn = norm_post_ffn[l].reshape(1, D_MODEL)

        sh1, sc1, g1, sh2, sc2, g2 = mod_p
        q2, k32, kb, v32, vb, u, ga, gs = _inproj(yp, npre_mix, sc1, sh1, cos_p, sin_p, w_ext, tm_p)
        ob = _prompt_attention(q2, kb, vb, lq1, lk1, lq2, lk2, sub, lam_init, tq)
        ysb, hre, him = _ssm_prompt(u, a_re, a_im, b_mat, c_mat, d_row, wg, tscan, scan_tiles)
        x1, h2b, c1, e1, r2, e2 = _mix(yp, ob, ysb, ga, gs, g1, sc2, sh2, npost_mix, npre_ffn,
                                      wao, wso, wout, wq, skt, tm_p)
        yp = _peer(h2b, u_bf, vt_bf, c1, e1, r2, e2, x1, g2, npost_ffn, _tile(tp, 512), 512)
        outs[0].append(k32.reshape(1, tp, N_HEADS, 2 * HEAD_DIM))
        outs[1].append(v32.reshape(1, tp, N_HEADS, V_DIM))
        outs[2].append(hre.reshape(1, N_GROUPS, STATE_DIM))
        outs[3].append(him.reshape(1, N_GROUPS, STATE_DIM))

        sh1, sc1, g1, sh2, sc2, g2 = mod_s
        q2, k32, _, v32, _, u, ga, gs = _inproj(ys, npre_mix, sc1, sh1, cos_s, sin_s, w_ext, ns)
        q_rows = q2.reshape(2, nb, tn_new, N_HEADS, V_DIM).transpose(1, 3, 0, 2, 4)
        q_rows = q_rows.reshape(nb, N_HEADS * 2 * tn_new, V_DIM)
        k_new = jnp.pad(k32.reshape(nb, tn_new * N_HEADS, V_DIM),
                        ((0, 0), (0, (NEW_ROWS - tn_new) * N_HEADS), (0, 0)))
        v_new = jnp.pad(v32.reshape(nb, tn_new * N_HEADS, V_DIM),
                        ((0, 0), (0, (NEW_ROWS - tn_new) * N_HEADS), (0, 0)))
        ck = cache_k.reshape(depth * n_phys, PAGE_SIZE * N_HEADS, V_DIM)
        cv = cache_v.reshape(depth * n_phys, PAGE_SIZE * N_HEADS, V_DIM)
        o_s = _sample_attention(page_table + l * n_phys, q_rows, k_new, v_new, ck, cv,
                                lq1, lk1, lq2, lk2, sub, lam_init, tn_new, pages_per_step)
        ob = o_s.reshape(nb, N_HEADS, 2, tn_new, V_DIM)[:, :, 0].transpose(0, 2, 1, 3)
        ob = ob.reshape(ns, ATTN_WIDTH).astype(BF16)
        u_tb = u.reshape(nb, tn_new, SSM_WIDTH).transpose(1, 0, 2)
        y_tb, hre_s, him_s = _ssm_sample(u_tb, state_ssm_re[l].reshape(nb, N_STATE),
                                         state_ssm_im[l].reshape(nb, N_STATE),
                                         a_re, a_im, b_mat, c_mat, d_row, wg)
        ysb = y_tb.transpose(1, 0, 2).reshape(ns, SSM_WIDTH)
        x1, h2b, c1, e1, r2, e2 = _mix(ys, ob, ysb, ga, gs, g1, sc2, sh2, npost_mix, npre_ffn,
                                      wao, wso, wout, wq, skt, ns)
        ys = _peer(h2b, u_bf, vt_bf, c1, e1, r2, e2, x1, g2, npost_ffn, ns, 512)
        outs[4].append(k32.reshape(nb, tn_new, N_HEADS, 2 * HEAD_DIM))
        outs[5].append(v32.reshape(nb, tn_new, N_HEADS, V_DIM))
        outs[6].append(hre_s.reshape(nb, N_GROUPS, STATE_DIM))
        outs[7].append(him_s.reshape(nb, N_GROUPS, STATE_DIM))

    st = [jnp.stack(o) for o in outs]
    return (yp.reshape(bp, tp, D_MODEL), ys.reshape(nb, tn_new, D_MODEL),
            st[0], st[1], st[2], st[3], st[4], st[5], st[6], st[7])
```
